```python
import functools
import jax, jax.numpy as jnp
from jax import lax
import numpy as np

D_MODEL = 1024
BATCH = 8
SEQ = 4096
DEPTH = 1
DEC_BATCH = 32
DEC_SEQ = 4
PAST_LEN = 16384
PAGE_SIZE = 128

HEAD_DIM = 64
N_HEADS_A = 8
N_HEADS_B = 8
C_A = N_HEADS_A * HEAD_DIM
C_B = N_HEADS_B * HEAD_DIM
DILATIONS = ((128, 1), (512, 4), (2048, 16))
WINDOW_MAX = 2048
Q_BLOCK = 128
D_W_LORA = 64
D_A_LORA = 64
D_G_LORA = 128
D_B_IN = 3 * C_B + D_W_LORA + D_A_LORA + D_G_LORA
D_IN = 3 * C_A + D_B_IN
D_FF = ((8 * D_MODEL + 3 * 256 - 1) // (3 * 256)) * 256
PLE_DIM = 256
ALPHA = (2 * DEPTH) ** 0.25
BETA = (8 * DEPTH) ** -0.25
LN_EPS = 1e-5
GN_EPS = 64e-5

kernel_name = "hymba_longnet_rwkv7_decoder_step"


def layer_norm(x, w, b):
    xf = x.astype(jnp.float32)
    mu = jnp.mean(xf, axis=-1, keepdims=True)
    var = jnp.mean(jnp.square(xf - mu), axis=-1, keepdims=True)
    return ((xf - mu) * lax.rsqrt(var + LN_EPS) * w + b).astype(x.dtype)


def alibi_slopes():
    h = jnp.arange(1, N_HEADS_A + 1, dtype=jnp.float32)
    return jnp.exp2(-8.0 * h / N_HEADS_A)


def dilated_attn_prompt(q, k, v, window, dil):
    B, T, H, Dh = q.shape
    span = window // dil
    M = T // dil
    nb = -(-M // Q_BLOCK)
    Mp = nb * Q_BLOCK

    def to_blocks(a):
        a = a.astype(jnp.float32).reshape(B, M, dil, H, Dh).transpose(0, 2, 1, 3, 4)
        a = jnp.pad(a, ((0, 0), (0, 0), (0, Mp - M), (0, 0), (0, 0)))
        return a.reshape(B, dil, nb, Q_BLOCK, H, Dh)

    def with_prev(a):
        prev = jnp.pad(a, ((0, 0), (0, 0), (1, 0), (0, 0), (0, 0), (0, 0)))[:, :, :-1]
        return jnp.concatenate([prev, a], axis=3)

    qb = to_blocks(q) * (HEAD_DIM ** -0.5)
    kb = with_prev(to_blocks(k))
    vb = with_prev(to_blocks(v))
    qi = jnp.arange(Q_BLOCK)[:, None]
    ki = jnp.arange(2 * Q_BLOCK)[None, :]
    dist = Q_BLOCK + qi - ki
    band = (dist >= 0) & (dist <= span)
    exists = (jnp.arange(nb)[:, None, None] > 0) | (ki[None] >= Q_BLOCK)
    valid = band[None] & exists
    bias = -alibi_slopes()[:, None, None] * (dist * dil).astype(jnp.float32)
    s = jnp.einsum("brnqhd,brnkhd->brnhqk", qb, kb) + bias
    s = jnp.where(valid[None, None, :, None], s, -jnp.inf)
    m = jnp.max(s, axis=-1, keepdims=True)
    p = jnp.exp(s - m)
    den = jnp.sum(p, axis=-1)
    o = jnp.einsum("brnhqk,brnkhd->brnqhd", p, vb) / jnp.swapaxes(den, -1, -2)[..., None]
    lse = jnp.swapaxes(m[..., 0] + jnp.log(den), -1, -2)
    o = o.reshape(B, dil, Mp, H, Dh)[:, :, :M].transpose(0, 2, 1, 3, 4).reshape(B, T, H, Dh)
    lse = lse.reshape(B, dil, Mp, H)[:, :, :M].transpose(0, 2, 1, 3).reshape(B, T, H)
    return o, lse


def dilated_attn_sample(q, kc, vc, window, dil):
    B, S, H, Dh = q.shape
    L = kc.shape[1] - S
    off = jnp.arange(window // dil + 1) * dil
    idx = L + jnp.arange(S)[:, None] - off[None, :]
    valid = idx >= 0
    idx = jnp.maximum(idx, 0)
    kg = kc[:, idx].astype(jnp.float32)
    vg = vc[:, idx].astype(jnp.float32)
    bias = -alibi_slopes()[:, None, None] * off.astype(jnp.float32)[None, None, :]
    s = jnp.einsum("bshd,bskhd->bhsk", q.astype(jnp.float32) * (HEAD_DIM ** -0.5), kg) + bias
    s = jnp.where(valid[None, None], s, -jnp.inf)
    m = jnp.max(s, axis=-1, keepdims=True)
    p = jnp.exp(s - m)
    den = jnp.sum(p, axis=-1)
    o = jnp.einsum("bhsk,bskhd->bshd", p, vg) / jnp.swapaxes(den, 1, 2)[..., None]
    lse = jnp.swapaxes(m[..., 0] + jnp.log(den), 1, 2)
    return o, lse


def merge_dilations(results):
    o = jnp.stack([r[0] for r in results])
    lse = jnp.stack([r[1] for r in results])
    wts = jax.nn.softmax(lse, axis=0)
    return jnp.einsum("gbth,gbthd->bthd", wts, o)


def prompt_attention(q, k, v):
    o = merge_dilations([dilated_attn_prompt(q, k, v, w, d) for (w, d) in DILATIONS])
    return o, k[:, -WINDOW_MAX:], v[:, -WINDOW_MAX:]


def sample_attention(cache_k, cache_v, q, k, v):
    S = q.shape[1]
    kc = jnp.concatenate([cache_k.astype(k.dtype), k], axis=1)
    vc = jnp.concatenate([cache_v.astype(v.dtype), v], axis=1)
    o = merge_dilations([dilated_attn_sample(q, kc, vc, w, d) for (w, d) in DILATIONS])
    return o, kc[:, S:], vc[:, S:]


def rwkv7_mix(zb, shift_prev, wkv_prev, mu_shift, w0, w2, a0, a2, g2, k_k, k_a, r_k, lnx_w, lnx_b):
    B, T, _ = zb.shape
    H, N = N_HEADS_B, HEAD_DIM
    prev = jnp.concatenate([shift_prev[:, None].astype(zb.dtype), zb[:, :-1]], axis=1)
    zm = (zb + (prev - zb) * mu_shift).astype(jnp.float32)
    r = zm[..., :C_B]
    k = zm[..., C_B:2 * C_B]
    v = zm[..., 2 * C_B:3 * C_B]
    w_in = zm[..., 3 * C_B:3 * C_B + D_W_LORA]
    a_in = zm[..., 3 * C_B + D_W_LORA:3 * C_B + D_W_LORA + D_A_LORA]
    g_in = zm[..., 3 * C_B + D_W_LORA + D_A_LORA:]
    w_log = -jax.nn.softplus(-(w0 + jnp.tanh(w_in) @ w2)) - 0.5
    decay = jnp.exp(-jnp.exp(w_log))
    a = jax.nn.sigmoid(a0 + a_in @ a2)
    g = jax.nn.sigmoid(g_in) @ g2
    kk = (k * k_k).reshape(B, T, H, N)
    kk = kk / jnp.maximum(jnp.sqrt(jnp.sum(kk * kk, axis=-1, keepdims=True)), 1e-12)
    k = k * (1.0 + (a - 1.0) * k_a)
    rh, kh, vh, dh, ah = (t.reshape(B, T, H, N) for t in (r, k, v, decay, a))

    def step(S, inp):
        r_t, k_t, v_t, d_t, kk_t, a_t = inp
        sa = jnp.einsum("bhij,bhj->bhi", S, -kk_t)
        S = S * d_t[:, :, None, :] + sa[..., None] * (kk_t * a_t)[:, :, None, :] \
            + v_t[..., None] * k_t[:, :, None, :]
        return S, jnp.einsum("bhij,bhj->bhi", S, r_t)

    xs = tuple(jnp.swapaxes(t, 0, 1) for t in (rh, kh, vh, dh, kk, ah))
    S_last, ys = lax.scan(step, wkv_prev.astype(jnp.float32), xs)
    y = jnp.swapaxes(ys, 0, 1)
    ym = jnp.mean(y, axis=-1, keepdims=True)
    yv = jnp.mean(jnp.square(y - ym), axis=-1, keepdims=True)
    yn = ((y - ym) * lax.rsqrt(yv + GN_EPS)).reshape(B, T, C_B) * lnx_w + lnx_b
    bonus = jnp.sum(rh * kh * r_k, axis=-1, keepdims=True) * vh
    out = (yn + bonus.reshape(B, T, C_B)) * g
    return out.astype(zb.dtype), zb[:, -1], S_last.astype(wkv_prev.dtype)


def hybrid_layer(x, p_l, attend, shift_prev, wkv_prev, w_in, mu_shift, w0, w2, a0, a2, g2,
                 k_k, k_a, r_k, lnx_w, lnx_b, w_out, ln1_w, ln1_b, w_gate, w_up, w_down,
                 ln2_w, ln2_b, w_ple_gate, w_ple_proj, ln3_w, ln3_b):
    B, T, _ = x.shape
    z = jnp.einsum("btd,de->bte", x, w_in)
    q = z[..., :C_A].reshape(B, T, N_HEADS_A, HEAD_DIM)
    k = z[..., C_A:2 * C_A].reshape(B, T, N_HEADS_A, HEAD_DIM)
    v = z[..., 2 * C_A:3 * C_A].reshape(B, T, N_HEADS_A, HEAD_DIM)
    att, k_win, v_win = attend(q, k, v)
    rw, shift_new, wkv_new = rwkv7_mix(z[..., 3 * C_A:], shift_prev, wkv_prev, mu_shift, w0, w2,
                                       a0, a2, g2, k_k, k_a, r_k, lnx_w, lnx_b)
    heads = jnp.concatenate([att.reshape(B, T, C_A).astype(x.dtype), rw], axis=-1)
    mix = jnp.einsum("btc,cd->btd", heads, w_out)
    h = layer_norm(ALPHA * x + mix, ln1_w, ln1_b)
    ffn = jnp.einsum("btf,fd->btd", jax.nn.silu(h @ w_gate) * (h @ w_up), w_down)
    h = layer_norm(ALPHA * h + ffn, ln2_w, ln2_b)
    ple = jax.nn.sigmoid(h @ w_ple_gate) * (p_l @ w_ple_proj)
    y = layer_norm(ALPHA * h + ple, ln3_w, ln3_b)
    return y, k_win, v_win, shift_new, wkv_new


def setup_inputs(seed: int = 0) -> dict:
    key = jax.random.key(seed)
    ks = iter(jax.random.split(key, 40))

    def nrm(shape, s=1.0):
        return s * jax.random.normal(next(ks), shape, jnp.float32)

    L = DEPTH
    win_buf = min(WINDOW_MAX, PAST_LEN)
    return {
        "x_prompt": nrm((BATCH, SEQ, D_MODEL)),
        "x_sample": nrm((DEC_BATCH, DEC_SEQ, D_MODEL)),
        "p_prompt": nrm((L, BATCH, SEQ, PLE_DIM)),
        "p_sample": nrm((L, DEC_BATCH, DEC_SEQ, PLE_DIM)),
        "cache_k_win": nrm((L, DEC_BATCH, win_buf, N_HEADS_A, HEAD_DIM)),
        "cache_v_win": nrm((L, DEC_BATCH, win_buf, N_HEADS_A, HEAD_DIM)),
        "state_wkv": nrm((L, DEC_BATCH, N_HEADS_B, HEAD_DIM, HEAD_DIM), 0.2),
        "state_shift": nrm((L, DEC_BATCH, D_B_IN)),
        "w_in": nrm((L, D_MODEL, D_IN), D_MODEL ** -0.5),
        "mu_shift": jax.random.uniform(next(ks), (L, D_B_IN), jnp.float32),
        "w0": jax.random.uniform(next(ks), (L, C_B), jnp.float32, minval=-6.0, maxval=-1.0),
        "w2": nrm((L, D_W_LORA, C_B), 0.1),
        "a0": nrm((L, C_B), 0.1),
        "a2": nrm((L, D_A_LORA, C_B), 0.1),
        "g2": nrm((L, D_G_LORA, C_B), D_G_LORA ** -0.5),
        "k_k": 0.85 + nrm((L, C_B), 0.05),
        "k_a": 1.0 + nrm((L, C_B), 0.05),
        "r_k": nrm((L, N_HEADS_B, HEAD_DIM), 0.1),
        "lnx_w": 1.0 + nrm((L, C_B), 0.02),
        "lnx_b": nrm((L, C_B), 0.02),
        "w_out": nrm((L, C_A + C_B, D_MODEL), BETA * (C_A + C_B) ** -0.5),
        "ln1_w": 1.0 + nrm((L, D_MODEL), 0.02),
        "ln1_b": nrm((L, D_MODEL), 0.02),
        "w_gate": nrm((L, D_MODEL, D_FF), D_MODEL ** -0.5),
        "w_up": nrm((L, D_MODEL, D_FF), D_MODEL ** -0.5),
        "w_down": nrm((L, D_FF, D_MODEL), BETA * D_FF ** -0.5),
        "ln2_w": 1.0 + nrm((L, D_MODEL), 0.02),
        "ln2_b": nrm((L, D_MODEL), 0.02),
        "w_ple_gate": nrm((L, D_MODEL, D_MODEL), D_MODEL ** -0.5),
        "w_ple_proj": nrm((L, PLE_DIM, D_MODEL), BETA * PLE_DIM ** -0.5),
        "ln3_w": 1.0 + nrm((L, D_MODEL), 0.02),
        "ln3_b": nrm((L, D_MODEL), 0.02),
    }


def reference(x_prompt, x_sample, p_prompt, p_sample, cache_k_win, cache_v_win, state_wkv,
              state_shift, w_in, mu_shift, w0, w2, a0, a2, g2, k_k, k_a, r_k, lnx_w, lnx_b,
              w_out, ln1_w, ln1_b, w_gate, w_up, w_down, ln2_w, ln2_b, w_ple_gate, w_ple_proj,
              ln3_w, ln3_b):
    xp, xs = x_prompt, x_sample
    bp = x_prompt.shape[0]
    shift0 = jnp.zeros((bp, D_B_IN), x_prompt.dtype)
    wkv0 = jnp.zeros((bp, N_HEADS_B, HEAD_DIM, HEAD_DIM), state_wkv.dtype)
    kp_l, vp_l, wp_l, sp_l, ks_l, vs_l, ws_l, ss_l = [], [], [], [], [], [], [], []
    for l in range(DEPTH):
        lw = (w_in[l], mu_shift[l], w0[l], w2[l], a0[l], a2[l], g2[l], k_k[l], k_a[l], r_k[l],
              lnx_w[l], lnx_b[l], w_out[l], ln1_w[l], ln1_b[l], w_gate[l], w_up[l], w_down[l],
              ln2_w[l], ln2_b[l], w_ple_gate[l], w_ple_proj[l], ln3_w[l], ln3_b[l])
        xp, kw, vw, sh, wk = hybrid_layer(xp, p_prompt[l], prompt_attention, shift0, wkv0, *lw)
        kp_l.append(kw); vp_l.append(vw); sp_l.append(sh); wp_l.append(wk)
        attend_s = functools.partial(sample_attention, cache_k_win[l], cache_v_win[l])
        xs, kw, vw, sh, wk = hybrid_layer(xs, p_sample[l], attend_s, state_shift[l], state_wkv[l], *lw)
        ks_l.append(kw); vs_l.append(vw); ss_l.append(sh); ws_l.append(wk)
    return (xp, xs, jnp.stack(kp_l), jnp.stack(vp_l), jnp.stack(wp_l), jnp.stack(sp_l),
            jnp.stack(ks_l), jnp.stack(vs_l), jnp.stack(ws_l), jnp.stack(ss_l))
```

```python
import functools

import jax
import jax.numpy as jnp
from jax import lax
from jax.experimental import pallas as pl
from jax.experimental.pallas import tpu as pltpu

BF = jnp.bfloat16
F32 = jnp.float32

D_MODEL = 1024
HEAD_DIM = 64
N_HEADS = 8
C_HEADS = N_HEADS * HEAD_DIM
PAIR = 2 * HEAD_DIM
N_PAIRS = N_HEADS // 2
DILATIONS = ((128, 1), (512, 4), (2048, 16))
WINDOW_MAX = 2048
Q_BLOCK = 128
D_LORA_WA = 128
D_G_LORA = 128
D_B_IN = 3 * C_HEADS + D_LORA_WA + D_G_LORA
D_IN = 3 * C_HEADS + D_B_IN
D_FF = 2816
PLE_DIM = 256
LN_EPS = 1e-5
GN_EPS = 64e-5
NEG_BIG = -1e30

CHUNK = 16
SUPER = 64
N_CHUNKS = SUPER // CHUNK
FF_CHUNK = 256
IN_CHUNK = 256
VMEM_LIMIT_BYTES = 56 * 1024 * 1024

(_V_MU_R, _V_MU_K, _V_MU_V, _V_W0, _V_A0, _V_KK, _V_KA, _V_RK, _V_LNW, _V_LNB) = range(10)
_V_ROWS = 16


def _params(*sem):
    return pltpu.CompilerParams(dimension_semantics=sem, vmem_limit_bytes=VMEM_LIMIT_BYTES)


def _const_spec(shape):
    nd = len(shape)
    return pl.BlockSpec(shape, lambda *_: (0,) * nd, pipeline_mode=pl.Buffered(1))


def _dot(a, b):
    return jnp.dot(a.astype(BF), b.astype(BF), preferred_element_type=F32)


def _dot_nt(a, b):
    return lax.dot_general(a.astype(BF), b.astype(BF), (((1,), (1,)), ((), ())),
                           preferred_element_type=F32)


def _split(x):
    hi = x.astype(BF)
    lo = (x - hi.astype(F32)).astype(BF)
    return hi, lo


def _dot_exact_lhs(mask_bf, x):
    hi, lo = _split(x)
    return (jnp.dot(mask_bf, hi, preferred_element_type=F32)
            + jnp.dot(mask_bf, lo, preferred_element_type=F32))


def _dot_exact_rhs(x, mask_bf):
    hi, lo = _split(x)
    return (jnp.dot(hi, mask_bf, preferred_element_type=F32)
            + jnp.dot(lo, mask_bf, preferred_element_type=F32))


def _layer_norm(x, w, b):
    mu = jnp.mean(x, axis=-1, keepdims=True)
    xc = x - mu
    var = jnp.mean(xc * xc, axis=-1, keepdims=True)
    return xc * lax.rsqrt(var + LN_EPS) * w + b


def _in_proj_kernel(x_ref, w_ref, o_ref):
    xb = x_ref[...].astype(BF)
    for j in range(0, D_IN, IN_CHUNK):
        o_ref[:, j:j + IN_CHUNK] = jnp.dot(xb, w_ref[:, j:j + IN_CHUNK],
                                           preferred_element_type=F32)


def _in_proj(x2d, w_bf):
    m = x2d.shape[0]
    tm = min(512, m)
    assert m % tm == 0
    return pl.pallas_call(
        _in_proj_kernel,
        out_shape=jax.ShapeDtypeStruct((m, D_IN), F32),
        grid=(m // tm,),
        in_specs=[pl.BlockSpec((tm, D_MODEL), lambda i: (i, 0)),
                  _const_spec((D_MODEL, D_IN))],
        out_specs=pl.BlockSpec((tm, D_IN), lambda i: (i, 0)),
        compiler_params=_params("arbitrary"),
        name="in_proj",
    )(x2d, w_bf)


def _post_kernel(att_ref, rw_ref, x_ref, p_ref, wo_ref, wg_ref, wu_ref, wd_ref, wpg_ref,
                 wpp_ref, ln_ref, o_ref, *, alpha):
    mix = (_dot(att_ref[...], wo_ref[0:C_HEADS, :])
           + _dot(rw_ref[...], wo_ref[C_HEADS:2 * C_HEADS, :]))
    h = _layer_norm(alpha * x_ref[...] + mix, ln_ref[0:1, :], ln_ref[1:2, :])
    hb = h.astype(BF)
    ffn = jnp.zeros_like(h)
    for j in range(0, D_FF, FF_CHUNK):
        g = jnp.dot(hb, wg_ref[:, j:j + FF_CHUNK], preferred_element_type=F32)
        u = jnp.dot(hb, wu_ref[:, j:j + FF_CHUNK], preferred_element_type=F32)
        act = g * jax.nn.sigmoid(g) * u
        ffn = ffn + jnp.dot(act.astype(BF), wd_ref[j:j + FF_CHUNK, :],
                            preferred_element_type=F32)
    h = _layer_norm(alpha * h + ffn, ln_ref[2:3, :], ln_ref[3:4, :])
    ple = jax.nn.sigmoid(_dot(h, wpg_ref[...])) * _dot(p_ref[...], wpp_ref[...])
    o_ref[...] = _layer_norm(alpha * h + ple, ln_ref[4:5, :], ln_ref[5:6, :])


def _post(att, rw, x2d, p2d, wts, alpha):
    m = x2d.shape[0]
    tm = min(512, m)
    assert m % tm == 0
    row = lambda w: pl.BlockSpec((tm, w), lambda i: (i, 0))
    return pl.pallas_call(
        functools.partial(_post_kernel, alpha=alpha),
        out_shape=jax.ShapeDtypeStruct((m, D_MODEL), F32),
        grid=(m // tm,),
        in_specs=[row(C_HEADS), row(C_HEADS), row(D_MODEL), row(PLE_DIM),
                  _const_spec((2 * C_HEADS, D_MODEL)),
                  _const_spec((D_MODEL, D_FF)), _const_spec((D_MODEL, D_FF)),
                  _const_spec((D_FF, D_MODEL)),
                  _const_spec((D_MODEL, D_MODEL)), _const_spec((PLE_DIM, D_MODEL)),
                  _const_spec((8, D_MODEL))],
        out_specs=row(D_MODEL),
        compiler_params=_params("arbitrary"),
        name="post",
    )(att, rw, x2d, p2d, wts["w_out"], wts["w_gate"], wts["w_up"], wts["w_down"],
      wts["w_ple_gate"], wts["w_ple_proj"], wts["ln"])


def _attn_prompt_kernel(sl_ref, q_ref, k_ref, v_ref, o_ref, m_scr, l_scr, *, seq):
    pair = pl.program_id(1)
    slopes = (sl_ref[2 * pair], sl_ref[2 * pair + 1])
    lane = lax.broadcasted_iota(jnp.int32, (1, PAIR), 1)
    head0 = lane < HEAD_DIM
    qi0 = lax.broadcasted_iota(jnp.int32, (Q_BLOCK, 1), 0)
    ki0 = lax.broadcasted_iota(jnp.int32, (1, 2 * Q_BLOCK), 1)

    m_scr[...] = jnp.full(m_scr.shape, NEG_BIG, F32)
    l_scr[...] = jnp.zeros(l_scr.shape, F32)
    o_ref[...] = jnp.zeros(o_ref.shape, F32)

    for window, dil in DILATIONS:
        span = window // dil
        nb = seq // dil // Q_BLOCK
        assert nb >= 2 and nb * dil * Q_BLOCK == seq

        def unit(u, carry, dil=dil, span=span):
            n = u >> (dil.bit_length() - 1)
            r = u & (dil - 1)
            nprev = jnp.maximum(n - 1, 0)
            qstart = n * (Q_BLOCK * dil) + r
            kstart = nprev * (Q_BLOCK * dil) + r
            if dil == 1:
                qsl = pl.ds(qstart, Q_BLOCK)
                ksl = pl.ds(kstart, 2 * Q_BLOCK)
            else:
                qsl = pl.ds(qstart, Q_BLOCK, stride=dil)
                ksl = pl.ds(kstart, 2 * Q_BLOCK, stride=dil)
            q = q_ref[qsl, :]
            k = k_ref[ksl, :].astype(BF)
            v = v_ref[ksl, :].astype(BF)
            dist = (qi0 + n * Q_BLOCK) - (ki0 + nprev * Q_BLOCK)
            valid = (dist >= 0) & (dist <= span)
            distf = (dist * dil).astype(F32)
            m_old = m_scr[qsl, :]
            l_old = l_scr[qsl, :]
            a_old = o_ref[qsl, :]
            per_head = []
            for e in range(2):
                qe = jnp.where(head0 if e == 0 else ~head0, q, 0.0)
                s = _dot_nt(qe, k) * (HEAD_DIM ** -0.5) - slopes[e] * distf
                s = jnp.where(valid, s, NEG_BIG)
                mo = m_old[:, e * HEAD_DIM:e * HEAD_DIM + 1]
                mn = jnp.maximum(mo, jnp.max(s, axis=1, keepdims=True))
                p = jnp.exp(s - mn)
                per_head.append((mn, jnp.exp(mo - mn), jnp.sum(p, axis=1, keepdims=True),
                                 jnp.dot(p.astype(BF), v, preferred_element_type=F32)))
            (mn0, al0, ls0, pv0), (mn1, al1, ls1, pv1) = per_head
            al = jnp.where(head0, al0, al1)
            m_scr[qsl, :] = jnp.where(head0, mn0, mn1)
            l_scr[qsl, :] = al * l_old + jnp.where(head0, ls0, ls1)
            o_ref[qsl, :] = al * a_old + jnp.where(head0, pv0, pv1)
            return carry

        lax.fori_loop(0, nb * dil, unit, 0)

    o_ref[...] = o_ref[...] / l_scr[...]


def _attn_prompt(z3, slopes):
    b, seq, _ = z3.shape
    col = lambda off: pl.BlockSpec((None, seq, PAIR), lambda i, p: (i, 0, off + p))
    return pl.pallas_call(
        functools.partial(_attn_prompt_kernel, seq=seq),
        out_shape=jax.ShapeDtypeStruct((b, seq, C_HEADS), F32),
        grid=(b, N_PAIRS),
        in_specs=[pl.BlockSpec(memory_space=pltpu.SMEM),
                  col(0), col(N_PAIRS), col(2 * N_PAIRS)],
        out_specs=pl.BlockSpec((None, seq, PAIR), lambda i, p: (i, 0, p)),
        scratch_shapes=[pltpu.VMEM((seq, PAIR), F32), pltpu.VMEM((seq, PAIR), F32)],
        compiler_params=_params("arbitrary", "arbitrary"),
        name="attn_prompt",
    )(slopes, z3, z3, z3)


def _attn_sample_kernel(sl_ref, q_ref, kn_ref, vn_ref, ck_ref, cv_ref, o_ref, ko_ref, vo_ref,
                        *, s_new, cache_len):
    keep = cache_len - s_new
    ko_ref[0:keep, :] = ck_ref[s_new:cache_len, :]
    ko_ref[keep:cache_len, :] = kn_ref[...]
    vo_ref[0:keep, :] = cv_ref[s_new:cache_len, :]
    vo_ref[keep:cache_len, :] = vn_ref[...]

    rows = s_new * N_HEADS
    ri = lax.broadcasted_iota(jnp.int32, (rows, 1), 0)
    lane = lax.broadcasted_iota(jnp.int32, (1, C_HEADS), 1)
    own = (ri & (N_HEADS - 1)) == (lane >> 6)
    q = q_ref[...]
    qe = jnp.broadcast_to(q[:, None, :], (s_new, N_HEADS, C_HEADS)).reshape(rows, C_HEADS)
    qe = jnp.where(own, qe, 0.0)
    slope = sl_ref[:, 0:1]
    spos = ri >> 3

    def weights(dist):
        mult = jnp.zeros(dist.shape, F32)
        for window, dil in DILATIONS:
            hit = (dist >= 0) & (dist <= window) & ((dist & (dil - 1)) == 0)
            mult = mult + hit.astype(F32)
        return mult

    dist_c = cache_len + spos - lax.broadcasted_iota(jnp.int32, (1, cache_len), 1)
    dist_n = spos - lax.broadcasted_iota(jnp.int32, (1, s_new), 1)
    mult_c = weights(dist_c)
    mult_n = weights(dist_n)
    sc = _dot_nt(qe, ck_ref[...]) * (HEAD_DIM ** -0.5) - slope * dist_c.astype(F32)
    sn = _dot_nt(qe, kn_ref[...]) * (HEAD_DIM ** -0.5) - slope * dist_n.astype(F32)
    sc = jnp.where(mult_c > 0, sc, NEG_BIG)
    sn = jnp.where(mult_n > 0, sn, NEG_BIG)
    mx = jnp.maximum(jnp.max(sc, axis=1, keepdims=True), jnp.max(sn, axis=1, keepdims=True))
    pc = mult_c * jnp.exp(sc - mx)
    pn = mult_n * jnp.exp(sn - mx)
    den = jnp.sum(pc, axis=1, keepdims=True) + jnp.sum(pn, axis=1, keepdims=True)
    num = _dot(pc, cv_ref[...]) + _dot(pn, vn_ref[...])
    num = jnp.where(own, num, 0.0).reshape(s_new, N_HEADS, C_HEADS).sum(axis=1)
    den = jnp.where(own, den, 0.0).reshape(s_new, N_HEADS, C_HEADS).sum(axis=1)
    o_ref[...] = num / den


def _attn_sample(z3, cache_k, cache_v, slopes):
    b, s_new, _ = z3.shape
    cache_len = cache_k.shape[1]
    new = lambda c: pl.BlockSpec((None, s_new, C_HEADS), lambda i: (i, 0, c))
    cache = pl.BlockSpec((None, cache_len, C_HEADS), lambda i: (i, 0, 0))
    slope_rows = jnp.broadcast_to(jnp.tile(slopes, s_new)[:, None], (s_new * N_HEADS, PAIR))
    return pl.pallas_call(
        functools.partial(_attn_sample_kernel, s_new=s_new, cache_len=cache_len),
        out_shape=(jax.ShapeDtypeStruct((b, s_new, C_HEADS), F32),
                   jax.ShapeDtypeStruct(cache_k.shape, F32),
                   jax.ShapeDtypeStruct(cache_v.shape, F32)),
        grid=(b,),
        in_specs=[_const_spec((s_new * N_HEADS, PAIR)), new(0), new(1), new(2), cache, cache],
        out_specs=(new(0), cache, cache),
        compiler_params=_params("arbitrary"),
        name="attn_sample",
    )(slope_rows, z3, z3, z3, cache_k, cache_v)


def _rwkv_kernel(r_ref, k_ref, v_ref, wag_ref, pr_ref, pk_ref, pv_ref, pwag_ref,
                 sr_ref, sk_ref, sv_ref, swag_ref, m0_ref, vec_ref, muwag_ref, lora_ref, seg_ref,
                 o_ref, mout_ref, m_scr, y_scr, *, t_valid):
    tb = pl.program_id(1)
    first = tb == 0

    @pl.when(first)
    def _():
        m_scr[...] = m0_ref[...]

    rows = lax.broadcasted_iota(jnp.int32, (SUPER, 1), 0)

    def token_shift(cur_ref, prev_ref, carry_ref, mu):
        cur = cur_ref[...]
        last = jnp.where(first, carry_ref[...], prev_ref[7:8, :])
        prev = jnp.where(rows == 0, last, pltpu.roll(cur, 1, 0))
        return cur + (prev - cur) * mu

    vec = vec_ref[...]
    row = lambda i: vec[i:i + 1, :]
    zr = token_shift(r_ref, pr_ref, sr_ref, row(_V_MU_R))
    zk = token_shift(k_ref, pk_ref, sk_ref, row(_V_MU_K))
    zv = token_shift(v_ref, pv_ref, sv_ref, row(_V_MU_V))
    zwag = token_shift(wag_ref, pwag_ref, swag_ref, muwag_ref[...])
    wa = zwag[:, 0:D_LORA_WA]
    gi = zwag[:, D_LORA_WA:D_LORA_WA + D_G_LORA]

    wlin = row(_V_W0) + _dot(jnp.tanh(wa), lora_ref[0])
    softplus = jnp.maximum(-wlin, 0.0) + jnp.log(1.0 + jnp.exp(-jnp.abs(wlin)))
    w_log = -softplus - 0.5
    ld = -jnp.exp(w_log)
    lr = jax.nn.sigmoid(row(_V_A0) + _dot(wa, lora_ref[1]))
    gate = _dot(jax.nn.sigmoid(gi), lora_ref[2])
    seg = seg_ref[...]
    kk = zk * row(_V_KK)
    kk = kk / jnp.maximum(jnp.sqrt(_dot_exact_rhs(kk * kk, seg)), 1e-12)
    kmod = zk * (1.0 + (lr - 1.0) * row(_V_KA))
    vv = zv
    if t_valid is not None:
        live = (rows + tb * SUPER) < t_valid
        ld = jnp.where(live, ld, 0.0)
        kk = jnp.where(live, kk, 0.0)
        kmod = jnp.where(live, kmod, 0.0)
        vv = jnp.where(live, vv, 0.0)

    ti = lax.broadcasted_iota(jnp.int32, (SUPER, SUPER), 0)
    tj = lax.broadcasted_iota(jnp.int32, (SUPER, SUPER), 1)
    same_chunk = (ti >> 4) == (tj >> 4)
    tril = (same_chunk & (ti >= tj)).astype(BF)
    ones = same_chunk.astype(BF)
    cum = _dot_exact_lhs(tril, ld)
    tot = _dot_exact_lhs(ones, ld)
    dec_in = jnp.exp(cum)
    dec_ex = jnp.exp(cum - ld)
    dec_inv = jnp.exp(-cum)
    dec_end = jnp.exp(tot - cum)
    dec_tot = jnp.exp(tot)
    beta = kk * lr
    abar = -(kk * dec_ex)
    rbar = zr * dec_in
    bt = beta * dec_inv
    kt = kmod * dec_inv
    bh = beta * dec_end
    kh = kmod * dec_end

    ri = lax.broadcasted_iota(jnp.int32, (PAIR, PAIR), 0)
    ci = lax.broadcasted_iota(jnp.int32, (PAIR, PAIR), 1)
    same16 = (ri >> 4) == (ci >> 4)
    strict = same16 & (ri > ci)
    incl = same16 & (ri >= ci)
    same_head = (ri >> 6) == (ci >> 6)
    eye = ri == ci
    lane = lax.broadcasted_iota(jnp.int32, (1, PAIR), 1)
    head0 = lane < HEAD_DIM
    zeros_sp = jnp.zeros((SUPER, PAIR), F32)
    zeros_pp = jnp.zeros((PAIR, PAIR), F32)

    def stack(x):
        return jnp.concatenate([jnp.where(head0, x, 0.0), jnp.where(head0, 0.0, x)], axis=0)

    def unstack(x):
        return x[0:SUPER, :] + x[SUPER:PAIR, :]

    for p in range(N_PAIRS):
        sl = slice(p * PAIR, (p + 1) * PAIR)
        ab, rb, v_p = abar[:, sl], rbar[:, sl], vv[:, sl]
        lhs = jnp.concatenate([ab, ab, rb, rb], axis=0)
        rhs = jnp.concatenate([stack(bt[:, sl]), stack(kt[:, sl])], axis=0)
        a_all = _dot_nt(lhs, rhs)
        n_ab = jnp.where(strict, a_all[0:PAIR, 0:PAIR], 0.0)
        a_ak = jnp.where(strict, a_all[0:PAIR, PAIR:2 * PAIR], 0.0)
        a_rb = jnp.where(incl, a_all[PAIR:2 * PAIR, 0:PAIR], 0.0)
        a_rk = jnp.where(incl, a_all[PAIR:2 * PAIR, PAIR:2 * PAIR], 0.0)
        tinv = jnp.where(eye, 1.0, 0.0) + n_ab
        power = n_ab
        for _ in range(3):
            power = _dot(power, power)
            tinv = tinv + _dot(power, tinv)
        v_s = stack(v_p)
        u_s = _dot(a_ak, v_s)
        ta = _dot(tinv, jnp.concatenate([stack(ab), u_s], axis=1))
        a1_s, u1_s = ta[:, 0:PAIR], ta[:, PAIR:2 * PAIR]
        ry = _dot(jnp.concatenate([a_rb, a_rk], axis=1),
                  jnp.concatenate([jnp.concatenate([a1_s, u1_s], axis=1),
                                   jnp.concatenate([zeros_pp, v_s], axis=1)], axis=0))
        r1 = unstack(stack(rb) + ry[:, 0:PAIR])
        y0 = unstack(ry[:, PAIR:2 * PAIR])
        bk_t = jnp.concatenate([bh[:, sl], kh[:, sl]], axis=0).T
        rhs3 = jnp.concatenate([jnp.concatenate([unstack(a1_s), unstack(u1_s)], axis=1),
                                jnp.concatenate([zeros_sp, v_p], axis=1)], axis=0)
        m = m_scr[p]
        for c in range(N_CHUNKS):
            in_chunk = ((lane & (SUPER - 1)) >> 4) == c
            gh = _dot(jnp.where(in_chunk, bk_t, 0.0), rhs3)
            g_c = (jnp.where(eye, dec_tot[c * CHUNK:c * CHUNK + 1, sl], 0.0)
                   + jnp.where(same_head, gh[:, 0:PAIR], 0.0))
            h_c = jnp.where(same_head, gh[:, PAIR:2 * PAIR], 0.0)
            tok = slice(c * CHUNK, (c + 1) * CHUNK)
            y_scr[tok, sl] = _dot(r1[tok, :], m) + y0[tok, :]
            m = _dot(g_c, m) + h_c
        m_scr[p] = m

    y = y_scr[...]
    mean = _dot_exact_rhs(y, seg) * (1.0 / HEAD_DIM)
    yc = y - mean
    var = _dot_exact_rhs(yc * yc, seg) * (1.0 / HEAD_DIM)
    yn = yc * lax.rsqrt(var + GN_EPS) * row(_V_LNW) + row(_V_LNB)
    bonus = _dot_exact_rhs(zr * kmod * row(_V_RK), seg) * zv
    o_ref[...] = (yn + bonus) * gate

    @pl.when(tb == pl.num_programs(1) - 1)
    def _():
        mout_ref[...] = m_scr[...]


def _rwkv(z3, shift_prev, m0, wts, t_valid):
    b, seq, _ = z3.shape
    assert seq % SUPER == 0
    nt = seq // SUPER
    cur = lambda w, c: pl.BlockSpec((None, SUPER, w), lambda i, t: (i, t, c))
    prev = lambda w, c: pl.BlockSpec(
        (None, 8, w), lambda i, t: (i, jnp.maximum(t * (SUPER // 8) - 1, 0), c))
    carry = lambda w: pl.BlockSpec((None, 1, w), lambda i, t: (i, 0, 0))
    state = pl.BlockSpec((None, N_PAIRS, PAIR, PAIR), lambda i, t: (i, 0, 0, 0))
    wag_w = D_LORA_WA + D_G_LORA
    sp = shift_prev[:, None, :]
    return pl.pallas_call(
        functools.partial(_rwkv_kernel, t_valid=None if t_valid == seq else t_valid),
        out_shape=(jax.ShapeDtypeStruct((b, seq, C_HEADS), F32),
                   jax.ShapeDtypeStruct((b, N_PAIRS, PAIR, PAIR), F32)),
        grid=(b, nt),
        in_specs=[cur(C_HEADS, 3), cur(C_HEADS, 4), cur(C_HEADS, 5), cur(wag_w, 12),
                  prev(C_HEADS, 3), prev(C_HEADS, 4), prev(C_HEADS, 5), prev(wag_w, 12),
                  carry(C_HEADS), carry(C_HEADS), carry(C_HEADS), carry(wag_w),
                  state,
                  _const_spec((_V_ROWS, C_HEADS)), _const_spec((1, wag_w)),
                  _const_spec((3, PAIR, C_HEADS)), _const_spec((C_HEADS, C_HEADS))],
        out_specs=(pl.BlockSpec((None, SUPER, C_HEADS), lambda i, t: (i, t, 0)), state),
        scratch_shapes=[pltpu.VMEM((N_PAIRS, PAIR, PAIR), F32), pltpu.VMEM((SUPER, C_HEADS), F32)],
        compiler_params=_params("arbitrary", "arbitrary"),
        name="rwkv",
    )(z3, z3, z3, z3, z3, z3, z3, z3,
      sp[:, :, 0:C_HEADS], sp[:, :, C_HEADS:2 * C_HEADS], sp[:, :, 2 * C_HEADS:3 * C_HEADS],
      sp[:, :, 3 * C_HEADS:], m0, wts["vec"], wts["mu_wag"], wts["lora"], wts["seg"])


def _state_to_pairs(s):
    b = s.shape[0]
    st = jnp.swapaxes(s, -1, -2).reshape(b, N_PAIRS, 2, HEAD_DIM, HEAD_DIM)
    eye2 = jnp.eye(2, dtype=s.dtype)
    m = st[:, :, :, :, None, :] * eye2[None, None, :, None, :, None]
    return m.reshape(b, N_PAIRS, PAIR, PAIR)


def _pairs_to_state(m):
    b = m.shape[0]
    m6 = m.reshape(b, N_PAIRS, 2, HEAD_DIM, 2, HEAD_DIM)
    st = jnp.stack([m6[:, :, 0, :, 0, :], m6[:, :, 1, :, 1, :]], axis=2)
    return jnp.swapaxes(st.reshape(b, N_HEADS, HEAD_DIM, HEAD_DIM), -1, -2)


def _pack_layer(w_in, mu_shift, w0, w2, a0, a2, g2, k_k, k_a, r_k, lnx_w, lnx_b, w_out, ln1_w,
                ln1_b, w_gate, w_up, w_down, ln2_w, ln2_b, w_ple_gate, w_ple_proj, ln3_w, ln3_b):
    mu_r, mu_k, mu_v = (mu_shift[i * C_HEADS:(i + 1) * C_HEADS] for i in range(3))
    vec = jnp.stack([mu_r, mu_k, mu_v, w0, a0, k_k, k_a, r_k.reshape(-1), lnx_w, lnx_b])
    vec = jnp.concatenate([vec, jnp.zeros((_V_ROWS - vec.shape[0], C_HEADS), F32)], axis=0)
    half = D_LORA_WA // 2
    zeros = jnp.zeros((half, C_HEADS), F32)
    lora = jnp.stack([jnp.concatenate([w2, zeros], axis=0),
                      jnp.concatenate([zeros, a2], axis=0),
                      g2]).astype(BF)
    head_of = jnp.arange(C_HEADS) // HEAD_DIM
    seg = (head_of[:, None] == head_of[None, :]).astype(BF)
    ln = jnp.stack([ln1_w, ln1_b, ln2_w, ln2_b, ln3_w, ln3_b,
                    jnp.zeros_like(ln1_w), jnp.zeros_like(ln1_w)])
    return {
        "w_in": w_in.astype(BF), "vec": vec, "mu_wag": mu_shift[None, 3 * C_HEADS:],
        "lora": lora, "seg": seg, "w_out": w_out.astype(BF), "w_gate": w_gate.astype(BF),
        "w_up": w_up.astype(BF), "w_down": w_down.astype(BF),
        "w_ple_gate": w_ple_gate.astype(BF), "w_ple_proj": w_ple_proj.astype(BF), "ln": ln,
    }


def _alibi_slopes():
    h = jnp.arange(1, N_HEADS + 1, dtype=F32)
    return jnp.exp2(-8.0 * h / N_HEADS)


def _layer(x, p_l, wts, alpha, shift_prev, wkv_prev, cache_k=None, cache_v=None):
    b, seq, _ = x.shape
    x2d = x.reshape(b * seq, D_MODEL)
    z = _in_proj(x2d, wts["w_in"])
    z3 = z.reshape(b, seq, D_IN)
    slopes = _alibi_slopes()
    if cache_k is None:
        att = _attn_prompt(z3, slopes)
        keep = min(WINDOW_MAX, seq)
        k_win = z3[:, seq - keep:, C_HEADS:2 * C_HEADS].reshape(b, keep, N_HEADS, HEAD_DIM)
        v_win = z3[:, seq - keep:, 2 * C_HEADS:3 * C_HEADS].reshape(b, keep, N_HEADS, HEAD_DIM)
    else:
        cache_len = cache_k.shape[1]
        att, k_win, v_win = _attn_sample(z3, cache_k.reshape(b, cache_len, C_HEADS),
                                         cache_v.reshape(b, cache_len, C_HEADS), slopes)
        k_win = k_win.reshape(cache_k.shape)
        v_win = v_win.reshape(cache_v.shape)
    pad = (-seq) % SUPER
    z3p = jnp.pad(z3, ((0, 0), (0, pad), (0, 0))) if pad else z3
    rw, m_last = _rwkv(z3p, shift_prev, _state_to_pairs(wkv_prev), wts, seq)
    rw = rw[:, :seq]
    y = _post(att.reshape(b * seq, C_HEADS), rw.reshape(b * seq, C_HEADS), x2d,
              p_l.reshape(b * seq, PLE_DIM), wts, alpha)
    shift_new = z3[:, seq - 1, 3 * C_HEADS:]
    return y.reshape(b, seq, D_MODEL), k_win, v_win, shift_new, _pairs_to_state(m_last)


def kernel(x_prompt, x_sample, p_prompt, p_sample, cache_k_win, cache_v_win, state_wkv, state_shift, w_in, mu_shift, w0, w2, a0, a2, g2, k_k, k_a, r_k, lnx_w, lnx_b, w_out, ln1_w, ln1_b, w_gate, w_up, w_down, ln2_w, ln2_b, w_ple_gate, w_ple_proj, ln3_w, ln3_b):
    depth = w_in.shape[0]
    alpha = float((2 * depth) ** 0.25)
    xp, xs = x_prompt, x_sample
    bp = x_prompt.shape[0]
    shift0 = jnp.zeros((bp, D_B_IN), x_prompt.dtype)
    wkv0 = jnp.zeros((bp, N_HEADS, HEAD_DIM, HEAD_DIM), state_wkv.dtype)
    outs = [[] for _ in range(8)]
    for l in range(depth):
        wts = _pack_layer(w_in[l], mu_shift[l], w0[l], w2[l], a0[l], a2[l], g2[l], k_k[l], k_a[l],
                          r_k[l], lnx_w[l], lnx_b[l], w_out[l], ln1_w[l], ln1_b[l], w_gate[l],
                          w_up[l], w_down[l], ln2_w[l], ln2_b[l], w_ple_gate[l], w_ple_proj[l],
                          ln3_w[l], ln3_b[l])
        xp, kw, vw, sh, wk = _layer(xp, p_prompt[l], wts, alpha, shift0, wkv0)
        for lst, val in zip(outs[0:4], (kw, vw, wk, sh)):
            lst.append(val)
        xs, kw, vw, sh, wk = _layer(xs, p_sample[l], wts, alpha, state_shift[l], state_wkv[l],
                                    cache_k_win[l], cache_v_win[l])
        for lst, val in zip(outs[4:8], (kw, vw, wk, sh)):
            lst.append(val)
    return (xp, xs) + tuple(jnp.stack(o) for o in outs)
```

```python
import functools

import jax
import jax.numpy as jnp
from jax import lax
from jax.experimental import pallas as pl
from jax.experimental.pallas import tpu as pltpu

BF = jnp.bfloat16
F32 = jnp.float32

D_MODEL = 1024
HEAD_DIM = 64
N_HEADS = 8
C_HEADS = N_HEADS * HEAD_DIM
PAIR = 2 * HEAD_DIM
N_PAIRS = N_HEADS // 2
DILATIONS = ((128, 1), (512, 4), (2048, 16))
WINDOW_MAX = 2048
Q_BLOCK = 128
D_LORA_WA = 128
D_G_LORA = 128
D_B_IN = 3 * C_HEADS + D_LORA_WA + D_G_LORA
D_IN = 3 * C_HEADS + D_B_IN
D_FF = 2816
PLE_DIM = 256
LN_EPS = 1e-5
GN_EPS = 64e-5
NEG_BIG = -1e30

CHUNK = 16
SUPER = 64
N_CHUNKS = SUPER // CHUNK
RWKV_BLOCKS = 4
FF_CHUNK = 256
ATTN_UNROLL = 4
IN_CHUNK = 256
VMEM_LIMIT_BYTES = 56 * 1024 * 1024

(_V_MU_R, _V_MU_K, _V_MU_V, _V_W0, _V_A0, _V_KK, _V_KA, _V_RK, _V_LNW, _V_LNB) = range(10)
_V_ROWS = 16


def _params(*sem):
    return pltpu.CompilerParams(dimension_semantics=sem, vmem_limit_bytes=VMEM_LIMIT_BYTES)


def _const_spec(shape):
    nd = len(shape)
    return pl.BlockSpec(shape, lambda *_: (0,) * nd, pipeline_mode=pl.Buffered(1))


def _dot(a, b):
    return jnp.dot(a.astype(BF), b.astype(BF), preferred_element_type=F32)


def _dot_nt(a, b):
    return lax.dot_general(a.astype(BF), b.astype(BF), (((1,), (1,)), ((), ())),
                           preferred_element_type=F32)


def _split(x):
    hi = x.astype(BF)
    lo = (x - hi.astype(F32)).astype(BF)
    return hi, lo


def _dot_exact_lhs(mask_bf, x):
    hi, lo = _split(x)
    return (jnp.dot(mask_bf, hi, preferred_element_type=F32)
            + jnp.dot(mask_bf, lo, preferred_element_type=F32))


def _dot_exact_rhs(x, mask_bf):
    hi, lo = _split(x)
    return (jnp.dot(hi, mask_bf, preferred_element_type=F32)
            + jnp.dot(lo, mask_bf, preferred_element_type=F32))


def _layer_norm(x, w, b):
    mu = jnp.mean(x, axis=-1, keepdims=True)
    xc = x - mu
    var = jnp.mean(xc * xc, axis=-1, keepdims=True)
    return xc * lax.rsqrt(var + LN_EPS) * w + b


def _in_proj_kernel(x_ref, w_ref, o_ref):
    xb = x_ref[...].astype(BF)
    for j in range(0, D_IN, IN_CHUNK):
        o_ref[:, j:j + IN_CHUNK] = jnp.dot(xb, w_ref[:, j:j + IN_CHUNK],
                                           preferred_element_type=F32)


def _in_proj(x2d, w_bf):
    m = x2d.shape[0]
    tm = min(512, m)
    assert m % tm == 0
    return pl.pallas_call(
        _in_proj_kernel,
        out_shape=jax.ShapeDtypeStruct((m, D_IN), F32),
        grid=(m // tm,),
        in_specs=[pl.BlockSpec((tm, D_MODEL), lambda i: (i, 0)),
                  _const_spec((D_MODEL, D_IN))],
        out_specs=pl.BlockSpec((tm, D_IN), lambda i: (i, 0)),
        compiler_params=_params("arbitrary"),
        name="in_proj",
    )(x2d, w_bf)


def _post_kernel(att_ref, rw_ref, x_ref, p_ref, wo_ref, wg_ref, wu_ref, wd_ref, wpg_ref,
                 wpp_ref, ln_ref, o_ref, *, alpha):
    mix = (_dot(att_ref[...], wo_ref[0:C_HEADS, :])
           + _dot(rw_ref[...], wo_ref[C_HEADS:2 * C_HEADS, :]))
    h = _layer_norm(alpha * x_ref[...] + mix, ln_ref[0:1, :], ln_ref[1:2, :])
    hb = h.astype(BF)
    ffn = jnp.zeros_like(h)
    for j in range(0, D_FF, FF_CHUNK):
        g = jnp.dot(hb, wg_ref[:, j:j + FF_CHUNK], preferred_element_type=F32)
        u = jnp.dot(hb, wu_ref[:, j:j + FF_CHUNK], preferred_element_type=F32)
        act = g * jax.nn.sigmoid(g) * u
        ffn = ffn + jnp.dot(act.astype(BF), wd_ref[j:j + FF_CHUNK, :],
                            preferred_element_type=F32)
    h = _layer_norm(alpha * h + ffn, ln_ref[2:3, :], ln_ref[3:4, :])
    ple = jax.nn.sigmoid(_dot(h, wpg_ref[...])) * _dot(p_ref[...], wpp_ref[...])
    o_ref[...] = _layer_norm(alpha * h + ple, ln_ref[4:5, :], ln_ref[5:6, :])


def _post(att, rw, x2d, p2d, wts, alpha):
    m = x2d.shape[0]
    tm = min(512, m)
    assert m % tm == 0
    row = lambda w: pl.BlockSpec((tm, w), lambda i: (i, 0))
    return pl.pallas_call(
        functools.partial(_post_kernel, alpha=alpha),
        out_shape=jax.ShapeDtypeStruct((m, D_MODEL), F32),
        grid=(m // tm,),
        in_specs=[row(C_HEADS), row(C_HEADS), row(D_MODEL), row(PLE_DIM),
                  _const_spec((2 * C_HEADS, D_MODEL)),
                  _const_spec((D_MODEL, D_FF)), _const_spec((D_MODEL, D_FF)),
                  _const_spec((D_FF, D_MODEL)),
                  _const_spec((D_MODEL, D_MODEL)), _const_spec((PLE_DIM, D_MODEL)),
                  _const_spec((8, D_MODEL))],
        out_specs=row(D_MODEL),
        compiler_params=_params("arbitrary"),
        name="post",
    )(att, rw, x2d, p2d, wts["w_out"], wts["w_gate"], wts["w_up"], wts["w_down"],
      wts["w_ple_gate"], wts["w_ple_proj"], wts["ln"])


def _attn_prompt_kernel(sl_ref, q_ref, k_ref, v_ref, o_ref, m_scr, l_scr, mb_scr, *, seq):
    pair = pl.program_id(1)
    slopes = (sl_ref[2 * pair], sl_ref[2 * pair + 1])
    lane = lax.broadcasted_iota(jnp.int32, (1, PAIR), 1)
    head0 = lane < HEAD_DIM
    qi0 = lax.broadcasted_iota(jnp.int32, (Q_BLOCK, 1), 0)
    ki0 = lax.broadcasted_iota(jnp.int32, (1, 2 * Q_BLOCK), 1)

    m_scr[...] = jnp.full(m_scr.shape, NEG_BIG, F32)
    l_scr[...] = jnp.zeros(l_scr.shape, F32)
    o_ref[...] = jnp.zeros(o_ref.shape, F32)

    for window, dil in DILATIONS:
        span = window // dil
        nb = seq // dil // Q_BLOCK
        n_units = nb * dil
        assert nb >= 2 and n_units * Q_BLOCK == seq and n_units % ATTN_UNROLL == 0
        for case in range(2):
            dist = (qi0 + case * Q_BLOCK) - ki0
            valid = (dist >= 0) & (dist <= span)
            distf = (dist * dil).astype(F32)
            for e in range(2):
                mb_scr[2 * case + e] = jnp.where(valid, -slopes[e] * distf, NEG_BIG)

        def group(g, carry, dil=dil):
            loaded = []
            for j in range(ATTN_UNROLL):
                u = g * ATTN_UNROLL + j
                n = u >> (dil.bit_length() - 1)
                r = u & (dil - 1)
                nprev = jnp.maximum(n - 1, 0)
                qstart = n * (Q_BLOCK * dil) + r
                kstart = nprev * (Q_BLOCK * dil) + r
                if dil == 1:
                    qsl = pl.ds(qstart, Q_BLOCK)
                    ksl = pl.ds(kstart, 2 * Q_BLOCK)
                else:
                    qsl = pl.ds(qstart, Q_BLOCK, stride=dil)
                    ksl = pl.ds(kstart, 2 * Q_BLOCK, stride=dil)
                q = q_ref[qsl, :] * (HEAD_DIM ** -0.5)
                loaded.append((qsl, jnp.minimum(n, 1), q, k_ref[ksl, :].astype(BF),
                               v_ref[ksl, :].astype(BF), m_scr[qsl, :], l_scr[qsl, :],
                               o_ref[qsl, :]))
            scores = [[_dot_nt(jnp.where(head0 if e == 0 else ~head0, q, 0.0), k)
                       + mb_scr[2 * case + e] for e in range(2)]
                      for (_, case, q, k, _, _, _, _) in loaded]
            stats = []
            for (_, _, _, _, _, m_old, _, _), s2 in zip(loaded, scores):
                per_head = []
                for e in range(2):
                    mo = m_old[:, e * HEAD_DIM:e * HEAD_DIM + 1]
                    mn = jnp.maximum(mo, jnp.max(s2[e], axis=1, keepdims=True))
                    p = jnp.exp(s2[e] - mn)
                    per_head.append((mn, jnp.exp(mo - mn), jnp.sum(p, axis=1, keepdims=True),
                                     p.astype(BF)))
                stats.append(per_head)
            pvs = [[jnp.dot(st[e][3], v, preferred_element_type=F32) for e in range(2)]
                   for (_, _, _, _, v, _, _, _), st in zip(loaded, stats)]
            for (qsl, _, _, _, _, _, l_old, a_old), st, pv in zip(loaded, stats, pvs):
                al = jnp.where(head0, st[0][1], st[1][1])
                m_scr[qsl, :] = jnp.where(head0, st[0][0], st[1][0])
                l_scr[qsl, :] = al * l_old + jnp.where(head0, st[0][2], st[1][2])
                o_ref[qsl, :] = al * a_old + jnp.where(head0, pv[0], pv[1])
            return carry

        lax.fori_loop(0, n_units // ATTN_UNROLL, group, 0)

    o_ref[...] = o_ref[...] / l_scr[...]


def _attn_prompt(z3, slopes):
    b, seq, _ = z3.shape
    col = lambda off: pl.BlockSpec((None, seq, PAIR), lambda i, p: (i, 0, off + p))
    return pl.pallas_call(
        functools.partial(_attn_prompt_kernel, seq=seq),
        out_shape=jax.ShapeDtypeStruct((b, seq, C_HEADS), F32),
        grid=(b, N_PAIRS),
        in_specs=[pl.BlockSpec(memory_space=pltpu.SMEM),
                  col(0), col(N_PAIRS), col(2 * N_PAIRS)],
        out_specs=pl.BlockSpec((None, seq, PAIR), lambda i, p: (i, 0, p)),
        scratch_shapes=[pltpu.VMEM((seq, PAIR), F32), pltpu.VMEM((seq, PAIR), F32),
                        pltpu.VMEM((4, Q_BLOCK, 2 * Q_BLOCK), F32)],
        compiler_params=_params("arbitrary", "arbitrary"),
        name="attn_prompt",
    )(slopes, z3, z3, z3)


def _attn_sample_kernel(sl_ref, q_ref, kn_ref, vn_ref, ck_ref, cv_ref, o_ref, ko_ref, vo_ref,
                        *, s_new, cache_len):
    keep = cache_len - s_new
    ko_ref[0:keep, :] = ck_ref[s_new:cache_len, :]
    ko_ref[keep:cache_len, :] = kn_ref[...]
    vo_ref[0:keep, :] = cv_ref[s_new:cache_len, :]
    vo_ref[keep:cache_len, :] = vn_ref[...]

    rows = s_new * N_HEADS
    ri = lax.broadcasted_iota(jnp.int32, (rows, 1), 0)
    lane = lax.broadcasted_iota(jnp.int32, (1, C_HEADS), 1)
    own = (ri & (N_HEADS - 1)) == (lane >> 6)
    q = q_ref[...]
    qe = jnp.broadcast_to(q[:, None, :], (s_new, N_HEADS, C_HEADS)).reshape(rows, C_HEADS)
    qe = jnp.where(own, qe, 0.0)
    slope = sl_ref[:, 0:1]
    spos = ri >> 3

    def weights(dist):
        mult = jnp.zeros(dist.shape, F32)
        for window, dil in DILATIONS:
            hit = (dist >= 0) & (dist <= window) & ((dist & (dil - 1)) == 0)
            mult = mult + hit.astype(F32)
        return mult

    dist_c = cache_len + spos - lax.broadcasted_iota(jnp.int32, (1, cache_len), 1)
    dist_n = spos - lax.broadcasted_iota(jnp.int32, (1, s_new), 1)
    mult_c = weights(dist_c)
    mult_n = weights(dist_n)
    sc = _dot_nt(qe, ck_ref[...]) * (HEAD_DIM ** -0.5) - slope * dist_c.astype(F32)
    sn = _dot_nt(qe, kn_ref[...]) * (HEAD_DIM ** -0.5) - slope * dist_n.astype(F32)
    sc = jnp.where(mult_c > 0, sc, NEG_BIG)
    sn = jnp.where(mult_n > 0, sn, NEG_BIG)
    mx = jnp.maximum(jnp.max(sc, axis=1, keepdims=True), jnp.max(sn, axis=1, keepdims=True))
    pc = mult_c * jnp.exp(sc - mx)
    pn = mult_n * jnp.exp(sn - mx)
    den = jnp.sum(pc, axis=1, keepdims=True) + jnp.sum(pn, axis=1, keepdims=True)
    num = _dot(pc, cv_ref[...]) + _dot(pn, vn_ref[...])
    num = jnp.where(own, num, 0.0).reshape(s_new, N_HEADS, C_HEADS).sum(axis=1)
    den = jnp.where(own, den, 0.0).reshape(s_new, N_HEADS, C_HEADS).sum(axis=1)
    o_ref[...] = num / den


def _attn_sample(z3, cache_k, cache_v, slopes):
    b, s_new, _ = z3.shape
    cache_len = cache_k.shape[1]
    new = lambda c: pl.BlockSpec((None, s_new, C_HEADS), lambda i: (i, 0, c))
    cache = pl.BlockSpec((None, cache_len, C_HEADS), lambda i: (i, 0, 0))
    slope_rows = jnp.broadcast_to(jnp.tile(slopes, s_new)[:, None], (s_new * N_HEADS, PAIR))
    return pl.pallas_call(
        functools.partial(_attn_sample_kernel, s_new=s_new, cache_len=cache_len),
        out_shape=(jax.ShapeDtypeStruct((b, s_new, C_HEADS), F32),
                   jax.ShapeDtypeStruct(cache_k.shape, F32),
                   jax.ShapeDtypeStruct(cache_v.shape, F32)),
        grid=(b,),
        in_specs=[_const_spec((s_new * N_HEADS, PAIR)), new(0), new(1), new(2), cache, cache],
        out_specs=(new(0), cache, cache),
        compiler_params=_params("arbitrary"),
        name="attn_sample",
    )(slope_rows, z3, z3, z3, cache_k, cache_v)


def _rwkv_kernel(r_ref, k_ref, v_ref, wag_ref, pr_ref, pk_ref, pv_ref, pwag_ref,
                 sr_ref, sk_ref, sv_ref, swag_ref, m0_ref, vec_ref, muwag_ref, lora_ref, seg_ref,
                 o_ref, mout_ref, m_scr, y_scr, *, t_valid, nsb):
    tb = pl.program_id(1)
    first = tb == 0
    n_rows = nsb * SUPER

    @pl.when(first)
    def _():
        m_scr[...] = m0_ref[...]

    rows = lax.broadcasted_iota(jnp.int32, (n_rows, 1), 0)

    def token_shift(cur_ref, prev_ref, carry_ref, mu):
        cur = cur_ref[...]
        last = jnp.where(first, carry_ref[...], prev_ref[7:8, :])
        prev = jnp.where(rows == 0, last, pltpu.roll(cur, 1, 0))
        return cur + (prev - cur) * mu

    vec = vec_ref[...]
    row = lambda i: vec[i:i + 1, :]
    zr = token_shift(r_ref, pr_ref, sr_ref, row(_V_MU_R))
    zk = token_shift(k_ref, pk_ref, sk_ref, row(_V_MU_K))
    zv = token_shift(v_ref, pv_ref, sv_ref, row(_V_MU_V))
    zwag = token_shift(wag_ref, pwag_ref, swag_ref, muwag_ref[...])
    wa = zwag[:, 0:D_LORA_WA]
    gi = zwag[:, D_LORA_WA:D_LORA_WA + D_G_LORA]

    wlin = row(_V_W0) + _dot(jnp.tanh(wa), lora_ref[0])
    softplus = jnp.maximum(-wlin, 0.0) + jnp.log(1.0 + jnp.exp(-jnp.abs(wlin)))
    w_log = -softplus - 0.5
    ld = -jnp.exp(w_log)
    lr = jax.nn.sigmoid(row(_V_A0) + _dot(wa, lora_ref[1]))
    gate = _dot(jax.nn.sigmoid(gi), lora_ref[2])
    seg = seg_ref[...]
    kk = zk * row(_V_KK)
    kk = kk / jnp.maximum(jnp.sqrt(_dot_exact_rhs(kk * kk, seg)), 1e-12)
    kmod = zk * (1.0 + (lr - 1.0) * row(_V_KA))
    vv = zv
    if t_valid is not None:
        live = (rows + tb * n_rows) < t_valid
        ld = jnp.where(live, ld, 0.0)
        kk = jnp.where(live, kk, 0.0)
        kmod = jnp.where(live, kmod, 0.0)
        vv = jnp.where(live, vv, 0.0)

    ti = lax.broadcasted_iota(jnp.int32, (n_rows, n_rows), 0)
    tj = lax.broadcasted_iota(jnp.int32, (n_rows, n_rows), 1)
    same_chunk = (ti >> 4) == (tj >> 4)
    tril = (same_chunk & (ti >= tj)).astype(BF)
    ones = same_chunk.astype(BF)
    cum = _dot_exact_lhs(tril, ld)
    tot = _dot_exact_lhs(ones, ld)
    dec_in = jnp.exp(cum)
    dec_ex = jnp.exp(cum - ld)
    dec_inv = jnp.exp(-cum)
    dec_end = jnp.exp(tot - cum)
    dec_tot = jnp.exp(tot)
    beta = kk * lr
    abar = -(kk * dec_ex)
    rbar = zr * dec_in
    bt = beta * dec_inv
    kt = kmod * dec_inv
    bh = beta * dec_end
    kh = kmod * dec_end

    ri = lax.broadcasted_iota(jnp.int32, (PAIR, PAIR), 0)
    ci = lax.broadcasted_iota(jnp.int32, (PAIR, PAIR), 1)
    same16 = (ri >> 4) == (ci >> 4)
    strict = same16 & (ri > ci)
    incl = same16 & (ri >= ci)
    same_head = (ri >> 6) == (ci >> 6)
    eye = ri == ci
    lane = lax.broadcasted_iota(jnp.int32, (1, PAIR), 1)
    head0 = lane < HEAD_DIM
    zeros_sp = jnp.zeros((SUPER, PAIR), F32)
    zeros_pp = jnp.zeros((PAIR, PAIR), F32)

    def stack(x):
        return jnp.concatenate([jnp.where(head0, x, 0.0), jnp.where(head0, 0.0, x)], axis=0)

    def unstack(x):
        return x[0:SUPER, :] + x[SUPER:PAIR, :]

    units = [(s, p) for s in range(nsb) for p in range(N_PAIRS)]
    tile = lambda x, u: x[u[0] * SUPER:(u[0] + 1) * SUPER, u[1] * PAIR:(u[1] + 1) * PAIR]
    ab = [tile(abar, u) for u in units]
    rb = [tile(rbar, u) for u in units]
    v_p = [tile(vv, u) for u in units]
    v_s = [stack(x) for x in v_p]
    a_all = [_dot_nt(jnp.concatenate([a, a, r, r], axis=0),
                     jnp.concatenate([stack(tile(bt, u)), stack(tile(kt, u))], axis=0))
             for a, r, u in zip(ab, rb, units)]
    n_ab = [jnp.where(strict, a[0:PAIR, 0:PAIR], 0.0) for a in a_all]
    a_ak = [jnp.where(strict, a[0:PAIR, PAIR:2 * PAIR], 0.0) for a in a_all]
    a_rbk = [jnp.concatenate([jnp.where(incl, a[PAIR:2 * PAIR, 0:PAIR], 0.0),
                              jnp.where(incl, a[PAIR:2 * PAIR, PAIR:2 * PAIR], 0.0)], axis=1)
             for a in a_all]
    eye_f = jnp.where(eye, 1.0, 0.0)
    tinv = [eye_f + n for n in n_ab]
    power = n_ab
    for _ in range(3):
        power = [_dot(x, x) for x in power]
        tinv = [t + _dot(x, t) for x, t in zip(power, tinv)]
    u_s = [_dot(a, v) for a, v in zip(a_ak, v_s)]
    ta = [_dot(t, jnp.concatenate([stack(a), u], axis=1))
          for t, a, u in zip(tinv, ab, u_s)]
    ry = [_dot(a, jnp.concatenate([t, jnp.concatenate([zeros_pp, v], axis=1)], axis=0))
          for a, t, v in zip(a_rbk, ta, v_s)]
    r1 = [unstack(stack(r) + y[:, 0:PAIR]) for r, y in zip(rb, ry)]
    y0 = [unstack(y[:, PAIR:2 * PAIR]) for y in ry]
    bk_t = [jnp.concatenate([tile(bh, u), tile(kh, u)], axis=0).T
            for u in units]
    rhs3 = [jnp.concatenate([jnp.concatenate([unstack(t[:, 0:PAIR]), unstack(t[:, PAIR:2 * PAIR])],
                                             axis=1),
                             jnp.concatenate([zeros_sp, v], axis=1)], axis=0)
            for t, v in zip(ta, v_p)]
    in_chunk = [((lane & (SUPER - 1)) >> 4) == c for c in range(N_CHUNKS)]
    gh = [[_dot(jnp.where(in_chunk[c], b, 0.0), r) for b, r in zip(bk_t, rhs3)]
          for c in range(N_CHUNKS)]

    m = [m_scr[p] for p in range(N_PAIRS)]
    for s in range(nsb):
        for c in range(N_CHUNKS):
            tok = slice(c * CHUNK, (c + 1) * CHUNK)
            r0 = s * SUPER + c * CHUNK
            for p in range(N_PAIRS):
                u = s * N_PAIRS + p
                sl = slice(p * PAIR, (p + 1) * PAIR)
                g_c = (jnp.where(eye, dec_tot[r0:r0 + 1, sl], 0.0)
                       + jnp.where(same_head, gh[c][u][:, 0:PAIR], 0.0))
                h_c = jnp.where(same_head, gh[c][u][:, PAIR:2 * PAIR], 0.0)
                y_scr[r0:r0 + CHUNK, sl] = _dot(r1[u][tok, :], m[p]) + y0[u][tok, :]
                m[p] = _dot(g_c, m[p]) + h_c
    for p in range(N_PAIRS):
        m_scr[p] = m[p]

    y = y_scr[...]
    mean = _dot_exact_rhs(y, seg) * (1.0 / HEAD_DIM)
    yc = y - mean
    var = _dot_exact_rhs(yc * yc, seg) * (1.0 / HEAD_DIM)
    yn = yc * lax.rsqrt(var + GN_EPS) * row(_V_LNW) + row(_V_LNB)
    bonus = _dot_exact_rhs(zr * kmod * row(_V_RK), seg) * zv
    o_ref[...] = (yn + bonus) * gate

    @pl.when(tb == pl.num_programs(1) - 1)
    def _():
        mout_ref[...] = m_scr[...]


def _rwkv(z3, shift_prev, m0, wts, t_valid, nsb):
    b, seq, _ = z3.shape
    n_rows = nsb * SUPER
    assert seq % n_rows == 0
    nt = seq // n_rows
    cur = lambda w, c: pl.BlockSpec((None, n_rows, w), lambda i, t: (i, t, c))
    prev = lambda w, c: pl.BlockSpec(
        (None, 8, w), lambda i, t: (i, jnp.maximum(t * (n_rows // 8) - 1, 0), c))
    carry = lambda w: pl.BlockSpec((None, 1, w), lambda i, t: (i, 0, 0))
    state = pl.BlockSpec((None, N_PAIRS, PAIR, PAIR), lambda i, t: (i, 0, 0, 0))
    wag_w = D_LORA_WA + D_G_LORA
    sp = shift_prev[:, None, :]
    return pl.pallas_call(
        functools.partial(_rwkv_kernel, t_valid=None if t_valid == seq else t_valid, nsb=nsb),
        out_shape=(jax.ShapeDtypeStruct((b, seq, C_HEADS), F32),
                   jax.ShapeDtypeStruct((b, N_PAIRS, PAIR, PAIR), F32)),
        grid=(b, nt),
        in_specs=[cur(C_HEADS, 3), cur(C_HEADS, 4), cur(C_HEADS, 5), cur(wag_w, 12),
                  prev(C_HEADS, 3), prev(C_HEADS, 4), prev(C_HEADS, 5), prev(wag_w, 12),
                  carry(C_HEADS), carry(C_HEADS), carry(C_HEADS), carry(wag_w),
                  state,
                  _const_spec((_V_ROWS, C_HEADS)), _const_spec((1, wag_w)),
                  _const_spec((3, PAIR, C_HEADS)), _const_spec((C_HEADS, C_HEADS))],
        out_specs=(pl.BlockSpec((None, n_rows, C_HEADS), lambda i, t: (i, t, 0)), state),
        scratch_shapes=[pltpu.VMEM((N_PAIRS, PAIR, PAIR), F32),
                        pltpu.VMEM((n_rows, C_HEADS), F32)],
        compiler_params=_params("arbitrary", "arbitrary"),
        name="rwkv",
    )(z3, z3, z3, z3, z3, z3, z3, z3,
      sp[:, :, 0:C_HEADS], sp[:, :, C_HEADS:2 * C_HEADS], sp[:, :, 2 * C_HEADS:3 * C_HEADS],
      sp[:, :, 3 * C_HEADS:], m0, wts["vec"], wts["mu_wag"], wts["lora"], wts["seg"])


def _state_to_pairs(s):
    b = s.shape[0]
    st = jnp.swapaxes(s, -1, -2).reshape(b, N_PAIRS, 2, HEAD_DIM, HEAD_DIM)
    eye2 = jnp.eye(2, dtype=s.dtype)
    m = st[:, :, :, :, None, :] * eye2[None, None, :, None, :, None]
    return m.reshape(b, N_PAIRS, PAIR, PAIR)


def _pairs_to_state(m):
    b = m.shape[0]
    m6 = m.reshape(b, N_PAIRS, 2, HEAD_DIM, 2, HEAD_DIM)
    st = jnp.stack([m6[:, :, 0, :, 0, :], m6[:, :, 1, :, 1, :]], axis=2)
    return jnp.swapaxes(st.reshape(b, N_HEADS, HEAD_DIM, HEAD_DIM), -1, -2)


def _pack_layer(w_in, mu_shift, w0, w2, a0, a2, g2, k_k, k_a, r_k, lnx_w, lnx_b, w_out, ln1_w,
                ln1_b, w_gate, w_up, w_down, ln2_w, ln2_b, w_ple_gate, w_ple_proj, ln3_w, ln3_b):
    mu_r, mu_k, mu_v = (mu_shift[i * C_HEADS:(i + 1) * C_HEADS] for i in range(3))
    vec = jnp.stack([mu_r, mu_k, mu_v, w0, a0, k_k, k_a, r_k.reshape(-1), lnx_w, lnx_b])
    vec = jnp.concatenate([vec, jnp.zeros((_V_ROWS - vec.shape[0], C_HEADS), F32)], axis=0)
    half = D_LORA_WA // 2
    zeros = jnp.zeros((half, C_HEADS), F32)
    lora = jnp.stack([jnp.concatenate([w2, zeros], axis=0),
                      jnp.concatenate([zeros, a2], axis=0),
                      g2]).astype(BF)
    head_of = jnp.arange(C_HEADS) // HEAD_DIM
    seg = (head_of[:, None] == head_of[None, :]).astype(BF)
    ln = jnp.stack([ln1_w, ln1_b, ln2_w, ln2_b, ln3_w, ln3_b,
                    jnp.zeros_like(ln1_w), jnp.zeros_like(ln1_w)])
    return {
        "w_in": w_in.astype(BF), "vec": vec, "mu_wag": mu_shift[None, 3 * C_HEADS:],
        "lora": lora, "seg": seg, "w_out": w_out.astype(BF), "w_gate": w_gate.astype(BF),
        "w_up": w_up.astype(BF), "w_down": w_down.astype(BF),
        "w_ple_gate": w_ple_gate.astype(BF), "w_ple_proj": w_ple_proj.astype(BF), "ln": ln,
    }


def _alibi_slopes():
    h = jnp.arange(1, N_HEADS + 1, dtype=F32)
    return jnp.exp2(-8.0 * h / N_HEADS)


def _layer(x, p_l, wts, alpha, shift_prev, wkv_prev, cache_k=None, cache_v=None):
    b, seq, _ = x.shape
    x2d = x.reshape(b * seq, D_MODEL)
    z = _in_proj(x2d, wts["w_in"])
    z3 = z.reshape(b, seq, D_IN)
    slopes = _alibi_slopes()
    if cache_k is None:
        att = _attn_prompt(z3, slopes)
        keep = min(WINDOW_MAX, seq)
        k_win = z3[:, seq - keep:, C_HEADS:2 * C_HEADS].reshape(b, keep, N_HEADS, HEAD_DIM)
        v_win = z3[:, seq - keep:, 2 * C_HEADS:3 * C_HEADS].reshape(b, keep, N_HEADS, HEAD_DIM)
    else:
        cache_len = cache_k.shape[1]
        att, k_win, v_win = _attn_sample(z3, cache_k.reshape(b, cache_len, C_HEADS),
                                         cache_v.reshape(b, cache_len, C_HEADS), slopes)
        k_win = k_win.reshape(cache_k.shape)
        v_win = v_win.reshape(cache_v.shape)
    nsb = RWKV_BLOCKS if seq % (RWKV_BLOCKS * SUPER) == 0 else 1
    pad = (-seq) % (nsb * SUPER)
    z3p = jnp.pad(z3, ((0, 0), (0, pad), (0, 0))) if pad else z3
    rw, m_last = _rwkv(z3p, shift_prev, _state_to_pairs(wkv_prev), wts, seq, nsb)
    rw = rw[:, :seq]
    y = _post(att.reshape(b * seq, C_HEADS), rw.reshape(b * seq, C_HEADS), x2d,
              p_l.reshape(b * seq, PLE_DIM), wts, alpha)
    shift_new = z3[:, seq - 1, 3 * C_HEADS:]
    return y.reshape(b, seq, D_MODEL), k_win, v_win, shift_new, _pairs_to_state(m_last)


def kernel(x_prompt, x_sample, p_prompt, p_sample, cache_k_win, cache_v_win, state_wkv, state_shift, w_in, mu_shift, w0, w2, a0, a2, g2, k_k, k_a, r_k, lnx_w, lnx_b, w_out, ln1_w, ln1_b, w_gate, w_up, w_down, ln2_w, ln2_b, w_ple_gate, w_ple_proj, ln3_w, ln3_b):
    depth = w_in.shape[0]
    alpha = float((2 * depth) ** 0.25)
    xp, xs = x_prompt, x_sample
    bp = x_prompt.shape[0]
    shift0 = jnp.zeros((bp, D_B_IN), x_prompt.dtype)
    wkv0 = jnp.zeros((bp, N_HEADS, HEAD_DIM, HEAD_DIM), state_wkv.dtype)
    outs = [[] for _ in range(8)]
    for l in range(depth):
        wts = _pack_layer(w_in[l], mu_shift[l], w0[l], w2[l], a0[l], a2[l], g2[l], k_k[l], k_a[l],
                          r_k[l], lnx_w[l], lnx_b[l], w_out[l], ln1_w[l], ln1_b[l], w_gate[l],
                          w_up[l], w_down[l], ln2_w[l], ln2_b[l], w_ple_gate[l], w_ple_proj[l],
                          ln3_w[l], ln3_b[l])
        xp, kw, vw, sh, wk = _layer(xp, p_prompt[l], wts, alpha, shift0, wkv0)
        for lst, val in zip(outs[0:4], (kw, vw, wk, sh)):
            lst.append(val)
        xs, kw, vw, sh, wk = _layer(xs, p_sample[l], wts, alpha, state_shift[l], state_wkv[l],
                                    cache_k_win[l], cache_v_win[l])
        for lst, val in zip(outs[4:8], (kw, vw, wk, sh)):
            lst.append(val)
    return (xp, xs) + tuple(jnp.stack(o) for o in outs)
```

```python
import functools

import jax
import jax.numpy as jnp
from jax import lax
from jax.experimental import pallas as pl
from jax.experimental.pallas import tpu as pltpu

BF = jnp.bfloat16
F32 = jnp.float32

D_MODEL = 1024
HEAD_DIM = 64
N_HEADS = 8
C_HEADS = N_HEADS * HEAD_DIM
PAIR = 2 * HEAD_DIM
N_PAIRS = N_HEADS // 2
DILATIONS = ((128, 1), (512, 4), (2048, 16))
WINDOW_MAX = 2048
Q_BLOCK = 128
D_LORA_WA = 128
D_G_LORA = 128
D_B_IN = 3 * C_HEADS + D_LORA_WA + D_G_LORA
D_IN = 3 * C_HEADS + D_B_IN
D_FF = 2816
PLE_DIM = 256
LN_EPS = 1e-5
GN_EPS = 64e-5
NEG_BIG = -1e30

CHUNK = 16
SUPER = 64
N_CHUNKS = SUPER // CHUNK
RWKV_BLOCKS = 4
FF_CHUNK = 256
ATTN_UNROLL = 4
IN_CHUNK = 256
VMEM_LIMIT_BYTES = 56 * 1024 * 1024

(_V_MU_R, _V_MU_K, _V_MU_V, _V_W0, _V_A0, _V_KK, _V_KA, _V_RK, _V_LNW, _V_LNB) = range(10)
_V_ROWS = 16


def _params(*sem):
    return pltpu.CompilerParams(dimension_semantics=sem, vmem_limit_bytes=VMEM_LIMIT_BYTES)


def _const_spec(shape):
    nd = len(shape)
    return pl.BlockSpec(shape, lambda *_: (0,) * nd, pipeline_mode=pl.Buffered(1))


def _dot(a, b):
    return jnp.dot(a.astype(BF), b.astype(BF), preferred_element_type=F32)


def _dot_nt(a, b):
    return lax.dot_general(a.astype(BF), b.astype(BF), (((1,), (1,)), ((), ())),
                           preferred_element_type=F32)


def _split(x):
    hi = x.astype(BF)
    lo = (x - hi.astype(F32)).astype(BF)
    return hi, lo


def _dot_exact_lhs(mask_bf, x):
    hi, lo = _split(x)
    return (jnp.dot(mask_bf, hi, preferred_element_type=F32)
            + jnp.dot(mask_bf, lo, preferred_element_type=F32))


def _dot_exact_rhs(x, mask_bf):
    hi, lo = _split(x)
    return (jnp.dot(hi, mask_bf, preferred_element_type=F32)
            + jnp.dot(lo, mask_bf, preferred_element_type=F32))


def _layer_norm(x, w, b):
    mu = jnp.mean(x, axis=-1, keepdims=True)
    xc = x - mu
    var = jnp.mean(xc * xc, axis=-1, keepdims=True)
    return xc * lax.rsqrt(var + LN_EPS) * w + b


def _in_proj_kernel(x_ref, w_ref, o_ref):
    xb = x_ref[...].astype(BF)
    for c in range(D_IN // IN_CHUNK):
        o_ref[:, c * IN_CHUNK:(c + 1) * IN_CHUNK] = jnp.dot(xb, w_ref[c],
                                                            preferred_element_type=F32)


def _in_proj(x2d, w_bf):
    m = x2d.shape[0]
    tm = min(512, m)
    assert m % tm == 0
    return pl.pallas_call(
        _in_proj_kernel,
        out_shape=jax.ShapeDtypeStruct((m, D_IN), F32),
        grid=(m // tm,),
        in_specs=[pl.BlockSpec((tm, D_MODEL), lambda i: (i, 0)),
                  _const_spec((D_IN // IN_CHUNK, D_MODEL, IN_CHUNK))],
        out_specs=pl.BlockSpec((tm, D_IN), lambda i: (i, 0)),
        compiler_params=_params("arbitrary"),
        name="in_proj",
    )(x2d, w_bf)


def _post_kernel(att_ref, rw_ref, x_ref, p_ref, wo_ref, wg_ref, wu_ref, wd_ref, wpg_ref,
                 wpp_ref, ln_ref, o_ref, *, alpha):
    mix = (_dot(att_ref[...], wo_ref[0:C_HEADS, :])
           + _dot(rw_ref[...], wo_ref[C_HEADS:2 * C_HEADS, :]))
    h = _layer_norm(alpha * x_ref[...] + mix, ln_ref[0:1, :], ln_ref[1:2, :])
    hb = h.astype(BF)
    ffn = jnp.zeros_like(h)
    for c in range(D_FF // FF_CHUNK):
        g = jnp.dot(hb, wg_ref[c], preferred_element_type=F32)
        u = jnp.dot(hb, wu_ref[c], preferred_element_type=F32)
        act = g * jax.nn.sigmoid(g) * u
        ffn = ffn + jnp.dot(act.astype(BF), wd_ref[c * FF_CHUNK:(c + 1) * FF_CHUNK, :],
                            preferred_element_type=F32)
    h = _layer_norm(alpha * h + ffn, ln_ref[2:3, :], ln_ref[3:4, :])
    ple = jax.nn.sigmoid(_dot(h, wpg_ref[...])) * _dot(p_ref[...], wpp_ref[...])
    o_ref[...] = _layer_norm(alpha * h + ple, ln_ref[4:5, :], ln_ref[5:6, :])


def _post(att, rw, x2d, p2d, wts, alpha):
    m = x2d.shape[0]
    tm = min(512, m)
    assert m % tm == 0
    row = lambda w: pl.BlockSpec((tm, w), lambda i: (i, 0))
    return pl.pallas_call(
        functools.partial(_post_kernel, alpha=alpha),
        out_shape=jax.ShapeDtypeStruct((m, D_MODEL), F32),
        grid=(m // tm,),
        in_specs=[row(C_HEADS), row(C_HEADS), row(D_MODEL), row(PLE_DIM),
                  _const_spec((2 * C_HEADS, D_MODEL)),
                  _const_spec((D_FF // FF_CHUNK, D_MODEL, FF_CHUNK)),
                  _const_spec((D_FF // FF_CHUNK, D_MODEL, FF_CHUNK)),
                  _const_spec((D_FF, D_MODEL)),
                  _const_spec((D_MODEL, D_MODEL)), _const_spec((PLE_DIM, D_MODEL)),
                  _const_spec((8, D_MODEL))],
        out_specs=row(D_MODEL),
        compiler_params=_params("arbitrary"),
        name="post",
    )(att, rw, x2d, p2d, wts["w_out"], wts["w_gate"], wts["w_up"], wts["w_down"],
      wts["w_ple_gate"], wts["w_ple_proj"], wts["ln"])


def _attn_prompt_kernel(sl_ref, q_ref, k_ref, v_ref, o_ref, m_scr, l_scr, mb_scr, *, seq):
    pair = pl.program_id(1)
    slopes = (sl_ref[2 * pair], sl_ref[2 * pair + 1])
    lane = lax.broadcasted_iota(jnp.int32, (1, PAIR), 1)
    head0 = lane < HEAD_DIM
    qi0 = lax.broadcasted_iota(jnp.int32, (Q_BLOCK, 1), 0)
    ki0 = lax.broadcasted_iota(jnp.int32, (1, 2 * Q_BLOCK), 1)

    m_scr[...] = jnp.full(m_scr.shape, NEG_BIG, F32)
    l_scr[...] = jnp.zeros(l_scr.shape, F32)
    o_ref[...] = jnp.zeros(o_ref.shape, F32)

    for window, dil in DILATIONS:
        span = window // dil
        nb = seq // dil // Q_BLOCK
        n_units = nb * dil
        assert nb >= 2 and n_units * Q_BLOCK == seq and n_units % ATTN_UNROLL == 0
        for case in range(2):
            dist = (qi0 + case * Q_BLOCK) - ki0
            valid = (dist >= 0) & (dist <= span)
            distf = (dist * dil).astype(F32)
            for e in range(2):
                mb_scr[2 * case + e] = jnp.where(valid, -slopes[e] * distf, NEG_BIG)

        def group(g, carry, dil=dil):
            loaded = []
            for j in range(ATTN_UNROLL):
                u = g * ATTN_UNROLL + j
                n = u >> (dil.bit_length() - 1)
                r = u & (dil - 1)
                nprev = jnp.maximum(n - 1, 0)
                qstart = n * (Q_BLOCK * dil) + r
                kstart = nprev * (Q_BLOCK * dil) + r
                if dil == 1:
                    qsl = pl.ds(qstart, Q_BLOCK)
                    ksl = pl.ds(kstart, 2 * Q_BLOCK)
                else:
                    qsl = pl.ds(qstart, Q_BLOCK, stride=dil)
                    ksl = pl.ds(kstart, 2 * Q_BLOCK, stride=dil)
                q = q_ref[qsl, :] * (HEAD_DIM ** -0.5)
                loaded.append((qsl, jnp.minimum(n, 1), q, k_ref[ksl, :].astype(BF),
                               v_ref[ksl, :].astype(BF), m_scr[qsl, :], l_scr[qsl, :],
                               o_ref[qsl, :]))
            scores = [[_dot_nt(jnp.where(head0 if e == 0 else ~head0, q, 0.0), k)
                       + mb_scr[2 * case + e] for e in range(2)]
                      for (_, case, q, k, _, _, _, _) in loaded]
            stats = []
            for (_, _, _, _, _, m_old, _, _), s2 in zip(loaded, scores):
                mo = [m_old[:, e * HEAD_DIM:e * HEAD_DIM + 1] for e in range(2)]
                mn = [jnp.maximum(mo[e], jnp.max(s2[e], axis=1, keepdims=True))
                      for e in range(2)]
                p = [jnp.exp(s2[e] - mn[e]).astype(BF) for e in range(2)]
                al_swapped = jnp.where(head0, jnp.exp(mo[1] - mn[1]), jnp.exp(mo[0] - mn[0]))
                stats.append((jnp.where(head0, mn[0], mn[1]), p, al_swapped))
            pvs = [[jnp.dot(p[e], jnp.where(head0 if e == 0 else ~head0, v, jnp.ones_like(v)),
                            preferred_element_type=F32) for e in range(2)]
                   for (_, _, _, _, v, _, _, _), (_, p, _) in zip(loaded, stats)]
            for (qsl, _, _, _, _, m_old, l_old, a_old), (m_new, _, al_swapped), pv in zip(
                    loaded, stats, pvs):
                m_scr[qsl, :] = m_new
                l_scr[qsl, :] = al_swapped * l_old + jnp.where(head0, pv[1], pv[0])
                o_ref[qsl, :] = jnp.exp(m_old - m_new) * a_old + jnp.where(head0, pv[0], pv[1])
            return carry

        lax.fori_loop(0, n_units // ATTN_UNROLL, group, 0)

    o_ref[...] = o_ref[...] / pltpu.roll(l_scr[...], HEAD_DIM, 1)


def _attn_prompt(z3, slopes):
    b, seq, _ = z3.shape
    col = lambda off: pl.BlockSpec((None, seq, PAIR), lambda i, p: (i, 0, off + p))
    return pl.pallas_call(
        functools.partial(_attn_prompt_kernel, seq=seq),
        out_shape=jax.ShapeDtypeStruct((b, seq, C_HEADS), F32),
        grid=(b, N_PAIRS),
        in_specs=[pl.BlockSpec(memory_space=pltpu.SMEM),
                  col(0), col(N_PAIRS), col(2 * N_PAIRS)],
        out_specs=pl.BlockSpec((None, seq, PAIR), lambda i, p: (i, 0, p)),
        scratch_shapes=[pltpu.VMEM((seq, PAIR), F32), pltpu.VMEM((seq, PAIR), F32),
                        pltpu.VMEM((4, Q_BLOCK, 2 * Q_BLOCK), F32)],
        compiler_params=_params("arbitrary", "arbitrary"),
        name="attn_prompt",
    )(slopes, z3, z3, z3)


def _attn_sample_kernel(sl_ref, q_ref, kn_ref, vn_ref, ck_ref, cv_ref, o_ref, ko_ref, vo_ref,
                        *, s_new, cache_len):
    lane_t = lax.broadcasted_iota(jnp.int32, (1, PAIR), 1)

    def shift_in(c_ref, new_ref, out_ref):
        rolled = pltpu.roll(c_ref[...], cache_len - s_new, 1)
        new_t = jnp.concatenate([new_ref[...], jnp.zeros((PAIR - s_new, C_HEADS), F32)], axis=0).T
        tail = jnp.where(lane_t >= PAIR - s_new, pltpu.roll(new_t, PAIR - s_new, 1),
                         rolled[:, cache_len - PAIR:cache_len])
        out_ref[:, 0:cache_len - PAIR] = rolled[:, 0:cache_len - PAIR]
        out_ref[:, cache_len - PAIR:cache_len] = tail

    shift_in(ck_ref, kn_ref, ko_ref)
    shift_in(cv_ref, vn_ref, vo_ref)

    rows = s_new * N_HEADS
    ri = lax.broadcasted_iota(jnp.int32, (rows, 1), 0)
    lane = lax.broadcasted_iota(jnp.int32, (1, C_HEADS), 1)
    own = (ri & (N_HEADS - 1)) == (lane >> 6)
    q = q_ref[...]
    qe = jnp.broadcast_to(q[:, None, :], (s_new, N_HEADS, C_HEADS)).reshape(rows, C_HEADS)
    qe = jnp.where(own, qe, 0.0)
    slope = sl_ref[:, 0:1]
    spos = ri >> 3

    def weights(dist):
        mult = jnp.zeros(dist.shape, F32)
        for window, dil in DILATIONS:
            hit = (dist >= 0) & (dist <= window) & ((dist & (dil - 1)) == 0)
            mult = mult + hit.astype(F32)
        return mult

    dist_c = cache_len + spos - lax.broadcasted_iota(jnp.int32, (1, cache_len), 1)
    dist_n = spos - lax.broadcasted_iota(jnp.int32, (1, s_new), 1)
    mult_c = weights(dist_c)
    mult_n = weights(dist_n)
    sc = _dot(qe, ck_ref[...]) * (HEAD_DIM ** -0.5) - slope * dist_c.astype(F32)
    sn = _dot_nt(qe, kn_ref[...]) * (HEAD_DIM ** -0.5) - slope * dist_n.astype(F32)
    sc = jnp.where(mult_c > 0, sc, NEG_BIG)
    sn = jnp.where(mult_n > 0, sn, NEG_BIG)
    mx = jnp.maximum(jnp.max(sc, axis=1, keepdims=True), jnp.max(sn, axis=1, keepdims=True))
    pc = mult_c * jnp.exp(sc - mx)
    pn = mult_n * jnp.exp(sn - mx)
    den = jnp.sum(pc, axis=1, keepdims=True) + jnp.sum(pn, axis=1, keepdims=True)
    num = _dot_nt(pc, cv_ref[...]) + _dot(pn, vn_ref[...])
    num = jnp.where(own, num, 0.0).reshape(s_new, N_HEADS, C_HEADS).sum(axis=1)
    den = jnp.where(own, den, 0.0).reshape(s_new, N_HEADS, C_HEADS).sum(axis=1)
    o_ref[...] = num / den


def _attn_sample(z3, cache_k, cache_v, slopes):
    b, s_new, _ = z3.shape
    cache_len = cache_k.shape[2]
    assert cache_len % PAIR == 0 and cache_len > PAIR and s_new < PAIR
    new = lambda c: pl.BlockSpec((None, s_new, C_HEADS), lambda i: (i, 0, c))
    cache = pl.BlockSpec((None, C_HEADS, cache_len), lambda i: (i, 0, 0))
    slope_rows = jnp.broadcast_to(jnp.tile(slopes, s_new)[:, None], (s_new * N_HEADS, PAIR))
    return pl.pallas_call(
        functools.partial(_attn_sample_kernel, s_new=s_new, cache_len=cache_len),
        out_shape=(jax.ShapeDtypeStruct((b, s_new, C_HEADS), F32),
                   jax.ShapeDtypeStruct(cache_k.shape, F32),
                   jax.ShapeDtypeStruct(cache_v.shape, F32)),
        grid=(b,),
        in_specs=[_const_spec((s_new * N_HEADS, PAIR)), new(0), new(1), new(2), cache, cache],
        out_specs=(new(0), cache, cache),
        compiler_params=_params("arbitrary"),
        name="attn_sample",
    )(slope_rows, z3, z3, z3, cache_k, cache_v)


def _rwkv_kernel(r_ref, k_ref, v_ref, wag_ref, pr_ref, pk_ref, pv_ref, pwag_ref,
                 sr_ref, sk_ref, sv_ref, swag_ref, m0_ref, vec_ref, muwag_ref, lora_ref, seg_ref,
                 o_ref, mout_ref, m_scr, y_scr, *, t_valid, nsb):
    tb = pl.program_id(1)
    first = tb == 0
    n_rows = nsb * SUPER

    @pl.when(first)
    def _():
        m_scr[...] = m0_ref[...]

    rows = lax.broadcasted_iota(jnp.int32, (n_rows, 1), 0)

    def token_shift(cur_ref, prev_ref, carry_ref, mu):
        cur = cur_ref[...]
        last = jnp.where(first, carry_ref[...], prev_ref[7:8, :])
        prev = jnp.where(rows == 0, last, pltpu.roll(cur, 1, 0))
        return cur + (prev - cur) * mu

    vec = vec_ref[...]
    row = lambda i: vec[i:i + 1, :]
    zr = token_shift(r_ref, pr_ref, sr_ref, row(_V_MU_R))
    zk = token_shift(k_ref, pk_ref, sk_ref, row(_V_MU_K))
    zv = token_shift(v_ref, pv_ref, sv_ref, row(_V_MU_V))
    zwag = token_shift(wag_ref, pwag_ref, swag_ref, muwag_ref[...])
    wa = zwag[:, 0:D_LORA_WA]
    gi = zwag[:, D_LORA_WA:D_LORA_WA + D_G_LORA]

    wlin = row(_V_W0) + _dot(jnp.tanh(wa), lora_ref[0])
    softplus = jnp.maximum(-wlin, 0.0) + jnp.log(1.0 + jnp.exp(-jnp.abs(wlin)))
    w_log = -softplus - 0.5
    ld = -jnp.exp(w_log)
    lr = jax.nn.sigmoid(row(_V_A0) + _dot(wa, lora_ref[1]))
    gate = _dot(jax.nn.sigmoid(gi), lora_ref[2])
    seg = seg_ref[...]
    kk = zk * row(_V_KK)
    kk = kk / jnp.maximum(jnp.sqrt(_dot_exact_rhs(kk * kk, seg)), 1e-12)
    kmod = zk * (1.0 + (lr - 1.0) * row(_V_KA))
    vv = zv
    if t_valid is not None:
        live = (rows + tb * n_rows) < t_valid
        ld = jnp.where(live, ld, 0.0)
        kk = jnp.where(live, kk, 0.0)
        kmod = jnp.where(live, kmod, 0.0)
        vv = jnp.where(live, vv, 0.0)

    ti = lax.broadcasted_iota(jnp.int32, (n_rows, n_rows), 0)
    tj = lax.broadcasted_iota(jnp.int32, (n_rows, n_rows), 1)
    same_chunk = (ti >> 4) == (tj >> 4)
    tril = (same_chunk & (ti >= tj)).astype(BF)
    ones = same_chunk.astype(BF)
    cum = _dot_exact_lhs(tril, ld)
    tot = _dot_exact_lhs(ones, ld)
    dec_in = jnp.exp(cum)
    dec_ex = jnp.exp(cum - ld)
    dec_inv = jnp.exp(-cum)
    dec_end = jnp.exp(tot - cum)
    dec_tot = jnp.exp(tot)
    beta = kk * lr
    abar = -(kk * dec_ex)
    rbar = zr * dec_in
    bt = beta * dec_inv
    kt = kmod * dec_inv
    bh = beta * dec_end
    kh = kmod * dec_end

    ri = lax.broadcasted_iota(jnp.int32, (PAIR, PAIR), 0)
    ci = lax.broadcasted_iota(jnp.int32, (PAIR, PAIR), 1)
    same16 = (ri >> 4) == (ci >> 4)
    strict = same16 & (ri > ci)
    incl = same16 & (ri >= ci)
    same_head = (ri >> 6) == (ci >> 6)
    eye = ri == ci
    lane = lax.broadcasted_iota(jnp.int32, (1, PAIR), 1)
    head0 = lane < HEAD_DIM
    zeros_sp = jnp.zeros((SUPER, PAIR), F32)
    zeros_pp = jnp.zeros((PAIR, PAIR), F32)

    def stack(x):
        return jnp.concatenate([jnp.where(head0, x, 0.0), jnp.where(head0, 0.0, x)], axis=0)

    def unstack(x):
        return x[0:SUPER, :] + x[SUPER:PAIR, :]

    units = [(s, p) for s in range(nsb) for p in range(N_PAIRS)]
    tile = lambda x, u: x[u[0] * SUPER:(u[0] + 1) * SUPER, u[1] * PAIR:(u[1] + 1) * PAIR]
    ab = [tile(abar, u) for u in units]
    rb = [tile(rbar, u) for u in units]
    v_p = [tile(vv, u) for u in units]
    v_s = [stack(x) for x in v_p]
    a_all = [_dot_nt(jnp.concatenate([a, a, r, r], axis=0),
                     jnp.concatenate([stack(tile(bt, u)), stack(tile(kt, u))], axis=0))
             for a, r, u in zip(ab, rb, units)]
    n_ab = [jnp.where(strict, a[0:PAIR, 0:PAIR], 0.0) for a in a_all]
    a_ak = [jnp.where(strict, a[0:PAIR, PAIR:2 * PAIR], 0.0) for a in a_all]
    a_rbk = [jnp.concatenate([jnp.where(incl, a[PAIR:2 * PAIR, 0:PAIR], 0.0),
                              jnp.where(incl, a[PAIR:2 * PAIR, PAIR:2 * PAIR], 0.0)], axis=1)
             for a in a_all]
    eye_f = jnp.where(eye, 1.0, 0.0)
    tinv = [eye_f + n for n in n_ab]
    power = n_ab
    for _ in range(3):
        power = [_dot(x, x) for x in power]
        tinv = [t + _dot(x, t) for x, t in zip(power, tinv)]
    u_s = [_dot(a, v) for a, v in zip(a_ak, v_s)]
    ta = [_dot(t, jnp.concatenate([stack(a), u], axis=1))
          for t, a, u in zip(tinv, ab, u_s)]
    ry = [_dot(a, jnp.concatenate([t, jnp.concatenate([zeros_pp, v], axis=1)], axis=0))
          for a, t, v in zip(a_rbk, ta, v_s)]
    r1 = [unstack(stack(r) + y[:, 0:PAIR]) for r, y in zip(rb, ry)]
    y0 = [unstack(y[:, PAIR:2 * PAIR]) for y in ry]
    bk_t = [jnp.concatenate([tile(bh, u), tile(kh, u)], axis=0).T
            for u in units]
    rhs3 = [jnp.concatenate([jnp.concatenate([unstack(t[:, 0:PAIR]), unstack(t[:, PAIR:2 * PAIR])],
                                             axis=1),
                             jnp.concatenate([zeros_sp, v], axis=1)], axis=0)
            for t, v in zip(ta, v_p)]
    in_chunk = [((lane & (SUPER - 1)) >> 4) == c for c in range(N_CHUNKS)]
    gh = [[_dot(jnp.where(in_chunk[c], b, 0.0), r) for b, r in zip(bk_t, rhs3)]
          for c in range(N_CHUNKS)]

    m = [m_scr[p] for p in range(N_PAIRS)]
    for s in range(nsb):
        for c in range(N_CHUNKS):
            tok = slice(c * CHUNK, (c + 1) * CHUNK)
            r0 = s * SUPER + c * CHUNK
            for p in range(N_PAIRS):
                u = s * N_PAIRS + p
                sl = slice(p * PAIR, (p + 1) * PAIR)
                g_c = (jnp.where(eye, dec_tot[r0:r0 + 1, sl], 0.0)
                       + jnp.where(same_head, gh[c][u][:, 0:PAIR], 0.0))
                h_c = jnp.where(same_head, gh[c][u][:, PAIR:2 * PAIR], 0.0)
                y_scr[r0:r0 + CHUNK, sl] = _dot(r1[u][tok, :], m[p]) + y0[u][tok, :]
                m[p] = _dot(g_c, m[p]) + h_c
    for p in range(N_PAIRS):
        m_scr[p] = m[p]

    y = y_scr[...]
    mean = _dot_exact_rhs(y, seg) * (1.0 / HEAD_DIM)
    yc = y - mean
    var = _dot_exact_rhs(yc * yc, seg) * (1.0 / HEAD_DIM)
    yn = yc * lax.rsqrt(var + GN_EPS) * row(_V_LNW) + row(_V_LNB)
    bonus = _dot_exact_rhs(zr * kmod * row(_V_RK), seg) * zv
    o_ref[...] = (yn + bonus) * gate

    @pl.when(tb == pl.num_programs(1) - 1)
    def _():
        mout_ref[...] = m_scr[...]


def _rwkv(z3, shift_prev, m0, wts, t_valid, nsb):
    b, seq, _ = z3.shape
    n_rows = nsb * SUPER
    assert seq % n_rows == 0
    nt = seq // n_rows
    cur = lambda w, c: pl.BlockSpec((None, n_rows, w), lambda i, t: (i, t, c))
    prev = lambda w, c: pl.BlockSpec(
        (None, 8, w), lambda i, t: (i, jnp.maximum(t * (n_rows // 8) - 1, 0), c))
    carry = lambda w: pl.BlockSpec((None, 1, w), lambda i, t: (i, 0, 0))
    state = pl.BlockSpec((None, N_PAIRS, PAIR, PAIR), lambda i, t: (i, 0, 0, 0))
    wag_w = D_LORA_WA + D_G_LORA
    sp = shift_prev[:, None, :]
    return pl.pallas_call(
        functools.partial(_rwkv_kernel, t_valid=None if t_valid == seq else t_valid, nsb=nsb),
        out_shape=(jax.ShapeDtypeStruct((b, seq, C_HEADS), F32),
                   jax.ShapeDtypeStruct((b, N_PAIRS, PAIR, PAIR), F32)),
        grid=(b, nt),
        in_specs=[cur(C_HEADS, 3), cur(C_HEADS, 4), cur(C_HEADS, 5), cur(wag_w, 12),
                  prev(C_HEADS, 3), prev(C_HEADS, 4), prev(C_HEADS, 5), prev(wag_w, 12),
                  carry(C_HEADS), carry(C_HEADS), carry(C_HEADS), carry(wag_w),
                  state,
                  _const_spec((_V_ROWS, C_HEADS)), _const_spec((1, wag_w)),
                  _const_spec((3, PAIR, C_HEADS)), _const_spec((C_HEADS, C_HEADS))],
        out_specs=(pl.BlockSpec((None, n_rows, C_HEADS), lambda i, t: (i, t, 0)), state),
        scratch_shapes=[pltpu.VMEM((N_PAIRS, PAIR, PAIR), F32),
                        pltpu.VMEM((n_rows, C_HEADS), F32)],
        compiler_params=_params("arbitrary", "arbitrary"),
        name="rwkv",
    )(z3, z3, z3, z3, z3, z3, z3, z3,
      sp[:, :, 0:C_HEADS], sp[:, :, C_HEADS:2 * C_HEADS], sp[:, :, 2 * C_HEADS:3 * C_HEADS],
      sp[:, :, 3 * C_HEADS:], m0, wts["vec"], wts["mu_wag"], wts["lora"], wts["seg"])


def _state_to_pairs(s):
    b = s.shape[0]
    st = jnp.swapaxes(s, -1, -2).reshape(b, N_PAIRS, 2, HEAD_DIM, HEAD_DIM)
    eye2 = jnp.eye(2, dtype=s.dtype)
    m = st[:, :, :, :, None, :] * eye2[None, None, :, None, :, None]
    return m.reshape(b, N_PAIRS, PAIR, PAIR)


def _pairs_to_state(m):
    b = m.shape[0]
    m6 = m.reshape(b, N_PAIRS, 2, HEAD_DIM, 2, HEAD_DIM)
    st = jnp.stack([m6[:, :, 0, :, 0, :], m6[:, :, 1, :, 1, :]], axis=2)
    return jnp.swapaxes(st.reshape(b, N_HEADS, HEAD_DIM, HEAD_DIM), -1, -2)


def _pack_layer(w_in, mu_shift, w0, w2, a0, a2, g2, k_k, k_a, r_k, lnx_w, lnx_b, w_out, ln1_w,
                ln1_b, w_gate, w_up, w_down, ln2_w, ln2_b, w_ple_gate, w_ple_proj, ln3_w, ln3_b):
    mu_r, mu_k, mu_v = (mu_shift[i * C_HEADS:(i + 1) * C_HEADS] for i in range(3))
    vec = jnp.stack([mu_r, mu_k, mu_v, w0, a0, k_k, k_a, r_k.reshape(-1), lnx_w, lnx_b])
    vec = jnp.concatenate([vec, jnp.zeros((_V_ROWS - vec.shape[0], C_HEADS), F32)], axis=0)
    half = D_LORA_WA // 2
    zeros = jnp.zeros((half, C_HEADS), F32)
    lora = jnp.stack([jnp.concatenate([w2, zeros], axis=0),
                      jnp.concatenate([zeros, a2], axis=0),
                      g2]).astype(BF)
    head_of = jnp.arange(C_HEADS) // HEAD_DIM
    seg = (head_of[:, None] == head_of[None, :]).astype(BF)
    ln = jnp.stack([ln1_w, ln1_b, ln2_w, ln2_b, ln3_w, ln3_b,
                    jnp.zeros_like(ln1_w), jnp.zeros_like(ln1_w)])

    def col_chunks(w, width):
        k, n = w.shape
        return jnp.swapaxes(w.astype(BF).reshape(k, n // width, width), 0, 1)

    return {
        "w_in": col_chunks(w_in, IN_CHUNK), "vec": vec, "mu_wag": mu_shift[None, 3 * C_HEADS:],
        "lora": lora, "seg": seg, "w_out": w_out.astype(BF),
        "w_gate": col_chunks(w_gate, FF_CHUNK), "w_up": col_chunks(w_up, FF_CHUNK),
        "w_down": w_down.astype(BF),
        "w_ple_gate": w_ple_gate.astype(BF), "w_ple_proj": w_ple_proj.astype(BF), "ln": ln,
    }


def _alibi_slopes():
    h = jnp.arange(1, N_HEADS + 1, dtype=F32)
    return jnp.exp2(-8.0 * h / N_HEADS)


def _layer(x, p_l, wts, alpha, shift_prev, wkv_prev, cache_k=None, cache_v=None):
    b, seq, _ = x.shape
    x2d = x.reshape(b * seq, D_MODEL)
    z = _in_proj(x2d, wts["w_in"])
    z3 = z.reshape(b, seq, D_IN)
    slopes = _alibi_slopes()
    if cache_k is None:
        att = _attn_prompt(z3, slopes)
        keep = min(WINDOW_MAX, seq)
        k_win = z3[:, seq - keep:, C_HEADS:2 * C_HEADS].reshape(b, keep, N_HEADS, HEAD_DIM)
        v_win = z3[:, seq - keep:, 2 * C_HEADS:3 * C_HEADS].reshape(b, keep, N_HEADS, HEAD_DIM)
    else:
        cache_len = cache_k.shape[1]
        to_cm = lambda c: jnp.transpose(c, (0, 2, 3, 1)).reshape(b, C_HEADS, cache_len)
        from_cm = lambda c: jnp.transpose(c.reshape(b, N_HEADS, HEAD_DIM, cache_len), (0, 3, 1, 2))
        att, k_win, v_win = _attn_sample(z3, to_cm(cache_k), to_cm(cache_v), slopes)
        k_win = from_cm(k_win)
        v_win = from_cm(v_win)
    nsb = RWKV_BLOCKS if seq % (RWKV_BLOCKS * SUPER) == 0 else 1
    pad = (-seq) % (nsb * SUPER)
    z3p = jnp.pad(z3, ((0, 0), (0, pad), (0, 0))) if pad else z3
    rw, m_last = _rwkv(z3p, shift_prev, _state_to_pairs(wkv_prev), wts, seq, nsb)
    rw = rw[:, :seq]
    y = _post(att.reshape(b * seq, C_HEADS), rw.reshape(b * seq, C_HEADS), x2d,
              p_l.reshape(b * seq, PLE_DIM), wts, alpha)
    shift_new = z3[:, seq - 1, 3 * C_HEADS:]
    return y.reshape(b, seq, D_MODEL), k_win, v_win, shift_new, _pairs_to_state(m_last)


def kernel(x_prompt, x_sample, p_prompt, p_sample, cache_k_win, cache_v_win, state_wkv, state_shift, w_in, mu_shift, w0, w2, a0, a2, g2, k_k, k_a, r_k, lnx_w, lnx_b, w_out, ln1_w, ln1_b, w_gate, w_up, w_down, ln2_w, ln2_b, w_ple_gate, w_ple_proj, ln3_w, ln3_b):
    depth = w_in.shape[0]
    alpha = float((2 * depth) ** 0.25)
    xp, xs = x_prompt, x_sample
    bp = x_prompt.shape[0]
    shift0 = jnp.zeros((bp, D_B_IN), x_prompt.dtype)
    wkv0 = jnp.zeros((bp, N_HEADS, HEAD_DIM, HEAD_DIM), state_wkv.dtype)
    outs = [[] for _ in range(8)]
    for l in range(depth):
        wts = _pack_layer(w_in[l], mu_shift[l], w0[l], w2[l], a0[l], a2[l], g2[l], k_k[l], k_a[l],
                          r_k[l], lnx_w[l], lnx_b[l], w_out[l], ln1_w[l], ln1_b[l], w_gate[l],
                          w_up[l], w_down[l], ln2_w[l], ln2_b[l], w_ple_gate[l], w_ple_proj[l],
                          ln3_w[l], ln3_b[l])
        xp, kw, vw, sh, wk = _layer(xp, p_prompt[l], wts, alpha, shift0, wkv0)
        for lst, val in zip(outs[0:4], (kw, vw, wk, sh)):
            lst.append(val)
        xs, kw, vw, sh, wk = _layer(xs, p_sample[l], wts, alpha, state_shift[l], state_wkv[l],
                                    cache_k_win[l], cache_v_win[l])
        for lst, val in zip(outs[4:8], (kw, vw, wk, sh)):
            lst.append(val)
    return (xp, xs) + tuple(jnp.stack(o) for o in outs)
```

```python
import functools

import jax
import jax.numpy as jnp
from jax import lax
from jax.experimental import pallas as pl
from jax.experimental.pallas import tpu as pltpu

BF = jnp.bfloat16
F32 = jnp.float32

D_MODEL = 1024
HEAD_DIM = 64
N_HEADS = 8
C_HEADS = N_HEADS * HEAD_DIM
PAIR = 2 * HEAD_DIM
N_PAIRS = N_HEADS // 2
DILATIONS = ((128, 1), (512, 4), (2048, 16))
WINDOW_MAX = 2048
Q_BLOCK = 128
DIL_MAX = max(d for _, d in DILATIONS)
D_LORA_WA = 128
D_G_LORA = 128
D_B_IN = 3 * C_HEADS + D_LORA_WA + D_G_LORA
D_IN = 3 * C_HEADS + D_B_IN
D_FF = 2816
PLE_DIM = 256
LN_EPS = 1e-5
GN_EPS = 64e-5
NEG_BIG = -1e30

CHUNK = 16
SUPER = 64
N_CHUNKS = SUPER // CHUNK
RWKV_BLOCKS = 4
FF_CHUNK = 256
ATTN_UNROLL = 4
IN_CHUNK = 256
VMEM_LIMIT_BYTES = 56 * 1024 * 1024

(_V_MU_R, _V_MU_K, _V_MU_V, _V_W0, _V_A0, _V_KK, _V_KA, _V_RK, _V_LNW, _V_LNB) = range(10)
_V_ROWS = 16


def _params(*sem):
    return pltpu.CompilerParams(dimension_semantics=sem, vmem_limit_bytes=VMEM_LIMIT_BYTES)


def _const_spec(shape):
    nd = len(shape)
    return pl.BlockSpec(shape, lambda *_: (0,) * nd, pipeline_mode=pl.Buffered(1))


def _dot(a, b):
    return jnp.dot(a.astype(BF), b.astype(BF), preferred_element_type=F32)


def _dot_nt(a, b):
    return lax.dot_general(a.astype(BF), b.astype(BF), (((1,), (1,)), ((), ())),
                           preferred_element_type=F32)


def _split(x):
    hi = x.astype(BF)
    lo = (x - hi.astype(F32)).astype(BF)
    return hi, lo


def _dot_exact_lhs(mask_bf, x):
    hi, lo = _split(x)
    return (jnp.dot(mask_bf, hi, preferred_element_type=F32)
            + jnp.dot(mask_bf, lo, preferred_element_type=F32))


def _dot_exact_rhs(x, mask_bf):
    hi, lo = _split(x)
    return (jnp.dot(hi, mask_bf, preferred_element_type=F32)
            + jnp.dot(lo, mask_bf, preferred_element_type=F32))


def _layer_norm(x, w, b):
    mu = jnp.mean(x, axis=-1, keepdims=True)
    xc = x - mu
    var = jnp.mean(xc * xc, axis=-1, keepdims=True)
    return xc * lax.rsqrt(var + LN_EPS) * w + b


def _in_proj_kernel(x_ref, w_ref, o_ref):
    xb = x_ref[...].astype(BF)
    for c in range(D_IN // IN_CHUNK):
        o_ref[:, c * IN_CHUNK:(c + 1) * IN_CHUNK] = jnp.dot(xb, w_ref[c],
                                                            preferred_element_type=F32)


def _in_proj(x2d, w_bf):
    m = x2d.shape[0]
    tm = min(512, m)
    assert m % tm == 0
    return pl.pallas_call(
        _in_proj_kernel,
        out_shape=jax.ShapeDtypeStruct((m, D_IN), F32),
        grid=(m // tm,),
        in_specs=[pl.BlockSpec((tm, D_MODEL), lambda i: (i, 0)),
                  _const_spec((D_IN // IN_CHUNK, D_MODEL, IN_CHUNK))],
        out_specs=pl.BlockSpec((tm, D_IN), lambda i: (i, 0)),
        compiler_params=_params("arbitrary"),
        name="in_proj",
    )(x2d, w_bf)


def _post_kernel(att_ref, rw_ref, x_ref, p_ref, wo_ref, wg_ref, wu_ref, wd_ref, wpg_ref,
                 wpp_ref, ln_ref, o_ref, *, alpha):
    mix = (_dot(att_ref[...], wo_ref[0:C_HEADS, :])
           + _dot(rw_ref[...], wo_ref[C_HEADS:2 * C_HEADS, :]))
    h = _layer_norm(alpha * x_ref[...] + mix, ln_ref[0:1, :], ln_ref[1:2, :])
    hb = h.astype(BF)
    ffn = jnp.zeros_like(h)
    for c in range(D_FF // FF_CHUNK):
        g = jnp.dot(hb, wg_ref[c], preferred_element_type=F32)
        u = jnp.dot(hb, wu_ref[c], preferred_element_type=F32)
        act = g * jax.nn.sigmoid(g) * u
        ffn = ffn + jnp.dot(act.astype(BF), wd_ref[c * FF_CHUNK:(c + 1) * FF_CHUNK, :],
                            preferred_element_type=F32)
    h = _layer_norm(alpha * h + ffn, ln_ref[2:3, :], ln_ref[3:4, :])
    ple = jax.nn.sigmoid(_dot(h, wpg_ref[...])) * _dot(p_ref[...], wpp_ref[...])
    o_ref[...] = _layer_norm(alpha * h + ple, ln_ref[4:5, :], ln_ref[5:6, :])


def _post(att, rw, x2d, p2d, wts, alpha):
    m = x2d.shape[0]
    tm = min(512, m)
    assert m % tm == 0
    row = lambda w: pl.BlockSpec((tm, w), lambda i: (i, 0))
    return pl.pallas_call(
        functools.partial(_post_kernel, alpha=alpha),
        out_shape=jax.ShapeDtypeStruct((m, D_MODEL), F32),
        grid=(m // tm,),
        in_specs=[row(C_HEADS), row(C_HEADS), row(D_MODEL), row(PLE_DIM),
                  _const_spec((2 * C_HEADS, D_MODEL)),
                  _const_spec((D_FF // FF_CHUNK, D_MODEL, FF_CHUNK)),
                  _const_spec((D_FF // FF_CHUNK, D_MODEL, FF_CHUNK)),
                  _const_spec((D_FF, D_MODEL)),
                  _const_spec((D_MODEL, D_MODEL)), _const_spec((PLE_DIM, D_MODEL)),
                  _const_spec((8, D_MODEL))],
        out_specs=row(D_MODEL),
        compiler_params=_params("arbitrary"),
        name="post",
    )(att, rw, x2d, p2d, wts["w_out"], wts["w_gate"], wts["w_up"], wts["w_down"],
      wts["w_ple_gate"], wts["w_ple_proj"], wts["ln"])


def _attn_prompt_kernel(sl_ref, q_ref, k_ref, v_ref, o_ref, q_scr, k_scr, v_scr, a_scr, m_scr,
                        l_scr, mb_scr, *, seq):
    pair = pl.program_id(1)
    slopes = (sl_ref[2 * pair], sl_ref[2 * pair + 1])
    lane = lax.broadcasted_iota(jnp.int32, (1, PAIR), 1)
    head0 = lane < HEAD_DIM
    qi = lax.broadcasted_iota(jnp.int32, (Q_BLOCK, 1), 0)
    ki = lax.broadcasted_iota(jnp.int32, (1, 2 * Q_BLOCK), 1)

    per = seq // DIL_MAX
    for r in range(DIL_MAX):
        src = pl.ds(r, per, stride=DIL_MAX)
        dst = slice(r * per, (r + 1) * per)
        q_scr[dst, :] = q_ref[src, :] * (HEAD_DIM ** -0.5)
        k_scr[dst, :] = k_ref[src, :].astype(BF)
        v_scr[dst, :] = v_ref[src, :].astype(BF)

    def gather(ref, runs):
        return jnp.concatenate([ref[rr, :] for rr in runs], axis=0)

    for window, dil in DILATIONS:
        n_runs = DIL_MAX // dil
        w = Q_BLOCK // n_runs
        nb = seq // dil // Q_BLOCK
        n_units = nb * dil
        assert nb >= 2 and n_units * Q_BLOCK == seq and n_units % ATTN_UNROLL == 0 and w % 8 == 0
        tq = (qi & (w - 1)) * DIL_MAX + (qi >> (w.bit_length() - 1)) * dil
        tk = (ki & (2 * w - 1)) * DIL_MAX + (ki >> (w.bit_length())) * dil
        if dil == 1:
            tk = ki
        for case in range(2):
            dist = case * Q_BLOCK * dil + tq - tk
            valid = (dist >= 0) & (dist <= window)
            distf = dist.astype(F32)
            mb_scr[case] = jnp.concatenate(
                [jnp.where(valid, -slopes[e] * distf, NEG_BIG) for e in range(2)], axis=0)

        def group(g, carry, dil=dil, n_runs=n_runs, w=w, first=(window, dil) == DILATIONS[0]):
            loaded = []
            for j in range(ATTN_UNROLL):
                u = g * ATTN_UNROLL + j
                n = u >> (dil.bit_length() - 1)
                rho = u & (dil - 1)
                nprev = jnp.maximum(n - 1, 0)
                base = [(dil * jj + rho) * per for jj in range(n_runs)]
                qruns = [pl.ds(pl.multiple_of(b + w * n, 8), w) for b in base]
                q = gather(q_scr, qruns)
                qq = jnp.concatenate([jnp.where(head0, q, 0.0), jnp.where(head0, 0.0, q)], axis=0)
                if dil == 1:
                    keys = pl.ds(pl.multiple_of(nprev * Q_BLOCK, Q_BLOCK), 2 * Q_BLOCK)
                    k = k_ref[keys, :].astype(BF)
                    v = v_ref[keys, :].astype(BF)
                else:
                    kruns = [pl.ds(pl.multiple_of(b + w * nprev, 16), 2 * w) for b in base]
                    k = gather(k_scr, kruns)
                    v = gather(v_scr, kruns)
                old = None if first else (
                    jnp.concatenate([gather(m_scr.at[0], qruns), gather(m_scr.at[1], qruns)], axis=0),
                    jnp.concatenate([gather(l_scr.at[0], qruns), gather(l_scr.at[1], qruns)], axis=0),
                    gather(a_scr, qruns))
                loaded.append((qruns, jnp.minimum(n, 1), qq.astype(BF), k,
                               jnp.concatenate([v, jnp.ones_like(v)], axis=1),
                               old))
            scores = [_dot_nt(qq, k) + mb_scr[case] for (_, case, qq, k, _, _) in loaded]
            stats = []
            for (_, _, _, _, _, old), s in zip(loaded, scores):
                m_new = jnp.max(s, axis=1, keepdims=True)
                m_new = (jnp.broadcast_to(m_new, (2 * Q_BLOCK, PAIR)) if first
                         else jnp.maximum(old[0], m_new))
                p = jnp.exp(s - jnp.concatenate([m_new, m_new], axis=1)).astype(BF)
                stats.append((m_new, p, None if first else jnp.exp(old[0] - m_new)))
            pvs = [jnp.dot(p, vo, preferred_element_type=F32)
                   for (_, _, _, _, vo, _), (_, p, _) in zip(loaded, stats)]
            for (qruns, _, _, _, _, old), (m_new, _, al), pv in zip(loaded, stats, pvs):
                l_new = pv[:, PAIR:2 * PAIR]
                a_new = (pv[0:Q_BLOCK, 0:PAIR], pv[Q_BLOCK:, 0:PAIR])
                if not first:
                    l_new = al * old[1] + l_new
                    a_new = (al[0:Q_BLOCK, :] * old[2] + a_new[0], al[Q_BLOCK:, :] * old[2] + a_new[1])
                a_new = jnp.where(head0, a_new[0], a_new[1])
                for jj, rr in enumerate(qruns):
                    a_scr[rr, :] = a_new[jj * w:(jj + 1) * w, :]
                    for e in range(2):
                        m_scr[e, rr, :] = m_new[e * Q_BLOCK + jj * w:e * Q_BLOCK + (jj + 1) * w, :]
                        l_scr[e, rr, :] = l_new[e * Q_BLOCK + jj * w:e * Q_BLOCK + (jj + 1) * w, :]
            return carry

        lax.fori_loop(0, n_units // ATTN_UNROLL, group, 0)

    for r in range(DIL_MAX):
        src = slice(r * per, (r + 1) * per)
        o_ref[pl.ds(r, per, stride=DIL_MAX), :] = a_scr[src, :] / jnp.where(
            head0, l_scr[0, src, :], l_scr[1, src, :])


def _attn_prompt(z3, slopes):
    b, seq, _ = z3.shape
    col = lambda off: pl.BlockSpec((None, seq, PAIR), lambda i, p: (i, 0, off + p))
    return pl.pallas_call(
        functools.partial(_attn_prompt_kernel, seq=seq),
        out_shape=jax.ShapeDtypeStruct((b, seq, C_HEADS), F32),
        grid=(b, N_PAIRS),
        in_specs=[pl.BlockSpec(memory_space=pltpu.SMEM),
                  col(0), col(N_PAIRS), col(2 * N_PAIRS)],
        out_specs=pl.BlockSpec((None, seq, PAIR), lambda i, p: (i, 0, p)),
        scratch_shapes=[pltpu.VMEM((seq, PAIR), F32), pltpu.VMEM((seq, PAIR), BF),
                        pltpu.VMEM((seq, PAIR), BF), pltpu.VMEM((seq, PAIR), F32)]
        + [pltpu.VMEM((2, seq, PAIR), F32)] * 2
        + [pltpu.VMEM((2, 2 * Q_BLOCK, 2 * Q_BLOCK), F32)],
        compiler_params=_params("arbitrary", "arbitrary"),
        name="attn_prompt",
    )(slopes, z3, z3, z3)


def _attn_sample_kernel(sl_ref, q_ref, kn_ref, vn_ref, ck_ref, cv_ref, o_ref, ko_ref, vo_ref,
                        *, s_new, cache_len):
    lane_t = lax.broadcasted_iota(jnp.int32, (1, PAIR), 1)

    def shift_in(c_ref, new_ref, out_ref):
        rolled = pltpu.roll(c_ref[...], cache_len - s_new, 1)
        new_t = jnp.concatenate([new_ref[...], jnp.zeros((PAIR - s_new, C_HEADS), F32)], axis=0).T
        tail = jnp.where(lane_t >= PAIR - s_new, pltpu.roll(new_t, PAIR - s_new, 1),
                         rolled[:, cache_len - PAIR:cache_len])
        out_ref[:, 0:cache_len - PAIR] = rolled[:, 0:cache_len - PAIR]
        out_ref[:, cache_len - PAIR:cache_len] = tail

    shift_in(ck_ref, kn_ref, ko_ref)
    shift_in(cv_ref, vn_ref, vo_ref)

    rows = s_new * N_HEADS
    ri = lax.broadcasted_iota(jnp.int32, (rows, 1), 0)
    lane = lax.broadcasted_iota(jnp.int32, (1, C_HEADS), 1)
    own = (ri & (N_HEADS - 1)) == (lane >> 6)
    q = q_ref[...]
    qe = jnp.broadcast_to(q[:, None, :], (s_new, N_HEADS, C_HEADS)).reshape(rows, C_HEADS)
    qe = jnp.where(own, qe, 0.0)
    slope = sl_ref[:, 0:1]
    spos = ri >> 3

    def weights(dist):
        mult = jnp.zeros(dist.shape, F32)
        for window, dil in DILATIONS:
            hit = (dist >= 0) & (dist <= window) & ((dist & (dil - 1)) == 0)
            mult = mult + hit.astype(F32)
        return mult

    dist_c = cache_len + spos - lax.broadcasted_iota(jnp.int32, (1, cache_len), 1)
    dist_n = spos - lax.broadcasted_iota(jnp.int32, (1, s_new), 1)
    mult_c = weights(dist_c)
    mult_n = weights(dist_n)
    sc = _dot(qe, ck_ref[...]) * (HEAD_DIM ** -0.5) - slope * dist_c.astype(F32)
    sn = _dot_nt(qe, kn_ref[...]) * (HEAD_DIM ** -0.5) - slope * dist_n.astype(F32)
    sc = jnp.where(mult_c > 0, sc, NEG_BIG)
    sn = jnp.where(mult_n > 0, sn, NEG_BIG)
    mx = jnp.maximum(jnp.max(sc, axis=1, keepdims=True), jnp.max(sn, axis=1, keepdims=True))
    pc = mult_c * jnp.exp(sc - mx)
    pn = mult_n * jnp.exp(sn - mx)
    den = jnp.sum(pc, axis=1, keepdims=True) + jnp.sum(pn, axis=1, keepdims=True)
    num = _dot_nt(pc, cv_ref[...]) + _dot(pn, vn_ref[...])
    num = jnp.where(own, num, 0.0).reshape(s_new, N_HEADS, C_HEADS).sum(axis=1)
    den = jnp.where(own, den, 0.0).reshape(s_new, N_HEADS, C_HEADS).sum(axis=1)
    o_ref[...] = num / den


def _attn_sample(z3, cache_k, cache_v, slopes):
    b, s_new, _ = z3.shape
    cache_len = cache_k.shape[2]
    assert cache_len % PAIR == 0 and cache_len > PAIR and s_new < PAIR
    new = lambda c: pl.BlockSpec((None, s_new, C_HEADS), lambda i: (i, 0, c))
    cache = pl.BlockSpec((None, C_HEADS, cache_len), lambda i: (i, 0, 0))
    slope_rows = jnp.broadcast_to(jnp.tile(slopes, s_new)[:, None], (s_new * N_HEADS, PAIR))
    return pl.pallas_call(
        functools.partial(_attn_sample_kernel, s_new=s_new, cache_len=cache_len),
        out_shape=(jax.ShapeDtypeStruct((b, s_new, C_HEADS), F32),
                   jax.ShapeDtypeStruct(cache_k.shape, F32),
                   jax.ShapeDtypeStruct(cache_v.shape, F32)),
        grid=(b,),
        in_specs=[_const_spec((s_new * N_HEADS, PAIR)), new(0), new(1), new(2), cache, cache],
        out_specs=(new(0), cache, cache),
        compiler_params=_params("arbitrary"),
        name="attn_sample",
    )(slope_rows, z3, z3, z3, cache_k, cache_v)


def _rwkv_kernel(r_ref, k_ref, v_ref, wag_ref, pr_ref, pk_ref, pv_ref, pwag_ref,
                 sr_ref, sk_ref, sv_ref, swag_ref, m0_ref, vec_ref, muwag_ref, lora_ref, seg_ref,
                 o_ref, mout_ref, m_scr, y_scr, *, t_valid, nsb):
    tb = pl.program_id(1)
    first = tb == 0
    n_rows = nsb * SUPER

    @pl.when(first)
    def _():
        m_scr[...] = m0_ref[...]

    rows = lax.broadcasted_iota(jnp.int32, (n_rows, 1), 0)

    def token_shift(cur_ref, prev_ref, carry_ref, mu):
        cur = cur_ref[...]
        last = jnp.where(first, carry_ref[...], prev_ref[7:8, :])
        prev = jnp.where(rows == 0, last, pltpu.roll(cur, 1, 0))
        return cur + (prev - cur) * mu

    vec = vec_ref[...]
    row = lambda i: vec[i:i + 1, :]
    zr = token_shift(r_ref, pr_ref, sr_ref, row(_V_MU_R))
    zk = token_shift(k_ref, pk_ref, sk_ref, row(_V_MU_K))
    zv = token_shift(v_ref, pv_ref, sv_ref, row(_V_MU_V))
    zwag = token_shift(wag_ref, pwag_ref, swag_ref, muwag_ref[...])
    wa = zwag[:, 0:D_LORA_WA]
    gi = zwag[:, D_LORA_WA:D_LORA_WA + D_G_LORA]

    wlin = row(_V_W0) + _dot(jnp.tanh(wa), lora_ref[0])
    softplus = jnp.maximum(-wlin, 0.0) + jnp.log(1.0 + jnp.exp(-jnp.abs(wlin)))
    w_log = -softplus - 0.5
    ld = -jnp.exp(w_log)
    lr = jax.nn.sigmoid(row(_V_A0) + _dot(wa, lora_ref[1]))
    gate = _dot(jax.nn.sigmoid(gi), lora_ref[2])
    seg = seg_ref[...]
    kk = zk * row(_V_KK)
    kk = kk / jnp.maximum(jnp.sqrt(_dot_exact_rhs(kk * kk, seg)), 1e-12)
    kmod = zk * (1.0 + (lr - 1.0) * row(_V_KA))
    vv = zv
    if t_valid is not None:
        live = (rows + tb * n_rows) < t_valid
        ld = jnp.where(live, ld, 0.0)
        kk = jnp.where(live, kk, 0.0)
        kmod = jnp.where(live, kmod, 0.0)
        vv = jnp.where(live, vv, 0.0)

    ti = lax.broadcasted_iota(jnp.int32, (n_rows, n_rows), 0)
    tj = lax.broadcasted_iota(jnp.int32, (n_rows, n_rows), 1)
    same_chunk = (ti >> 4) == (tj >> 4)
    tril = (same_chunk & (ti >= tj)).astype(BF)
    ones = same_chunk.astype(BF)
    cum = _dot_exact_lhs(tril, ld)
    tot = _dot_exact_lhs(ones, ld)
    dec_in = jnp.exp(cum)
    dec_ex = jnp.exp(cum - ld)
    dec_inv = jnp.exp(-cum)
    dec_end = jnp.exp(tot - cum)
    dec_tot = jnp.exp(tot)
    beta = kk * lr
    abar = -(kk * dec_ex)
    rbar = zr * dec_in
    bt = beta * dec_inv
    kt = kmod * dec_inv
    bh = beta * dec_end
    kh = kmod * dec_end

    ri = lax.broadcasted_iota(jnp.int32, (PAIR, PAIR), 0)
    ci = lax.broadcasted_iota(jnp.int32, (PAIR, PAIR), 1)
    same16 = (ri >> 4) == (ci >> 4)
    strict = same16 & (ri > ci)
    incl = same16 & (ri >= ci)
    same_head = (ri >> 6) == (ci >> 6)
    eye = ri == ci
    lane = lax.broadcasted_iota(jnp.int32, (1, PAIR), 1)
    head0 = lane < HEAD_DIM
    zeros_sp = jnp.zeros((SUPER, PAIR), F32)
    zeros_pp = jnp.zeros((PAIR, PAIR), F32)

    def stack(x):
        return jnp.concatenate([jnp.where(head0, x, 0.0), jnp.where(head0, 0.0, x)], axis=0)

    def unstack(x):
        return x[0:SUPER, :] + x[SUPER:PAIR, :]

    units = [(s, p) for s in range(nsb) for p in range(N_PAIRS)]
    tile = lambda x, u: x[u[0] * SUPER:(u[0] + 1) * SUPER, u[1] * PAIR:(u[1] + 1) * PAIR]
    ab = [tile(abar, u) for u in units]
    rb = [tile(rbar, u) for u in units]
    v_p = [tile(vv, u) for u in units]
    v_s = [stack(x) for x in v_p]
    a_all = [_dot_nt(jnp.concatenate([a, a, r, r], axis=0),
                     jnp.concatenate([stack(tile(bt, u)), stack(tile(kt, u))], axis=0))
             for a, r, u in zip(ab, rb, units)]
    n_ab = [jnp.where(strict, a[0:PAIR, 0:PAIR], 0.0) for a in a_all]
    a_ak = [jnp.where(strict, a[0:PAIR, PAIR:2 * PAIR], 0.0) for a in a_all]
    a_rbk = [jnp.concatenate([jnp.where(incl, a[PAIR:2 * PAIR, 0:PAIR], 0.0),
                              jnp.where(incl, a[PAIR:2 * PAIR, PAIR:2 * PAIR], 0.0)], axis=1)
             for a in a_all]
    eye_f = jnp.where(eye, 1.0, 0.0)
    tinv = [eye_f + n for n in n_ab]
    power = n_ab
    for _ in range(3):
        power = [_dot(x, x) for x in power]
        tinv = [t + _dot(x, t) for x, t in zip(power, tinv)]
    u_s = [_dot(a, v) for a, v in zip(a_ak, v_s)]
    ta = [_dot(t, jnp.concatenate([stack(a), u], axis=1))
          for t, a, u in zip(tinv, ab, u_s)]
    ry = [_dot(a, jnp.concatenate([t, jnp.concatenate([zeros_pp, v], axis=1)], axis=0))
          for a, t, v in zip(a_rbk, ta, v_s)]
    r1 = [unstack(stack(r) + y[:, 0:PAIR]) for r, y in zip(rb, ry)]
    y0 = [unstack(y[:, PAIR:2 * PAIR]) for y in ry]
    bk_t = [jnp.concatenate([tile(bh, u), tile(kh, u)], axis=0).T
            for u in units]
    rhs3 = [jnp.concatenate([jnp.concatenate([unstack(t[:, 0:PAIR]), unstack(t[:, PAIR:2 * PAIR])],
                                             axis=1),
                             jnp.concatenate([zeros_sp, v], axis=1)], axis=0)
            for t, v in zip(ta, v_p)]
    in_chunk = [((lane & (SUPER - 1)) >> 4) == c for c in range(N_CHUNKS)]
    gh = [[_dot(jnp.where(in_chunk[c], b, 0.0), r) for b, r in zip(bk_t, rhs3)]
          for c in range(N_CHUNKS)]

    m = [m_scr[p] for p in range(N_PAIRS)]
    for s in range(nsb):
        for c in range(N_CHUNKS):
            tok = slice(c * CHUNK, (c + 1) * CHUNK)
            r0 = s * SUPER + c * CHUNK
            for p in range(N_PAIRS):
                u = s * N_PAIRS + p
                sl = slice(p * PAIR, (p + 1) * PAIR)
                g_c = (jnp.where(eye, dec_tot[r0:r0 + 1, sl], 0.0)
                       + jnp.where(same_head, gh[c][u][:, 0:PAIR], 0.0))
                h_c = jnp.where(same_head, gh[c][u][:, PAIR:2 * PAIR], 0.0)
                y_scr[r0:r0 + CHUNK, sl] = _dot(r1[u][tok, :], m[p]) + y0[u][tok, :]
                m[p] = _dot(g_c, m[p]) + h_c
    for p in range(N_PAIRS):
        m_scr[p] = m[p]

    y = y_scr[...]
    mean = _dot_exact_rhs(y, seg) * (1.0 / HEAD_DIM)
    yc = y - mean
    var = _dot_exact_rhs(yc * yc, seg) * (1.0 / HEAD_DIM)
    yn = yc * lax.rsqrt(var + GN_EPS) * row(_V_LNW) + row(_V_LNB)
    bonus = _dot_exact_rhs(zr * kmod * row(_V_RK), seg) * zv
    o_ref[...] = (yn + bonus) * gate

    @pl.when(tb == pl.num_programs(1) - 1)
    def _():
        mout_ref[...] = m_scr[...]


def _rwkv(z3, shift_prev, m0, wts, t_valid, nsb):
    b, seq, _ = z3.shape
    n_rows = nsb * SUPER
    assert seq % n_rows == 0
    nt = seq // n_rows
    cur = lambda w, c: pl.BlockSpec((None, n_rows, w), lambda i, t: (i, t, c))
    prev = lambda w, c: pl.BlockSpec(
        (None, 8, w), lambda i, t: (i, jnp.maximum(t * (n_rows // 8) - 1, 0), c))
    carry = lambda w: pl.BlockSpec((None, 1, w), lambda i, t: (i, 0, 0))
    state = pl.BlockSpec((None, N_PAIRS, PAIR, PAIR), lambda i, t: (i, 0, 0, 0))
    wag_w = D_LORA_WA + D_G_LORA
    sp = shift_prev[:, None, :]
    return pl.pallas_call(
        functools.partial(_rwkv_kernel, t_valid=None if t_valid == seq else t_valid, nsb=nsb),
        out_shape=(jax.ShapeDtypeStruct((b, seq, C_HEADS), F32),
                   jax.ShapeDtypeStruct((b, N_PAIRS, PAIR, PAIR), F32)),
        grid=(b, nt),
        in_specs=[cur(C_HEADS, 3), cur(C_HEADS, 4), cur(C_HEADS, 5), cur(wag_w, 12),
                  prev(C_HEADS, 3), prev(C_HEADS, 4), prev(C_HEADS, 5), prev(wag_w, 12),
                  carry(C_HEADS), carry(C_HEADS), carry(C_HEADS), carry(wag_w),
                  state,
                  _const_spec((_V_ROWS, C_HEADS)), _const_spec((1, wag_w)),
                  _const_spec((3, PAIR, C_HEADS)), _const_spec((C_HEADS, C_HEADS))],
        out_specs=(pl.BlockSpec((None, n_rows, C_HEADS), lambda i, t: (i, t, 0)), state),
        scratch_shapes=[pltpu.VMEM((N_PAIRS, PAIR, PAIR), F32),
                        pltpu.VMEM((n_rows, C_HEADS), F32)],
        compiler_params=_params("arbitrary", "arbitrary"),
        name="rwkv",
    )(z3, z3, z3, z3, z3, z3, z3, z3,
      sp[:, :, 0:C_HEADS], sp[:, :, C_HEADS:2 * C_HEADS], sp[:, :, 2 * C_HEADS:3 * C_HEADS],
      sp[:, :, 3 * C_HEADS:], m0, wts["vec"], wts["mu_wag"], wts["lora"], wts["seg"])


def _state_to_pairs(s):
    b = s.shape[0]
    st = jnp.swapaxes(s, -1, -2).reshape(b, N_PAIRS, 2, HEAD_DIM, HEAD_DIM)
    eye2 = jnp.eye(2, dtype=s.dtype)
    m = st[:, :, :, :, None, :] * eye2[None, None, :, None, :, None]
    return m.reshape(b, N_PAIRS, PAIR, PAIR)


def _pairs_to_state(m):
    b = m.shape[0]
    m6 = m.reshape(b, N_PAIRS, 2, HEAD_DIM, 2, HEAD_DIM)
    st = jnp.stack([m6[:, :, 0, :, 0, :], m6[:, :, 1, :, 1, :]], axis=2)
    return jnp.swapaxes(st.reshape(b, N_HEADS, HEAD_DIM, HEAD_DIM), -1, -2)


def _pack_layer(w_in, mu_shift, w0, w2, a0, a2, g2, k_k, k_a, r_k, lnx_w, lnx_b, w_out, ln1_w,
                ln1_b, w_gate, w_up, w_down, ln2_w, ln2_b, w_ple_gate, w_ple_proj, ln3_w, ln3_b):
    mu_r, mu_k, mu_v = (mu_shift[i * C_HEADS:(i + 1) * C_HEADS] for i in range(3))
    vec = jnp.stack([mu_r, mu_k, mu_v, w0, a0, k_k, k_a, r_k.reshape(-1), lnx_w, lnx_b])
    vec = jnp.concatenate([vec, jnp.zeros((_V_ROWS - vec.shape[0], C_HEADS), F32)], axis=0)
    half = D_LORA_WA // 2
    zeros = jnp.zeros((half, C_HEADS), F32)
    lora = jnp.stack([jnp.concatenate([w2, zeros], axis=0),
                      jnp.concatenate([zeros, a2], axis=0),
                      g2]).astype(BF)
    head_of = jnp.arange(C_HEADS) // HEAD_DIM
    seg = (head_of[:, None] == head_of[None, :]).astype(BF)
    ln = jnp.stack([ln1_w, ln1_b, ln2_w, ln2_b, ln3_w, ln3_b,
                    jnp.zeros_like(ln1_w), jnp.zeros_like(ln1_w)])

    def col_chunks(w, width):
        k, n = w.shape
        return jnp.swapaxes(w.astype(BF).reshape(k, n // width, width), 0, 1)

    return {
        "w_in": col_chunks(w_in, IN_CHUNK), "vec": vec, "mu_wag": mu_shift[None, 3 * C_HEADS:],
        "lora": lora, "seg": seg, "w_out": w_out.astype(BF),
        "w_gate": col_chunks(w_gate, FF_CHUNK), "w_up": col_chunks(w_up, FF_CHUNK),
        "w_down": w_down.astype(BF),
        "w_ple_gate": w_ple_gate.astype(BF), "w_ple_proj": w_ple_proj.astype(BF), "ln": ln,
    }


def _alibi_slopes():
    h = jnp.arange(1, N_HEADS + 1, dtype=F32)
    return jnp.exp2(-8.0 * h / N_HEADS)


def _layer(x, p_l, wts, alpha, shift_prev, wkv_prev, cache_k=None, cache_v=None):
    b, seq, _ = x.shape
    x2d = x.reshape(b * seq, D_MODEL)
    z = _in_proj(x2d, wts["w_in"])
    z3 = z.reshape(b, seq, D_IN)
    slopes = _alibi_slopes()
    if cache_k is None:
        att = _attn_prompt(z3, slopes)
        keep = min(WINDOW_MAX, seq)
        k_win = z3[:, seq - keep:, C_HEADS:2 * C_HEADS].reshape(b, keep, N_HEADS, HEAD_DIM)
        v_win = z3[:, seq - keep:, 2 * C_HEADS:3 * C_HEADS].reshape(b, keep, N_HEADS, HEAD_DIM)
    else:
        cache_len = cache_k.shape[1]
        to_cm = lambda c: jnp.transpose(c, (0, 2, 3, 1)).reshape(b, C_HEADS, cache_len)
        from_cm = lambda c: jnp.transpose(c.reshape(b, N_HEADS, HEAD_DIM, cache_len), (0, 3, 1, 2))
        att, k_win, v_win = _attn_sample(z3, to_cm(cache_k), to_cm(cache_v), slopes)
        k_win = from_cm(k_win)
        v_win = from_cm(v_win)
    nsb = RWKV_BLOCKS if seq % (RWKV_BLOCKS * SUPER) == 0 else 1
    pad = (-seq) % (nsb * SUPER)
    z3p = jnp.pad(z3, ((0, 0), (0, pad), (0, 0))) if pad else z3
    rw, m_last = _rwkv(z3p, shift_prev, _state_to_pairs(wkv_prev), wts, seq, nsb)
    rw = rw[:, :seq]
    y = _post(att.reshape(b * seq, C_HEADS), rw.reshape(b * seq, C_HEADS), x2d,
              p_l.reshape(b * seq, PLE_DIM), wts, alpha)
    shift_new = z3[:, seq - 1, 3 * C_HEADS:]
    return y.reshape(b, seq, D_MODEL), k_win, v_win, shift_new, _pairs_to_state(m_last)


def kernel(x_prompt, x_sample, p_prompt, p_sample, cache_k_win, cache_v_win, state_wkv, state_shift, w_in, mu_shift, w0, w2, a0, a2, g2, k_k, k_a, r_k, lnx_w, lnx_b, w_out, ln1_w, ln1_b, w_gate, w_up, w_down, ln2_w, ln2_b, w_ple_gate, w_ple_proj, ln3_w, ln3_b):
    depth = w_in.shape[0]
    alpha = float((2 * depth) ** 0.25)
    xp, xs = x_prompt, x_sample
    bp = x_prompt.shape[0]
    shift0 = jnp.zeros((bp, D_B_IN), x_prompt.dtype)
    wkv0 = jnp.zeros((bp, N_HEADS, HEAD_DIM, HEAD_DIM), state_wkv.dtype)
    outs = [[] for _ in range(8)]
    for l in range(depth):
        wts = _pack_layer(w_in[l], mu_shift[l], w0[l], w2[l], a0[l], a2[l], g2[l], k_k[l], k_a[l],
                          r_k[l], lnx_w[l], lnx_b[l], w_out[l], ln1_w[l], ln1_b[l], w_gate[l],
                          w_up[l], w_down[l], ln2_w[l], ln2_b[l], w_ple_gate[l], w_ple_proj[l],
                          ln3_w[l], ln3_b[l])
        xp, kw, vw, sh, wk = _layer(xp, p_prompt[l], wts, alpha, shift0, wkv0)
        for lst, val in zip(outs[0:4], (kw, vw, wk, sh)):
            lst.append(val)
        xs, kw, vw, sh, wk = _layer(xs, p_sample[l], wts, alpha, state_shift[l], state_wkv[l],
                                    cache_k_win[l], cache_v_win[l])
        for lst, val in zip(outs[4:8], (kw, vw, wk, sh)):
            lst.append(val)
    return (xp, xs) + tuple(jnp.stack(o) for o in outs)
```

```python
import functools

import jax
import jax.numpy as jnp
from jax import lax
from jax.experimental import pallas as pl
from jax.experimental.pallas import tpu as pltpu

BF = jnp.bfloat16
F32 = jnp.float32

D_MODEL = 1024
HEAD_DIM = 64
N_HEADS = 8
C_HEADS = N_HEADS * HEAD_DIM
PAIR = 2 * HEAD_DIM
N_PAIRS = N_HEADS // 2
DILATIONS = ((128, 1), (512, 4), (2048, 16))
WINDOW_MAX = 2048
Q_BLOCK = 128
DIL_MAX = max(d for _, d in DILATIONS)
D_LORA_WA = 128
D_G_LORA = 128
D_B_IN = 3 * C_HEADS + D_LORA_WA + D_G_LORA
D_IN = 3 * C_HEADS + D_B_IN
D_FF = 2816
PLE_DIM = 256
LN_EPS = 1e-5
GN_EPS = 64e-5
NEG_BIG = -1e30

CHUNK = 16
SUPER = 64
N_CHUNKS = SUPER // CHUNK
RWKV_BLOCKS = 4
FF_CHUNK = 256
ATTN_UNROLL = 4
IN_CHUNK = 256
VMEM_LIMIT_BYTES = 56 * 1024 * 1024

(_V_MU_R, _V_MU_K, _V_MU_V, _V_W0, _V_A0, _V_KK, _V_KA, _V_RK, _V_LNW, _V_LNB) = range(10)
_V_ROWS = 16


def _params(*sem):
    return pltpu.CompilerParams(dimension_semantics=sem, vmem_limit_bytes=VMEM_LIMIT_BYTES)


def _const_spec(shape):
    nd = len(shape)
    return pl.BlockSpec(shape, lambda *_: (0,) * nd, pipeline_mode=pl.Buffered(1))


def _dot(a, b):
    return jnp.dot(a.astype(BF), b.astype(BF), preferred_element_type=F32)


def _dot_nt(a, b):
    return lax.dot_general(a.astype(BF), b.astype(BF), (((1,), (1,)), ((), ())),
                           preferred_element_type=F32)


def _split(x):
    hi = x.astype(BF)
    lo = (x - hi.astype(F32)).astype(BF)
    return hi, lo


def _dot_exact_lhs(mask_bf, x):
    hi, lo = _split(x)
    return (jnp.dot(mask_bf, hi, preferred_element_type=F32)
            + jnp.dot(mask_bf, lo, preferred_element_type=F32))


def _dot_exact_rhs(x, mask_bf):
    hi, lo = _split(x)
    return (jnp.dot(hi, mask_bf, preferred_element_type=F32)
            + jnp.dot(lo, mask_bf, preferred_element_type=F32))


def _layer_norm(x, w, b):
    mu = jnp.mean(x, axis=-1, keepdims=True)
    xc = x - mu
    var = jnp.mean(xc * xc, axis=-1, keepdims=True)
    return xc * lax.rsqrt(var + LN_EPS) * w + b


def _in_proj_kernel(x_ref, w_ref, o_ref):
    xb = x_ref[...].astype(BF)
    for c in range(D_IN // IN_CHUNK):
        o_ref[:, c * IN_CHUNK:(c + 1) * IN_CHUNK] = jnp.dot(xb, w_ref[c],
                                                            preferred_element_type=F32)


def _in_proj(x2d, w_bf):
    m = x2d.shape[0]
    tm = min(512, m)
    assert m % tm == 0
    return pl.pallas_call(
        _in_proj_kernel,
        out_shape=jax.ShapeDtypeStruct((m, D_IN), F32),
        grid=(m // tm,),
        in_specs=[pl.BlockSpec((tm, D_MODEL), lambda i: (i, 0)),
                  _const_spec((D_IN // IN_CHUNK, D_MODEL, IN_CHUNK))],
        out_specs=pl.BlockSpec((tm, D_IN), lambda i: (i, 0)),
        compiler_params=_params("arbitrary"),
        name="in_proj",
    )(x2d, w_bf)


def _post_kernel(att_ref, rw_ref, x_ref, p_ref, wo_ref, wg_ref, wu_ref, wd_ref, wpg_ref,
                 wpp_ref, ln_ref, o_ref, *, alpha):
    mix = (_dot(att_ref[...], wo_ref[0:C_HEADS, :])
           + _dot(rw_ref[...], wo_ref[C_HEADS:2 * C_HEADS, :]))
    h = _layer_norm(alpha * x_ref[...] + mix, ln_ref[0:1, :], ln_ref[1:2, :])
    hb = h.astype(BF)
    ffn = jnp.zeros_like(h)
    for c in range(D_FF // FF_CHUNK):
        g = jnp.dot(hb, wg_ref[c], preferred_element_type=F32)
        u = jnp.dot(hb, wu_ref[c], preferred_element_type=F32)
        act = g * jax.nn.sigmoid(g) * u
        ffn = ffn + jnp.dot(act.astype(BF), wd_ref[c * FF_CHUNK:(c + 1) * FF_CHUNK, :],
                            preferred_element_type=F32)
    h = _layer_norm(alpha * h + ffn, ln_ref[2:3, :], ln_ref[3:4, :])
    ple = jax.nn.sigmoid(_dot(h, wpg_ref[...])) * _dot(p_ref[...], wpp_ref[...])
    o_ref[...] = _layer_norm(alpha * h + ple, ln_ref[4:5, :], ln_ref[5:6, :])


def _post(att, rw, x2d, p2d, wts, alpha):
    m = x2d.shape[0]
    tm = min(512, m)
    assert m % tm == 0
    row = lambda w: pl.BlockSpec((tm, w), lambda i: (i, 0))
    return pl.pallas_call(
        functools.partial(_post_kernel, alpha=alpha),
        out_shape=jax.ShapeDtypeStruct((m, D_MODEL), F32),
        grid=(m // tm,),
        in_specs=[row(C_HEADS), row(C_HEADS), row(D_MODEL), row(PLE_DIM),
                  _const_spec((2 * C_HEADS, D_MODEL)),
                  _const_spec((D_FF // FF_CHUNK, D_MODEL, FF_CHUNK)),
                  _const_spec((D_FF // FF_CHUNK, D_MODEL, FF_CHUNK)),
                  _const_spec((D_FF, D_MODEL)),
                  _const_spec((D_MODEL, D_MODEL)), _const_spec((PLE_DIM, D_MODEL)),
                  _const_spec((8, D_MODEL))],
        out_specs=row(D_MODEL),
        compiler_params=_params("arbitrary"),
        name="post",
    )(att, rw, x2d, p2d, wts["w_out"], wts["w_gate"], wts["w_up"], wts["w_down"],
      wts["w_ple_gate"], wts["w_ple_proj"], wts["ln"])


def _attn_prompt_kernel(sl_ref, q_ref, k_ref, v_ref, o_ref, q_scr, k_scr, v_scr, a_scr, m_scr,
                        l_scr, mb_scr, *, seq):
    pair = pl.program_id(1)
    slopes = (sl_ref[2 * pair], sl_ref[2 * pair + 1])
    lane = lax.broadcasted_iota(jnp.int32, (1, PAIR), 1)
    head0 = lane < HEAD_DIM
    qi = lax.broadcasted_iota(jnp.int32, (Q_BLOCK, 1), 0)
    ki = lax.broadcasted_iota(jnp.int32, (1, 2 * Q_BLOCK), 1)

    per = seq // DIL_MAX
    for r in range(DIL_MAX):
        src = pl.ds(r, per, stride=DIL_MAX)
        dst = slice(r * per, (r + 1) * per)
        q_scr[dst, :] = q_ref[src, :] * (HEAD_DIM ** -0.5)
        k_scr[dst, :] = k_ref[src, :].astype(BF)
        v_scr[dst, :] = v_ref[src, :].astype(BF)

    def gather(ref, runs):
        return jnp.concatenate([ref[rr, :] for rr in runs], axis=0)

    for window, dil in DILATIONS:
        n_runs = DIL_MAX // dil
        w = Q_BLOCK // n_runs
        nb = seq // dil // Q_BLOCK
        n_units = nb * dil
        assert nb >= 2 and n_units * Q_BLOCK == seq and n_units % ATTN_UNROLL == 0 and w % 8 == 0
        tq = (qi & (w - 1)) * DIL_MAX + (qi >> (w.bit_length() - 1)) * dil
        tk = (ki & (2 * w - 1)) * DIL_MAX + (ki >> (w.bit_length())) * dil
        if dil == 1:
            tk = ki
        for case in range(2):
            dist = case * Q_BLOCK * dil + tq - tk
            valid = (dist >= 0) & (dist <= window)
            distf = dist.astype(F32)
            mb_scr[case] = jnp.concatenate(
                [jnp.where(valid, -slopes[e] * distf, NEG_BIG) for e in range(2)], axis=0)

        def group(g, carry, dil=dil, n_runs=n_runs, w=w, first=(window, dil) == DILATIONS[0]):
            loaded = []
            for j in range(ATTN_UNROLL):
                u = g * ATTN_UNROLL + j
                n = u >> (dil.bit_length() - 1)
                rho = u & (dil - 1)
                nprev = jnp.maximum(n - 1, 0)
                base = [(dil * jj + rho) * per for jj in range(n_runs)]
                qruns = [pl.ds(pl.multiple_of(b + w * n, 8), w) for b in base]
                q = gather(q_scr, qruns)
                qq = jnp.concatenate([jnp.where(head0, q, 0.0), jnp.where(head0, 0.0, q)], axis=0)
                if dil == 1:
                    keys = pl.ds(pl.multiple_of(nprev * Q_BLOCK, Q_BLOCK), 2 * Q_BLOCK)
                    k = k_ref[keys, :].astype(BF)
                    v = v_ref[keys, :].astype(BF)
                else:
                    kruns = [pl.ds(pl.multiple_of(b + w * nprev, 16), 2 * w) for b in base]
                    k = gather(k_scr, kruns)
                    v = gather(v_scr, kruns)
                old = None if first else (
                    jnp.concatenate([gather(m_scr.at[0], qruns), gather(m_scr.at[1], qruns)], axis=0),
                    jnp.concatenate([gather(l_scr.at[0], qruns), gather(l_scr.at[1], qruns)], axis=0),
                    gather(a_scr, qruns))
                loaded.append((qruns, jnp.minimum(n, 1), qq.astype(BF), k,
                               jnp.concatenate([v, jnp.ones_like(v)], axis=1),
                               old))
            scores = [_dot_nt(qq, k) + mb_scr[case] for (_, case, qq, k, _, _) in loaded]
            stats = []
            for (_, _, _, _, _, old), s in zip(loaded, scores):
                m_new = jnp.max(s, axis=1, keepdims=True)
                m_new = (jnp.broadcast_to(m_new, (2 * Q_BLOCK, PAIR)) if first
                         else jnp.maximum(old[0], m_new))
                p = jnp.exp(s - jnp.concatenate([m_new, m_new], axis=1)).astype(BF)
                stats.append((m_new, p, None if first else jnp.exp(old[0] - m_new)))
            pvs = [jnp.dot(p, vo, preferred_element_type=F32)
                   for (_, _, _, _, vo, _), (_, p, _) in zip(loaded, stats)]
            for (qruns, _, _, _, _, old), (m_new, _, al), pv in zip(loaded, stats, pvs):
                l_new = pv[:, PAIR:2 * PAIR]
                a_new = (pv[0:Q_BLOCK, 0:PAIR], pv[Q_BLOCK:, 0:PAIR])
                if not first:
                    l_new = al * old[1] + l_new
                    a_new = (al[0:Q_BLOCK, :] * old[2] + a_new[0], al[Q_BLOCK:, :] * old[2] + a_new[1])
                a_new = jnp.where(head0, a_new[0], a_new[1])
                for jj, rr in enumerate(qruns):
                    a_scr[rr, :] = a_new[jj * w:(jj + 1) * w, :]
                    for e in range(2):
                        m_scr[e, rr, :] = m_new[e * Q_BLOCK + jj * w:e * Q_BLOCK + (jj + 1) * w, :]
                        l_scr[e, rr, :] = l_new[e * Q_BLOCK + jj * w:e * Q_BLOCK + (jj + 1) * w, :]
            return carry

        lax.fori_loop(0, n_units // ATTN_UNROLL, group, 0)

    for r in range(DIL_MAX):
        src = slice(r * per, (r + 1) * per)
        o_ref[pl.ds(r, per, stride=DIL_MAX), :] = a_scr[src, :] / jnp.where(
            head0, l_scr[0, src, :], l_scr[1, src, :])


def _attn_prompt(z3, slopes):
    b, seq, _ = z3.shape
    col = lambda off: pl.BlockSpec((None, seq, PAIR), lambda i, p: (i, 0, off + p))
    return pl.pallas_call(
        functools.partial(_attn_prompt_kernel, seq=seq),
        out_shape=jax.ShapeDtypeStruct((b, seq, C_HEADS), F32),
        grid=(b, N_PAIRS),
        in_specs=[pl.BlockSpec(memory_space=pltpu.SMEM),
                  col(0), col(N_PAIRS), col(2 * N_PAIRS)],
        out_specs=pl.BlockSpec((None, seq, PAIR), lambda i, p: (i, 0, p)),
        scratch_shapes=[pltpu.VMEM((seq, PAIR), F32), pltpu.VMEM((seq, PAIR), BF),
                        pltpu.VMEM((seq, PAIR), BF), pltpu.VMEM((seq, PAIR), F32)]
        + [pltpu.VMEM((2, seq, PAIR), F32)] * 2
        + [pltpu.VMEM((2, 2 * Q_BLOCK, 2 * Q_BLOCK), F32)],
        compiler_params=_params("arbitrary", "arbitrary"),
        name="attn_prompt",
    )(slopes, z3, z3, z3)


def _attn_sample_kernel(sl_ref, q_ref, kn_ref, vn_ref, ck_ref, cv_ref, o_ref, ko_ref, vo_ref,
                        *, s_new, cache_len):
    lane_t = lax.broadcasted_iota(jnp.int32, (1, PAIR), 1)

    def shift_in(c_ref, new_ref, out_ref):
        rolled = pltpu.roll(c_ref[...], cache_len - s_new, 1)
        new_t = jnp.concatenate([new_ref[...], jnp.zeros((PAIR - s_new, C_HEADS), F32)], axis=0).T
        tail = jnp.where(lane_t >= PAIR - s_new, pltpu.roll(new_t, PAIR - s_new, 1),
                         rolled[:, cache_len - PAIR:cache_len])
        out_ref[:, 0:cache_len - PAIR] = rolled[:, 0:cache_len - PAIR]
        out_ref[:, cache_len - PAIR:cache_len] = tail

    shift_in(ck_ref, kn_ref, ko_ref)
    shift_in(cv_ref, vn_ref, vo_ref)

    rows = s_new * N_HEADS
    ri = lax.broadcasted_iota(jnp.int32, (rows, 1), 0)
    lane = lax.broadcasted_iota(jnp.int32, (1, C_HEADS), 1)
    own = (ri & (N_HEADS - 1)) == (lane >> 6)
    q = q_ref[...]
    qe = jnp.broadcast_to(q[:, None, :], (s_new, N_HEADS, C_HEADS)).reshape(rows, C_HEADS)
    qe = jnp.where(own, qe, 0.0)
    slope = sl_ref[:, 0:1]
    spos = ri >> 3

    def weights(dist):
        mult = jnp.zeros(dist.shape, F32)
        for window, dil in DILATIONS:
            hit = (dist >= 0) & (dist <= window) & ((dist & (dil - 1)) == 0)
            mult = mult + hit.astype(F32)
        return mult

    dist_c = cache_len + spos - lax.broadcasted_iota(jnp.int32, (1, cache_len), 1)
    dist_n = spos - lax.broadcasted_iota(jnp.int32, (1, s_new), 1)
    mult_c = weights(dist_c)
    mult_n = weights(dist_n)
    sc = _dot(qe, ck_ref[...]) * (HEAD_DIM ** -0.5) - slope * dist_c.astype(F32)
    sn = _dot_nt(qe, kn_ref[...]) * (HEAD_DIM ** -0.5) - slope * dist_n.astype(F32)
    sc = jnp.where(mult_c > 0, sc, NEG_BIG)
    sn = jnp.where(mult_n > 0, sn, NEG_BIG)
    mx = jnp.maximum(jnp.max(sc, axis=1, keepdims=True), jnp.max(sn, axis=1, keepdims=True))
    pc = mult_c * jnp.exp(sc - mx)
    pn = mult_n * jnp.exp(sn - mx)
    den = jnp.sum(pc, axis=1, keepdims=True) + jnp.sum(pn, axis=1, keepdims=True)
    num = _dot_nt(pc, cv_ref[...]) + _dot(pn, vn_ref[...])
    num = jnp.where(own, num, 0.0).reshape(s_new, N_HEADS, C_HEADS).sum(axis=1)
    den = jnp.where(own, den, 0.0).reshape(s_new, N_HEADS, C_HEADS).sum(axis=1)
    o_ref[...] = num / den


def _attn_sample(z3, cache_k, cache_v, slopes):
    b, s_new, _ = z3.shape
    cache_len = cache_k.shape[2]
    assert cache_len % PAIR == 0 and cache_len > PAIR and s_new < PAIR
    new = lambda c: pl.BlockSpec((None, s_new, C_HEADS), lambda i: (i, 0, c))
    cache = pl.BlockSpec((None, C_HEADS, cache_len), lambda i: (i, 0, 0))
    slope_rows = jnp.broadcast_to(jnp.tile(slopes, s_new)[:, None], (s_new * N_HEADS, PAIR))
    return pl.pallas_call(
        functools.partial(_attn_sample_kernel, s_new=s_new, cache_len=cache_len),
        out_shape=(jax.ShapeDtypeStruct((b, s_new, C_HEADS), F32),
                   jax.ShapeDtypeStruct(cache_k.shape, F32),
                   jax.ShapeDtypeStruct(cache_v.shape, F32)),
        grid=(b,),
        in_specs=[_const_spec((s_new * N_HEADS, PAIR)), new(0), new(1), new(2), cache, cache],
        out_specs=(new(0), cache, cache),
        compiler_params=_params("arbitrary"),
        name="attn_sample",
    )(slope_rows, z3, z3, z3, cache_k, cache_v)


def _rwkv_kernel(r_ref, k_ref, v_ref, wag_ref, pr_ref, pk_ref, pv_ref, pwag_ref,
                 sr_ref, sk_ref, sv_ref, swag_ref, m0_ref, vec_ref, muwag_ref, lora_ref, seg_ref,
                 o_ref, mout_ref, m_scr, y_scr, *, t_valid, nsb, chunk, per_chunk_state):
    tb = pl.program_id(1)
    first = tb == 0
    n_rows = nsb * SUPER
    shift = chunk.bit_length() - 1
    n_chunks = SUPER // chunk

    def pair_state(ref, p):
        zero = jnp.zeros((HEAD_DIM, HEAD_DIM), F32)
        return jnp.concatenate([jnp.concatenate([ref[2 * p], zero], axis=1),
                                jnp.concatenate([zero, ref[2 * p + 1]], axis=1)], axis=0)

    if not per_chunk_state:
        @pl.when(first)
        def _():
            for p in range(N_PAIRS):
                m_scr[p] = pair_state(m0_ref.at[0], p)

    rows = lax.broadcasted_iota(jnp.int32, (n_rows, 1), 0)

    def token_shift(cur_ref, prev_ref, carry_ref, mu):
        cur = cur_ref[...]
        if per_chunk_state:
            prev = prev_ref[...]
        else:
            last = jnp.where(first, carry_ref[...], prev_ref[7:8, :])
            prev = jnp.where(rows == 0, last, pltpu.roll(cur, 1, 0))
        return cur + (prev - cur) * mu

    vec = vec_ref[...]
    row = lambda i: vec[i:i + 1, :]
    zr = token_shift(r_ref, pr_ref, sr_ref, row(_V_MU_R))
    zk = token_shift(k_ref, pk_ref, sk_ref, row(_V_MU_K))
    zv = token_shift(v_ref, pv_ref, sv_ref, row(_V_MU_V))
    zwag = token_shift(wag_ref, pwag_ref, swag_ref, muwag_ref[...])
    wa = zwag[:, 0:D_LORA_WA]
    gi = zwag[:, D_LORA_WA:D_LORA_WA + D_G_LORA]

    wlin = row(_V_W0) + _dot(jnp.tanh(wa), lora_ref[0])
    softplus = jnp.maximum(-wlin, 0.0) + jnp.log(1.0 + jnp.exp(-jnp.abs(wlin)))
    w_log = -softplus - 0.5
    ld = -jnp.exp(w_log)
    lr = jax.nn.sigmoid(row(_V_A0) + _dot(wa, lora_ref[1]))
    gate = _dot(jax.nn.sigmoid(gi), lora_ref[2])
    seg = seg_ref[...]
    kk = zk * row(_V_KK)
    kk = kk / jnp.maximum(jnp.sqrt(_dot_exact_rhs(kk * kk, seg)), 1e-12)
    kmod = zk * (1.0 + (lr - 1.0) * row(_V_KA))
    vv = zv
    if t_valid is not None:
        live = ((rows & (chunk - 1)) if per_chunk_state else (rows + tb * n_rows)) < t_valid
        ld = jnp.where(live, ld, 0.0)
        kk = jnp.where(live, kk, 0.0)
        kmod = jnp.where(live, kmod, 0.0)
        vv = jnp.where(live, vv, 0.0)

    ti = lax.broadcasted_iota(jnp.int32, (n_rows, n_rows), 0)
    tj = lax.broadcasted_iota(jnp.int32, (n_rows, n_rows), 1)
    same_chunk = (ti >> shift) == (tj >> shift)
    tril = (same_chunk & (ti >= tj)).astype(BF)
    ones = same_chunk.astype(BF)
    cum = _dot_exact_lhs(tril, ld)
    tot = _dot_exact_lhs(ones, ld)
    dec_in = jnp.exp(cum)
    dec_ex = jnp.exp(cum - ld)
    dec_inv = jnp.exp(-cum)
    dec_end = jnp.exp(tot - cum)
    dec_tot = jnp.exp(tot)
    beta = kk * lr
    abar = -(kk * dec_ex)
    rbar = zr * dec_in
    bt = beta * dec_inv
    kt = kmod * dec_inv
    bh = beta * dec_end
    kh = kmod * dec_end

    ri = lax.broadcasted_iota(jnp.int32, (PAIR, PAIR), 0)
    ci = lax.broadcasted_iota(jnp.int32, (PAIR, PAIR), 1)
    same_blk = (ri >> shift) == (ci >> shift)
    strict = same_blk & (ri > ci)
    incl = same_blk & (ri >= ci)
    same_head = (ri >> 6) == (ci >> 6)
    eye = ri == ci
    lane = lax.broadcasted_iota(jnp.int32, (1, PAIR), 1)
    head0 = lane < HEAD_DIM
    zeros_sp = jnp.zeros((SUPER, PAIR), F32)
    zeros_pp = jnp.zeros((PAIR, PAIR), F32)

    def stack(x):
        return jnp.concatenate([jnp.where(head0, x, 0.0), jnp.where(head0, 0.0, x)], axis=0)

    def unstack(x):
        return x[0:SUPER, :] + x[SUPER:PAIR, :]

    units = [(s, p) for s in range(nsb) for p in range(N_PAIRS)]
    tile = lambda x, u: x[u[0] * SUPER:(u[0] + 1) * SUPER, u[1] * PAIR:(u[1] + 1) * PAIR]
    ab = [tile(abar, u) for u in units]
    rb = [tile(rbar, u) for u in units]
    v_p = [tile(vv, u) for u in units]
    v_s = [stack(x) for x in v_p]
    a_all = [_dot_nt(jnp.concatenate([a, a, r, r], axis=0),
                     jnp.concatenate([stack(tile(bt, u)), stack(tile(kt, u))], axis=0))
             for a, r, u in zip(ab, rb, units)]
    n_ab = [jnp.where(strict, a[0:PAIR, 0:PAIR], 0.0) for a in a_all]
    a_ak = [jnp.where(strict, a[0:PAIR, PAIR:2 * PAIR], 0.0) for a in a_all]
    a_rbk = [jnp.concatenate([jnp.where(incl, a[PAIR:2 * PAIR, 0:PAIR], 0.0),
                              jnp.where(incl, a[PAIR:2 * PAIR, PAIR:2 * PAIR], 0.0)], axis=1)
             for a in a_all]
    eye_f = jnp.where(eye, 1.0, 0.0)
    tinv = [eye_f + n for n in n_ab]
    power = n_ab
    for _ in range(shift - 1):
        power = [_dot(x, x) for x in power]
        tinv = [t + _dot(x, t) for x, t in zip(power, tinv)]
    u_s = [_dot(a, v) for a, v in zip(a_ak, v_s)]
    ta = [_dot(t, jnp.concatenate([stack(a), u], axis=1))
          for t, a, u in zip(tinv, ab, u_s)]
    ry = [_dot(a, jnp.concatenate([t, jnp.concatenate([zeros_pp, v], axis=1)], axis=0))
          for a, t, v in zip(a_rbk, ta, v_s)]
    r1 = [unstack(stack(r) + y[:, 0:PAIR]) for r, y in zip(rb, ry)]
    y0 = [unstack(y[:, PAIR:2 * PAIR]) for y in ry]
    bk_t = [jnp.concatenate([tile(bh, u), tile(kh, u)], axis=0).astype(BF).T
            for u in units]
    rhs3 = [jnp.concatenate([jnp.concatenate([unstack(t[:, 0:PAIR]), unstack(t[:, PAIR:2 * PAIR])],
                                             axis=1),
                             jnp.concatenate([zeros_sp, v], axis=1)], axis=0)
            for t, v in zip(ta, v_p)]
    in_chunk = [((lane & (SUPER - 1)) >> shift) == c for c in range(n_chunks)]
    gh = [_dot(jnp.concatenate([jnp.where(in_chunk[c], b, jnp.zeros_like(b))
                                for c in range(n_chunks)], axis=0), r)
          for b, r in zip(bk_t, rhs3)]

    m = None if per_chunk_state else [m_scr[p] for p in range(N_PAIRS)]
    for s in range(nsb):
        for c in range(n_chunks):
            tok = slice(c * chunk, (c + 1) * chunk)
            r0 = s * SUPER + c * chunk
            seq_i = s * n_chunks + c
            for p in range(N_PAIRS):
                u = s * N_PAIRS + p
                sl = slice(p * PAIR, (p + 1) * PAIR)
                gh_c = gh[u][c * PAIR:(c + 1) * PAIR, :]
                g_c = (jnp.where(eye, dec_tot[r0:r0 + 1, sl], 0.0)
                       + jnp.where(same_head, gh_c[:, 0:PAIR], 0.0))
                h_c = jnp.where(same_head, gh_c[:, PAIR:2 * PAIR], 0.0)
                m_in = pair_state(m0_ref.at[seq_i], p) if per_chunk_state else m[p]
                y_scr[r0:r0 + chunk, sl] = _dot(r1[u][tok, :], m_in) + y0[u][tok, :]
                m_out = _dot(g_c, m_in) + h_c
                if per_chunk_state:
                    mout_ref[seq_i, 2 * p] = m_out[0:HEAD_DIM, 0:HEAD_DIM]
                    mout_ref[seq_i, 2 * p + 1] = m_out[HEAD_DIM:PAIR, HEAD_DIM:PAIR]
                else:
                    m[p] = m_out
    if not per_chunk_state:
        for p in range(N_PAIRS):
            m_scr[p] = m[p]

    y = y_scr[...]
    mean = _dot_exact_rhs(y, seg) * (1.0 / HEAD_DIM)
    yc = y - mean
    var = _dot_exact_rhs(yc * yc, seg) * (1.0 / HEAD_DIM)
    yn = yc * lax.rsqrt(var + GN_EPS) * row(_V_LNW) + row(_V_LNB)
    bonus = _dot_exact_rhs(zr * kmod * row(_V_RK), seg) * zv
    o_ref[...] = (yn + bonus) * gate

    if not per_chunk_state:
        @pl.when(tb == pl.num_programs(1) - 1)
        def _():
            for p in range(N_PAIRS):
                mout_ref[0, 2 * p] = m_scr[p, 0:HEAD_DIM, 0:HEAD_DIM]
                mout_ref[0, 2 * p + 1] = m_scr[p, HEAD_DIM:PAIR, HEAD_DIM:PAIR]


def _rwkv(z3, prev3, shift_prev, m0, wts, t_valid, nsb, chunk, per_chunk_state):
    b, seq, _ = z3.shape
    n_rows = nsb * SUPER
    assert seq % n_rows == 0 and SUPER % chunk == 0 and chunk % 8 == 0
    nt = seq // n_rows
    n_state = n_rows // chunk if per_chunk_state else 1
    assert m0.shape[0] == b * n_state and (nt == 1 or not per_chunk_state)
    cur = lambda w, c: pl.BlockSpec((None, n_rows, w), lambda i, t: (i, t, c))
    if per_chunk_state:
        prev = cur
    else:
        prev = lambda w, c: pl.BlockSpec(
            (None, 8, w), lambda i, t: (i, jnp.maximum(t * (n_rows // 8) - 1, 0), c))
    carry = lambda w: pl.BlockSpec((None, 1, w), lambda i, t: (i, 0, 0))
    state = pl.BlockSpec((n_state, N_HEADS, HEAD_DIM, HEAD_DIM), lambda i, t: (i, 0, 0, 0))
    wag_w = D_LORA_WA + D_G_LORA
    sp = shift_prev[:, None, :]
    all_valid = t_valid == (chunk if per_chunk_state else seq)
    return pl.pallas_call(
        functools.partial(_rwkv_kernel, t_valid=None if all_valid else t_valid, nsb=nsb,
                          chunk=chunk, per_chunk_state=per_chunk_state),
        out_shape=(jax.ShapeDtypeStruct((b, seq, C_HEADS), F32),
                   jax.ShapeDtypeStruct(m0.shape, F32)),
        grid=(b, nt),
        in_specs=[cur(C_HEADS, 3), cur(C_HEADS, 4), cur(C_HEADS, 5), cur(wag_w, 12),
                  prev(C_HEADS, 3), prev(C_HEADS, 4), prev(C_HEADS, 5), prev(wag_w, 12),
                  carry(C_HEADS), carry(C_HEADS), carry(C_HEADS), carry(wag_w),
                  state,
                  _const_spec((_V_ROWS, C_HEADS)), _const_spec((1, wag_w)),
                  _const_spec((3, PAIR, C_HEADS)), _const_spec((C_HEADS, C_HEADS))],
        out_specs=(pl.BlockSpec((None, n_rows, C_HEADS), lambda i, t: (i, t, 0)), state),
        scratch_shapes=[pltpu.VMEM((N_PAIRS, PAIR, PAIR), F32),
                        pltpu.VMEM((n_rows, C_HEADS), F32)],
        compiler_params=_params("arbitrary", "arbitrary"),
        name="rwkv",
    )(z3, z3, z3, z3, prev3, prev3, prev3, prev3,
      sp[:, :, 0:C_HEADS], sp[:, :, C_HEADS:2 * C_HEADS], sp[:, :, 2 * C_HEADS:3 * C_HEADS],
      sp[:, :, 3 * C_HEADS:], m0, wts["vec"], wts["mu_wag"], wts["lora"], wts["seg"])


def _state_to_pairs(s):
    return jnp.swapaxes(s, -1, -2)


def _pairs_to_state(m):
    return jnp.swapaxes(m, -1, -2)


def _pack_layer(w_in, mu_shift, w0, w2, a0, a2, g2, k_k, k_a, r_k, lnx_w, lnx_b, w_out, ln1_w,
                ln1_b, w_gate, w_up, w_down, ln2_w, ln2_b, w_ple_gate, w_ple_proj, ln3_w, ln3_b):
    mu_r, mu_k, mu_v = (mu_shift[i * C_HEADS:(i + 1) * C_HEADS] for i in range(3))
    vec = jnp.stack([mu_r, mu_k, mu_v, w0, a0, k_k, k_a, r_k.reshape(-1), lnx_w, lnx_b])
    vec = jnp.concatenate([vec, jnp.zeros((_V_ROWS - vec.shape[0], C_HEADS), F32)], axis=0)
    half = D_LORA_WA // 2
    zeros = jnp.zeros((half, C_HEADS), F32)
    lora = jnp.stack([jnp.concatenate([w2, zeros], axis=0),
                      jnp.concatenate([zeros, a2], axis=0),
                      g2]).astype(BF)
    head_of = jnp.arange(C_HEADS) // HEAD_DIM
    seg = (head_of[:, None] == head_of[None, :]).astype(BF)
    ln = jnp.stack([ln1_w, ln1_b, ln2_w, ln2_b, ln3_w, ln3_b,
                    jnp.zeros_like(ln1_w), jnp.zeros_like(ln1_w)])

    def col_chunks(w, width):
        k, n = w.shape
        return jnp.swapaxes(w.astype(BF).reshape(k, n // width, width), 0, 1)

    return {
        "w_in": col_chunks(w_in, IN_CHUNK), "vec": vec, "mu_wag": mu_shift[None, 3 * C_HEADS:],
        "lora": lora, "seg": seg, "w_out": w_out.astype(BF),
        "w_gate": col_chunks(w_gate, FF_CHUNK), "w_up": col_chunks(w_up, FF_CHUNK),
        "w_down": w_down.astype(BF),
        "w_ple_gate": w_ple_gate.astype(BF), "w_ple_proj": w_ple_proj.astype(BF), "ln": ln,
    }


def _alibi_slopes():
    h = jnp.arange(1, N_HEADS + 1, dtype=F32)
    return jnp.exp2(-8.0 * h / N_HEADS)


def _layer(x, p_l, wts, alpha, shift_prev, wkv_prev, cache_k=None, cache_v=None):
    b, seq, _ = x.shape
    x2d = x.reshape(b * seq, D_MODEL)
    z = _in_proj(x2d, wts["w_in"])
    z3 = z.reshape(b, seq, D_IN)
    slopes = _alibi_slopes()
    if cache_k is None:
        att = _attn_prompt(z3, slopes)
        keep = min(WINDOW_MAX, seq)
        k_win = z3[:, seq - keep:, C_HEADS:2 * C_HEADS].reshape(b, keep, N_HEADS, HEAD_DIM)
        v_win = z3[:, seq - keep:, 2 * C_HEADS:3 * C_HEADS].reshape(b, keep, N_HEADS, HEAD_DIM)
    else:
        cache_len = cache_k.shape[1]
        to_cm = lambda c: jnp.transpose(c, (0, 2, 3, 1)).reshape(b, C_HEADS, cache_len)
        from_cm = lambda c: jnp.transpose(c.reshape(b, N_HEADS, HEAD_DIM, cache_len), (0, 3, 1, 2))
        att, k_win, v_win = _attn_sample(z3, to_cm(cache_k), to_cm(cache_v), slopes)
        k_win = from_cm(k_win)
        v_win = from_cm(v_win)
    if seq % SUPER == 0:
        nsb = RWKV_BLOCKS if seq % (RWKV_BLOCKS * SUPER) == 0 else 1
        rw, m_last = _rwkv(z3, z3, shift_prev, _state_to_pairs(wkv_prev), wts, seq, nsb, CHUNK,
                           False)
    else:
        chunk = 8
        per_block = SUPER // chunk
        assert seq <= chunk and b % per_block == 0
        first_prev = jnp.concatenate([jnp.zeros((b, 1, 3 * C_HEADS), F32), shift_prev[:, None, :]],
                                     axis=-1)
        prev = jnp.concatenate([first_prev, z3[:, :seq - 1]], axis=1)
        blocks = lambda a: jnp.pad(a, ((0, 0), (0, chunk - seq), (0, 0))).reshape(
            b // per_block, SUPER, D_IN)
        rw, m_last = _rwkv(blocks(z3), blocks(prev), jnp.zeros((b // per_block, D_B_IN), F32),
                           _state_to_pairs(wkv_prev), wts, seq, 1, chunk, True)
        rw = rw.reshape(b, chunk, C_HEADS)[:, :seq]
    y = _post(att.reshape(b * seq, C_HEADS), rw.reshape(b * seq, C_HEADS), x2d,
              p_l.reshape(b * seq, PLE_DIM), wts, alpha)
    shift_new = z3[:, seq - 1, 3 * C_HEADS:]
    return y.reshape(b, seq, D_MODEL), k_win, v_win, shift_new, _pairs_to_state(m_last)


def kernel(x_prompt, x_sample, p_prompt, p_sample, cache_k_win, cache_v_win, state_wkv, state_shift, w_in, mu_shift, w0, w2, a0, a2, g2, k_k, k_a, r_k, lnx_w, lnx_b, w_out, ln1_w, ln1_b, w_gate, w_up, w_down, ln2_w, ln2_b, w_ple_gate, w_ple_proj, ln3_w, ln3_b):
    depth = w_in.shape[0]
    alpha = float((2 * depth) ** 0.25)
    xp, xs = x_prompt, x_sample
    bp = x_prompt.shape[0]
    shift0 = jnp.zeros((bp, D_B_IN), x_prompt.dtype)
    wkv0 = jnp.zeros((bp, N_HEADS, HEAD_DIM, HEAD_DIM), state_wkv.dtype)
    outs = [[] for _ in range(8)]
    for l in range(depth):
        wts = _pack_layer(w_in[l], mu_shift[l], w0[l], w2[l], a0[l], a2[l], g2[l], k_k[l], k_a[l],
                          r_k[l], lnx_w[l], lnx_b[l], w_out[l], ln1_w[l], ln1_b[l], w_gate[l],
                          w_up[l], w_down[l], ln2_w[l], ln2_b[l], w_ple_gate[l], w_ple_proj[l],
                          ln3_w[l], ln3_b[l])
        xp, kw, vw, sh, wk = _layer(xp, p_prompt[l], wts, alpha, shift0, wkv0)
        for lst, val in zip(outs[0:4], (kw, vw, wk, sh)):
            lst.append(val)
        xs, kw, vw, sh, wk = _layer(xs, p_sample[l], wts, alpha, state_shift[l], state_wkv[l],
                                    cache_k_win[l], cache_v_win[l])
        for lst, val in zip(outs[4:8], (kw, vw, wk, sh)):
            lst.append(val)
    return (xp, xs) + tuple(jnp.stack(o) for o in outs)
```

```python
import functools

import jax
import jax.numpy as jnp
from jax import lax
from jax.experimental import pallas as pl
from jax.experimental.pallas import tpu as pltpu

BF = jnp.bfloat16
F32 = jnp.float32

D_MODEL = 1024
HEAD_DIM = 64
N_HEADS = 8
C_HEADS = N_HEADS * HEAD_DIM
PAIR = 2 * HEAD_DIM
N_PAIRS = N_HEADS // 2
DILATIONS = ((128, 1), (512, 4), (2048, 16))
WINDOW_MAX = 2048
Q_BLOCK = 128
DIL_MAX = max(d for _, d in DILATIONS)
D_LORA_WA = 128
D_G_LORA = 128
D_B_IN = 3 * C_HEADS + D_LORA_WA + D_G_LORA
D_IN = 3 * C_HEADS + D_B_IN
D_FF = 2816
PLE_DIM = 256
LN_EPS = 1e-5
GN_EPS = 64e-5
NEG_BIG = -1e30

CHUNK = 16
SUPER = 64
N_CHUNKS = SUPER // CHUNK
RWKV_BLOCKS = 8
RWKV_GROUP = 2
FF_CHUNK = 256
SEG_W = 256
ATTN_UNROLL = 4
IN_CHUNK = 256
VMEM_LIMIT_BYTES = 56 * 1024 * 1024

(_V_MU_R, _V_MU_K, _V_MU_V, _V_W0, _V_A0, _V_KK, _V_KA, _V_RK, _V_LNW, _V_LNB) = range(10)
_V_ROWS = 16


def _params(*sem):
    return pltpu.CompilerParams(dimension_semantics=sem, vmem_limit_bytes=VMEM_LIMIT_BYTES)


def _const_spec(shape):
    nd = len(shape)
    return pl.BlockSpec(shape, lambda *_: (0,) * nd, pipeline_mode=pl.Buffered(1))


def _dot(a, b):
    return jnp.dot(a.astype(BF), b.astype(BF), preferred_element_type=F32)


def _dot_nt(a, b):
    return lax.dot_general(a.astype(BF), b.astype(BF), (((1,), (1,)), ((), ())),
                           preferred_element_type=F32)


def _split(x):
    hi = x.astype(BF)
    lo = (x - hi.astype(F32)).astype(BF)
    return hi, lo


def _dot_exact_lhs(mask_bf, x):
    hi, lo = _split(x)
    return (jnp.dot(mask_bf, hi, preferred_element_type=F32)
            + jnp.dot(mask_bf, lo, preferred_element_type=F32))


def _dot_exact_rhs(x, mask_bf):
    hi, lo = _split(x)
    return (jnp.dot(hi, mask_bf, preferred_element_type=F32)
            + jnp.dot(lo, mask_bf, preferred_element_type=F32))


def _layer_norm(x, w, b):
    mu = jnp.mean(x, axis=-1, keepdims=True)
    xc = x - mu
    var = jnp.mean(xc * xc, axis=-1, keepdims=True)
    return xc * lax.rsqrt(var + LN_EPS) * w + b


def _in_proj_kernel(x_ref, w_ref, o_ref):
    xb = x_ref[...].astype(BF)
    for c in range(D_IN // IN_CHUNK):
        o_ref[:, c * IN_CHUNK:(c + 1) * IN_CHUNK] = jnp.dot(xb, w_ref[c],
                                                            preferred_element_type=F32)


def _in_proj(x2d, w_bf):
    m = x2d.shape[0]
    tm = min(512, m)
    assert m % tm == 0
    return pl.pallas_call(
        _in_proj_kernel,
        out_shape=jax.ShapeDtypeStruct((m, D_IN), F32),
        grid=(m // tm,),
        in_specs=[pl.BlockSpec((tm, D_MODEL), lambda i: (i, 0)),
                  _const_spec((D_IN // IN_CHUNK, D_MODEL, IN_CHUNK))],
        out_specs=pl.BlockSpec((tm, D_IN), lambda i: (i, 0)),
        compiler_params=_params("arbitrary"),
        name="in_proj",
    )(x2d, w_bf)


def _post_kernel(att_ref, rw_ref, x_ref, p_ref, wo_ref, wg_ref, wu_ref, wd_ref, wpg_ref,
                 wpp_ref, ln_ref, o_ref, *, alpha):
    mix = (_dot(att_ref[...], wo_ref[0:C_HEADS, :])
           + _dot(rw_ref[...], wo_ref[C_HEADS:2 * C_HEADS, :]))
    h = _layer_norm(alpha * x_ref[...] + mix, ln_ref[0:1, :], ln_ref[1:2, :])
    hb = h.astype(BF)
    ffn = jnp.zeros_like(h)
    for c in range(D_FF // FF_CHUNK):
        g = jnp.dot(hb, wg_ref[c], preferred_element_type=F32)
        u = jnp.dot(hb, wu_ref[c], preferred_element_type=F32)
        act = g * jax.nn.sigmoid(g) * u
        ffn = ffn + jnp.dot(act.astype(BF), wd_ref[c * FF_CHUNK:(c + 1) * FF_CHUNK, :],
                            preferred_element_type=F32)
    h = _layer_norm(alpha * h + ffn, ln_ref[2:3, :], ln_ref[3:4, :])
    ple = jax.nn.sigmoid(_dot(h, wpg_ref[...])) * _dot(p_ref[...], wpp_ref[...])
    o_ref[...] = _layer_norm(alpha * h + ple, ln_ref[4:5, :], ln_ref[5:6, :])


def _post(att, rw, x2d, p2d, wts, alpha):
    m = x2d.shape[0]
    tm = min(512, m)
    assert m % tm == 0
    row = lambda w: pl.BlockSpec((tm, w), lambda i: (i, 0))
    return pl.pallas_call(
        functools.partial(_post_kernel, alpha=alpha),
        out_shape=jax.ShapeDtypeStruct((m, D_MODEL), F32),
        grid=(m // tm,),
        in_specs=[row(C_HEADS), row(C_HEADS), row(D_MODEL), row(PLE_DIM),
                  _const_spec((2 * C_HEADS, D_MODEL)),
                  _const_spec((D_FF // FF_CHUNK, D_MODEL, FF_CHUNK)),
                  _const_spec((D_FF // FF_CHUNK, D_MODEL, FF_CHUNK)),
                  _const_spec((D_FF, D_MODEL)),
                  _const_spec((D_MODEL, D_MODEL)), _const_spec((PLE_DIM, D_MODEL)),
                  _const_spec((8, D_MODEL))],
        out_specs=row(D_MODEL),
        compiler_params=_params("arbitrary"),
        name="post",
    )(att, rw, x2d, p2d, wts["w_out"], wts["w_gate"], wts["w_up"], wts["w_down"],
      wts["w_ple_gate"], wts["w_ple_proj"], wts["ln"])


def _attn_prompt_kernel(sl_ref, q_ref, k_ref, v_ref, o_ref, q_scr, k_scr, v_scr, a_scr, m_scr,
                        l_scr, mb_scr, *, seq):
    pair = pl.program_id(1)
    slopes = (sl_ref[2 * pair], sl_ref[2 * pair + 1])
    lane = lax.broadcasted_iota(jnp.int32, (1, PAIR), 1)
    head0 = lane < HEAD_DIM
    qi = lax.broadcasted_iota(jnp.int32, (Q_BLOCK, 1), 0)
    ki = lax.broadcasted_iota(jnp.int32, (1, 2 * Q_BLOCK), 1)

    per = seq // DIL_MAX
    for r in range(DIL_MAX):
        src = pl.ds(r, per, stride=DIL_MAX)
        dst = slice(r * per, (r + 1) * per)
        q_scr[dst, :] = q_ref[src, :] * (HEAD_DIM ** -0.5)
        k_scr[dst, :] = k_ref[src, :].astype(BF)
        v_scr[dst, :] = v_ref[src, :].astype(BF)

    def gather(ref, runs):
        return jnp.concatenate([ref[rr, :] for rr in runs], axis=0)

    for window, dil in DILATIONS:
        n_runs = DIL_MAX // dil
        w = Q_BLOCK // n_runs
        nb = seq // dil // Q_BLOCK
        n_units = nb * dil
        assert nb >= 2 and n_units * Q_BLOCK == seq and n_units % ATTN_UNROLL == 0 and w % 8 == 0
        tq = (qi & (w - 1)) * DIL_MAX + (qi >> (w.bit_length() - 1)) * dil
        tk = (ki & (2 * w - 1)) * DIL_MAX + (ki >> (w.bit_length())) * dil
        if dil == 1:
            tk = ki
        for case in range(2):
            dist = case * Q_BLOCK * dil + tq - tk
            valid = (dist >= 0) & (dist <= window)
            distf = dist.astype(F32)
            mb_scr[case] = jnp.concatenate(
                [jnp.where(valid, -slopes[e] * distf, NEG_BIG) for e in range(2)], axis=0)

        def group(g, carry, dil=dil, n_runs=n_runs, w=w, first=(window, dil) == DILATIONS[0]):
            loaded = []
            for j in range(ATTN_UNROLL):
                u = g * ATTN_UNROLL + j
                n = u >> (dil.bit_length() - 1)
                rho = u & (dil - 1)
                nprev = jnp.maximum(n - 1, 0)
                base = [(dil * jj + rho) * per for jj in range(n_runs)]
                qruns = [pl.ds(pl.multiple_of(b + w * n, 8), w) for b in base]
                q = gather(q_scr, qruns)
                qq = jnp.concatenate([jnp.where(head0, q, 0.0), jnp.where(head0, 0.0, q)], axis=0)
                if dil == 1:
                    keys = pl.ds(pl.multiple_of(nprev * Q_BLOCK, Q_BLOCK), 2 * Q_BLOCK)
                    k = k_ref[keys, :].astype(BF)
                    v = v_ref[keys, :].astype(BF)
                else:
                    kruns = [pl.ds(pl.multiple_of(b + w * nprev, 16), 2 * w) for b in base]
                    k = gather(k_scr, kruns)
                    v = gather(v_scr, kruns)
                old = None if first else (
                    jnp.concatenate([gather(m_scr.at[0], qruns), gather(m_scr.at[1], qruns)], axis=0),
                    jnp.concatenate([gather(l_scr.at[0], qruns), gather(l_scr.at[1], qruns)], axis=0),
                    gather(a_scr, qruns))
                loaded.append((qruns, jnp.minimum(n, 1), qq.astype(BF), k,
                               jnp.concatenate([v, jnp.ones_like(v)], axis=1),
                               old))
            scores = [_dot_nt(qq, k) + mb_scr[case] for (_, case, qq, k, _, _) in loaded]
            stats = []
            for (_, _, _, _, _, old), s in zip(loaded, scores):
                m_new = jnp.max(s, axis=1, keepdims=True)
                m_new = (jnp.broadcast_to(m_new, (2 * Q_BLOCK, PAIR)) if first
                         else jnp.maximum(old[0], m_new))
                p = jnp.exp(s - jnp.concatenate([m_new, m_new], axis=1)).astype(BF)
                stats.append((m_new, p, None if first else jnp.exp(old[0] - m_new)))
            pvs = [jnp.dot(p, vo, preferred_element_type=F32)
                   for (_, _, _, _, vo, _), (_, p, _) in zip(loaded, stats)]
            for (qruns, _, _, _, _, old), (m_new, _, al), pv in zip(loaded, stats, pvs):
                l_new = pv[:, PAIR:2 * PAIR]
                a_new = (pv[0:Q_BLOCK, 0:PAIR], pv[Q_BLOCK:, 0:PAIR])
                if not first:
                    l_new = al * old[1] + l_new
                    a_new = (al[0:Q_BLOCK, :] * old[2] + a_new[0], al[Q_BLOCK:, :] * old[2] + a_new[1])
                a_new = jnp.where(head0, a_new[0], a_new[1])
                for jj, rr in enumerate(qruns):
                    a_scr[rr, :] = a_new[jj * w:(jj + 1) * w, :]
                    for e in range(2):
                        m_scr[e, rr, :] = m_new[e * Q_BLOCK + jj * w:e * Q_BLOCK + (jj + 1) * w, :]
                        l_scr[e, rr, :] = l_new[e * Q_BLOCK + jj * w:e * Q_BLOCK + (jj + 1) * w, :]
            return carry

        lax.fori_loop(0, n_units // ATTN_UNROLL, group, 0)

    for r in range(DIL_MAX):
        src = slice(r * per, (r + 1) * per)
        o_ref[pl.ds(r, per, stride=DIL_MAX), :] = a_scr[src, :] / jnp.where(
            head0, l_scr[0, src, :], l_scr[1, src, :])


def _attn_prompt(z3, slopes):
    b, seq, _ = z3.shape
    col = lambda off: pl.BlockSpec((None, seq, PAIR), lambda i, p: (i, 0, off + p))
    return pl.pallas_call(
        functools.partial(_attn_prompt_kernel, seq=seq),
        out_shape=jax.ShapeDtypeStruct((b, seq, C_HEADS), F32),
        grid=(b, N_PAIRS),
        in_specs=[pl.BlockSpec(memory_space=pltpu.SMEM),
                  col(0), col(N_PAIRS), col(2 * N_PAIRS)],
        out_specs=pl.BlockSpec((None, seq, PAIR), lambda i, p: (i, 0, p)),
        scratch_shapes=[pltpu.VMEM((seq, PAIR), F32), pltpu.VMEM((seq, PAIR), BF),
                        pltpu.VMEM((seq, PAIR), BF), pltpu.VMEM((seq, PAIR), F32)]
        + [pltpu.VMEM((2, seq, PAIR), F32)] * 2
        + [pltpu.VMEM((2, 2 * Q_BLOCK, 2 * Q_BLOCK), F32)],
        compiler_params=_params("arbitrary", "arbitrary"),
        name="attn_prompt",
    )(slopes, z3, z3, z3)


def _attn_sample_kernel(sl_ref, q_ref, kn_ref, vn_ref, ck_ref, cv_ref, o_ref, ko_ref, vo_ref,
                        *, s_new, cache_len):
    lane_t = lax.broadcasted_iota(jnp.int32, (1, PAIR), 1)

    def shift_in(c_ref, new_ref, out_ref):
        rolled = pltpu.roll(c_ref[...], cache_len - s_new, 1)
        new_t = jnp.concatenate([new_ref[...], jnp.zeros((PAIR - s_new, C_HEADS), F32)], axis=0).T
        tail = jnp.where(lane_t >= PAIR - s_new, pltpu.roll(new_t, PAIR - s_new, 1),
                         rolled[:, cache_len - PAIR:cache_len])
        out_ref[:, 0:cache_len - PAIR] = rolled[:, 0:cache_len - PAIR]
        out_ref[:, cache_len - PAIR:cache_len] = tail

    shift_in(ck_ref, kn_ref, ko_ref)
    shift_in(cv_ref, vn_ref, vo_ref)

    rows = s_new * N_HEADS
    ri = lax.broadcasted_iota(jnp.int32, (rows, 1), 0)
    lane = lax.broadcasted_iota(jnp.int32, (1, C_HEADS), 1)
    own = (ri & (N_HEADS - 1)) == (lane >> 6)
    q = q_ref[...]
    qe = jnp.broadcast_to(q[:, None, :], (s_new, N_HEADS, C_HEADS)).reshape(rows, C_HEADS)
    qe = jnp.where(own, qe, 0.0)
    slope = sl_ref[:, 0:1]
    spos = ri >> 3

    def weights(dist):
        mult = jnp.zeros(dist.shape, F32)
        for window, dil in DILATIONS:
            hit = (dist >= 0) & (dist <= window) & ((dist & (dil - 1)) == 0)
            mult = mult + hit.astype(F32)
        return mult

    dist_c = cache_len + spos - lax.broadcasted_iota(jnp.int32, (1, cache_len), 1)
    dist_n = spos - lax.broadcasted_iota(jnp.int32, (1, s_new), 1)
    mult_c = weights(dist_c)
    mult_n = weights(dist_n)
    sc = _dot(qe, ck_ref[...]) * (HEAD_DIM ** -0.5) - slope * dist_c.astype(F32)
    sn = _dot_nt(qe, kn_ref[...]) * (HEAD_DIM ** -0.5) - slope * dist_n.astype(F32)
    sc = jnp.where(mult_c > 0, sc, NEG_BIG)
    sn = jnp.where(mult_n > 0, sn, NEG_BIG)
    mx = jnp.maximum(jnp.max(sc, axis=1, keepdims=True), jnp.max(sn, axis=1, keepdims=True))
    pc = mult_c * jnp.exp(sc - mx)
    pn = mult_n * jnp.exp(sn - mx)
    den = jnp.sum(pc, axis=1, keepdims=True) + jnp.sum(pn, axis=1, keepdims=True)
    num = _dot_nt(pc, cv_ref[...]) + _dot(pn, vn_ref[...])
    num = jnp.where(own, num, 0.0).reshape(s_new, N_HEADS, C_HEADS).sum(axis=1)
    den = jnp.where(own, den, 0.0).reshape(s_new, N_HEADS, C_HEADS).sum(axis=1)
    o_ref[...] = num / den


def _attn_sample(z3, cache_k, cache_v, slopes):
    b, s_new, _ = z3.shape
    cache_len = cache_k.shape[2]
    assert cache_len % PAIR == 0 and cache_len > PAIR and s_new < PAIR
    new = lambda c: pl.BlockSpec((None, s_new, C_HEADS), lambda i: (i, 0, c))
    cache = pl.BlockSpec((None, C_HEADS, cache_len), lambda i: (i, 0, 0))
    slope_rows = jnp.broadcast_to(jnp.tile(slopes, s_new)[:, None], (s_new * N_HEADS, PAIR))
    return pl.pallas_call(
        functools.partial(_attn_sample_kernel, s_new=s_new, cache_len=cache_len),
        out_shape=(jax.ShapeDtypeStruct((b, s_new, C_HEADS), F32),
                   jax.ShapeDtypeStruct(cache_k.shape, F32),
                   jax.ShapeDtypeStruct(cache_v.shape, F32)),
        grid=(b,),
        in_specs=[_const_spec((s_new * N_HEADS, PAIR)), new(0), new(1), new(2), cache, cache],
        out_specs=(new(0), cache, cache),
        compiler_params=_params("arbitrary"),
        name="attn_sample",
    )(slope_rows, z3, z3, z3, cache_k, cache_v)


def _rwkv_kernel(r_ref, k_ref, v_ref, wag_ref, pr_ref, pk_ref, pv_ref, pwag_ref,
                 sr_ref, sk_ref, sv_ref, swag_ref, m0_ref, vec_ref, muwag_ref, lora_ref, seg_ref,
                 o_ref, mout_ref, m_scr, y_scr, *, t_valid, nsb, chunk, per_chunk_state):
    tb = pl.program_id(1)
    first = tb == 0
    n_rows = nsb * SUPER
    shift = chunk.bit_length() - 1
    n_chunks = SUPER // chunk

    def pair_state(ref, p):
        zero = jnp.zeros((HEAD_DIM, HEAD_DIM), F32)
        return jnp.concatenate([jnp.concatenate([ref[2 * p], zero], axis=1),
                                jnp.concatenate([zero, ref[2 * p + 1]], axis=1)], axis=0)

    if not per_chunk_state:
        @pl.when(first)
        def _():
            for p in range(N_PAIRS):
                m_scr[p] = pair_state(m0_ref.at[0], p)

    rows = lax.broadcasted_iota(jnp.int32, (n_rows, 1), 0)

    def token_shift(cur_ref, prev_ref, carry_ref, mu):
        cur = cur_ref[...]
        if per_chunk_state:
            prev = prev_ref[...]
        else:
            last = jnp.where(first, carry_ref[...], prev_ref[7:8, :])
            prev = jnp.where(rows == 0, last, pltpu.roll(cur, 1, 0))
        return cur + (prev - cur) * mu

    vec = vec_ref[...]
    row = lambda i: vec[i:i + 1, :]
    zr = token_shift(r_ref, pr_ref, sr_ref, row(_V_MU_R))
    zk = token_shift(k_ref, pk_ref, sk_ref, row(_V_MU_K))
    zv = token_shift(v_ref, pv_ref, sv_ref, row(_V_MU_V))
    zwag = token_shift(wag_ref, pwag_ref, swag_ref, muwag_ref[...])
    wa = zwag[:, 0:D_LORA_WA]
    gi = zwag[:, D_LORA_WA:D_LORA_WA + D_G_LORA]

    wlin = row(_V_W0) + _dot(jnp.tanh(wa), lora_ref[0])
    softplus = jnp.maximum(-wlin, 0.0) + jnp.log(1.0 + jnp.exp(-jnp.abs(wlin)))
    w_log = -softplus - 0.5
    ld = -jnp.exp(w_log)
    lr = jax.nn.sigmoid(row(_V_A0) + _dot(wa, lora_ref[1]))
    gate = _dot(jax.nn.sigmoid(gi), lora_ref[2])
    seg = seg_ref[...]

    def head_sum(x):
        return jnp.concatenate([_dot_exact_rhs(x[:, i:i + SEG_W], seg)
                                for i in range(0, C_HEADS, SEG_W)], axis=1)

    kk = zk * row(_V_KK)
    kk = kk / jnp.maximum(jnp.sqrt(head_sum(kk * kk)), 1e-12)
    kmod = zk * (1.0 + (lr - 1.0) * row(_V_KA))
    vv = zv
    if t_valid is not None:
        live = ((rows & (chunk - 1)) if per_chunk_state else (rows + tb * n_rows)) < t_valid
        ld = jnp.where(live, ld, 0.0)
        kk = jnp.where(live, kk, 0.0)
        kmod = jnp.where(live, kmod, 0.0)
        vv = jnp.where(live, vv, 0.0)

    span = min(PAIR, n_rows)
    ti = lax.broadcasted_iota(jnp.int32, (span, span), 0)
    tj = lax.broadcasted_iota(jnp.int32, (span, span), 1)
    same_chunk = (ti >> shift) == (tj >> shift)
    sums = jnp.concatenate([same_chunk & (ti >= tj), same_chunk], axis=0).astype(BF)
    cum_tot = [_dot_exact_lhs(sums, ld[i:i + span, :]) for i in range(0, n_rows, span)]
    cum = jnp.concatenate([x[0:span, :] for x in cum_tot], axis=0)
    tot = jnp.concatenate([x[span:2 * span, :] for x in cum_tot], axis=0)
    dec_in = jnp.exp(cum)
    dec_ex = jnp.exp(cum - ld)
    dec_inv = jnp.exp(-cum)
    dec_end = jnp.exp(tot - cum)
    dec_tot = jnp.exp(tot)
    beta = kk * lr
    abar = -(kk * dec_ex)
    rbar = zr * dec_in
    bt = beta * dec_inv
    kt = kmod * dec_inv
    bh = beta * dec_end
    kh = kmod * dec_end

    ri = lax.broadcasted_iota(jnp.int32, (PAIR, PAIR), 0)
    ci = lax.broadcasted_iota(jnp.int32, (PAIR, PAIR), 1)
    same_blk = (ri >> shift) == (ci >> shift)
    strict = same_blk & (ri > ci)
    incl = same_blk & (ri >= ci)
    same_head = (ri >> 6) == (ci >> 6)
    eye = ri == ci
    lane = lax.broadcasted_iota(jnp.int32, (1, PAIR), 1)
    head0 = lane < HEAD_DIM
    zeros_sp = jnp.zeros((SUPER, PAIR), F32)
    zeros_pp = jnp.zeros((PAIR, PAIR), F32)

    def stack(x):
        return jnp.concatenate([jnp.where(head0, x, 0.0), jnp.where(head0, 0.0, x)], axis=0)

    def unstack(x):
        return x[0:SUPER, :] + x[SUPER:PAIR, :]

    def both_heads(x, keep):
        return jnp.where(keep, jnp.concatenate([x, x], axis=0), 0.0)

    eye_f = jnp.where(eye, 1.0, 0.0)
    in_chunk = [((lane & (SUPER - 1)) >> shift) == c for c in range(n_chunks)]

    def phase1(blocks, out):
        units = [(s, p) for s in blocks for p in range(N_PAIRS)]
        tile = lambda x, u: x[u[0] * SUPER:(u[0] + 1) * SUPER, u[1] * PAIR:(u[1] + 1) * PAIR]
        ab = [tile(abar, u) for u in units]
        rb = [tile(rbar, u) for u in units]
        v_p = [tile(vv, u) for u in units]
        v_s = [stack(x) for x in v_p]
        a_all = [_dot_nt(jnp.concatenate([a, r], axis=0),
                         jnp.concatenate([stack(tile(bt, u)), stack(tile(kt, u))], axis=0))
                 for a, r, u in zip(ab, rb, units)]
        yield
        n_ab = [both_heads(a[0:SUPER, 0:PAIR], strict) for a in a_all]
        a_ak = [both_heads(a[0:SUPER, PAIR:2 * PAIR], strict) for a in a_all]
        a_rbk = [jnp.concatenate([both_heads(a[SUPER:PAIR, 0:PAIR], incl),
                                  both_heads(a[SUPER:PAIR, PAIR:2 * PAIR], incl)], axis=1)
                 for a in a_all]
        tinv = [eye_f + n for n in n_ab]
        power = n_ab
        for _ in range(shift - 1):
            power = [_dot(x, x) for x in power]
            yield
            tinv = [t + _dot(x, t) for x, t in zip(power, tinv)]
            yield
        u_s = [_dot(a, v) for a, v in zip(a_ak, v_s)]
        yield
        ta = [_dot(t, jnp.concatenate([stack(a), u], axis=1))
              for t, a, u in zip(tinv, ab, u_s)]
        yield
        ry = [_dot(a, jnp.concatenate([t, jnp.concatenate([zeros_pp, v], axis=1)], axis=0))
              for a, t, v in zip(a_rbk, ta, v_s)]
        yield
        r1 = [unstack(stack(r) + y[:, 0:PAIR]) for r, y in zip(rb, ry)]
        y0 = [unstack(y[:, PAIR:2 * PAIR]) for y in ry]
        bk_t = [jnp.concatenate([tile(bh, u), tile(kh, u)], axis=0).astype(BF).T
                for u in units]
        rhs3 = [jnp.concatenate([jnp.concatenate([unstack(t[:, 0:PAIR]),
                                                  unstack(t[:, PAIR:2 * PAIR])], axis=1),
                                 jnp.concatenate([zeros_sp, v], axis=1)], axis=0)
                for t, v in zip(ta, v_p)]
        gh = [_dot(jnp.concatenate([jnp.where(in_chunk[c], b, jnp.zeros_like(b))
                                    for c in range(n_chunks)], axis=0), r)
              for b, r in zip(bk_t, rhs3)]
        for i, u in enumerate(units):
            out[u] = (r1[i], y0[i], gh[i])
        yield

    m = None if per_chunk_state else [m_scr[p] for p in range(N_PAIRS)]

    def phase2(blocks, fac):
        for s in blocks:
            for c in range(n_chunks):
                tok = slice(c * chunk, (c + 1) * chunk)
                r0 = s * SUPER + c * chunk
                seq_i = s * n_chunks + c
                for p in range(N_PAIRS):
                    r1, y0, gh = fac[(s, p)]
                    sl = slice(p * PAIR, (p + 1) * PAIR)
                    gh_c = gh[c * PAIR:(c + 1) * PAIR, :]
                    g_c = (jnp.where(eye, dec_tot[r0:r0 + 1, sl], 0.0)
                           + jnp.where(same_head, gh_c[:, 0:PAIR], 0.0))
                    h_c = jnp.where(same_head, gh_c[:, PAIR:2 * PAIR], 0.0)
                    m_in = pair_state(m0_ref.at[seq_i], p) if per_chunk_state else m[p]
                    y_scr[r0:r0 + chunk, sl] = _dot(r1[tok, :], m_in) + y0[tok, :]
                    m_out = _dot(g_c, m_in) + h_c
                    if per_chunk_state:
                        mout_ref[seq_i, 2 * p] = m_out[0:HEAD_DIM, 0:HEAD_DIM]
                        mout_ref[seq_i, 2 * p + 1] = m_out[HEAD_DIM:PAIR, HEAD_DIM:PAIR]
                    else:
                        m[p] = m_out
                yield

    groups = [list(range(g, min(g + RWKV_GROUP, nsb))) for g in range(0, nsb, RWKV_GROUP)]
    fac = {}
    for _ in phase1(groups[0], fac):
        pass
    for done, nxt in zip(groups, groups[1:]):
        steps = phase2(done, fac)
        for _ in phase1(nxt, fac):
            next(steps, None)
        for _ in steps:
            pass
    for _ in phase2(groups[-1], fac):
        pass
    if not per_chunk_state:
        for p in range(N_PAIRS):
            m_scr[p] = m[p]

    y = y_scr[...]
    mean = head_sum(y) * (1.0 / HEAD_DIM)
    yc = y - mean
    var = head_sum(yc * yc) * (1.0 / HEAD_DIM)
    yn = yc * lax.rsqrt(var + GN_EPS) * row(_V_LNW) + row(_V_LNB)
    bonus = head_sum(zr * kmod * row(_V_RK)) * zv
    o_ref[...] = (yn + bonus) * gate

    if not per_chunk_state:
        @pl.when(tb == pl.num_programs(1) - 1)
        def _():
            for p in range(N_PAIRS):
                mout_ref[0, 2 * p] = m_scr[p, 0:HEAD_DIM, 0:HEAD_DIM]
                mout_ref[0, 2 * p + 1] = m_scr[p, HEAD_DIM:PAIR, HEAD_DIM:PAIR]


def _rwkv(z3, prev3, shift_prev, m0, wts, t_valid, nsb, chunk, per_chunk_state):
    b, seq, _ = z3.shape
    n_rows = nsb * SUPER
    assert seq % n_rows == 0 and SUPER % chunk == 0 and chunk % 8 == 0
    nt = seq // n_rows
    n_state = n_rows // chunk if per_chunk_state else 1
    assert m0.shape[0] == b * n_state and (nt == 1 or not per_chunk_state)
    cur = lambda w, c: pl.BlockSpec((None, n_rows, w), lambda i, t: (i, t, c))
    if per_chunk_state:
        prev = cur
    else:
        prev = lambda w, c: pl.BlockSpec(
            (None, 8, w), lambda i, t: (i, jnp.maximum(t * (n_rows // 8) - 1, 0), c))
    carry = lambda w: pl.BlockSpec((None, 1, w), lambda i, t: (i, 0, 0))
    state = pl.BlockSpec((n_state, N_HEADS, HEAD_DIM, HEAD_DIM), lambda i, t: (i, 0, 0, 0))
    wag_w = D_LORA_WA + D_G_LORA
    sp = shift_prev[:, None, :]
    all_valid = t_valid == (chunk if per_chunk_state else seq)
    return pl.pallas_call(
        functools.partial(_rwkv_kernel, t_valid=None if all_valid else t_valid, nsb=nsb,
                          chunk=chunk, per_chunk_state=per_chunk_state),
        out_shape=(jax.ShapeDtypeStruct((b, seq, C_HEADS), F32),
                   jax.ShapeDtypeStruct(m0.shape, F32)),
        grid=(b, nt),
        in_specs=[cur(C_HEADS, 3), cur(C_HEADS, 4), cur(C_HEADS, 5), cur(wag_w, 12),
                  prev(C_HEADS, 3), prev(C_HEADS, 4), prev(C_HEADS, 5), prev(wag_w, 12),
                  carry(C_HEADS), carry(C_HEADS), carry(C_HEADS), carry(wag_w),
                  state,
                  _const_spec((_V_ROWS, C_HEADS)), _const_spec((1, wag_w)),
                  _const_spec((3, PAIR, C_HEADS)), _const_spec((SEG_W, SEG_W))],
        out_specs=(pl.BlockSpec((None, n_rows, C_HEADS), lambda i, t: (i, t, 0)), state),
        scratch_shapes=[pltpu.VMEM((N_PAIRS, PAIR, PAIR), F32),
                        pltpu.VMEM((n_rows, C_HEADS), F32)],
        compiler_params=_params("arbitrary", "arbitrary"),
        name="rwkv",
    )(z3, z3, z3, z3, prev3, prev3, prev3, prev3,
      sp[:, :, 0:C_HEADS], sp[:, :, C_HEADS:2 * C_HEADS], sp[:, :, 2 * C_HEADS:3 * C_HEADS],
      sp[:, :, 3 * C_HEADS:], m0, wts["vec"], wts["mu_wag"], wts["lora"], wts["seg"])


def _state_to_pairs(s):
    return jnp.swapaxes(s, -1, -2)


def _pairs_to_state(m):
    return jnp.swapaxes(m, -1, -2)


def _pack_layer(w_in, mu_shift, w0, w2, a0, a2, g2, k_k, k_a, r_k, lnx_w, lnx_b, w_out, ln1_w,
                ln1_b, w_gate, w_up, w_down, ln2_w, ln2_b, w_ple_gate, w_ple_proj, ln3_w, ln3_b):
    mu_r, mu_k, mu_v = (mu_shift[i * C_HEADS:(i + 1) * C_HEADS] for i in range(3))
    vec = jnp.stack([mu_r, mu_k, mu_v, w0, a0, k_k, k_a, r_k.reshape(-1), lnx_w, lnx_b])
    vec = jnp.concatenate([vec, jnp.zeros((_V_ROWS - vec.shape[0], C_HEADS), F32)], axis=0)
    half = D_LORA_WA // 2
    zeros = jnp.zeros((half, C_HEADS), F32)
    lora = jnp.stack([jnp.concatenate([w2, zeros], axis=0),
                      jnp.concatenate([zeros, a2], axis=0),
                      g2]).astype(BF)
    head_of = jnp.arange(SEG_W) // HEAD_DIM
    seg = (head_of[:, None] == head_of[None, :]).astype(BF)
    ln = jnp.stack([ln1_w, ln1_b, ln2_w, ln2_b, ln3_w, ln3_b,
                    jnp.zeros_like(ln1_w), jnp.zeros_like(ln1_w)])

    def col_chunks(w, width):
        k, n = w.shape
        return jnp.swapaxes(w.astype(BF).reshape(k, n // width, width), 0, 1)

    return {
        "w_in": col_chunks(w_in, IN_CHUNK), "vec": vec, "mu_wag": mu_shift[None, 3 * C_HEADS:],
        "lora": lora, "seg": seg, "w_out": w_out.astype(BF),
        "w_gate": col_chunks(w_gate, FF_CHUNK), "w_up": col_chunks(w_up, FF_CHUNK),
        "w_down": w_down.astype(BF),
        "w_ple_gate": w_ple_gate.astype(BF), "w_ple_proj": w_ple_proj.astype(BF), "ln": ln,
    }


def _alibi_slopes():
    h = jnp.arange(1, N_HEADS + 1, dtype=F32)
    return jnp.exp2(-8.0 * h / N_HEADS)


def _layer(x, p_l, wts, alpha, shift_prev, wkv_prev, cache_k=None, cache_v=None):
    b, seq, _ = x.shape
    x2d = x.reshape(b * seq, D_MODEL)
    z = _in_proj(x2d, wts["w_in"])
    z3 = z.reshape(b, seq, D_IN)
    slopes = _alibi_slopes()
    if cache_k is None:
        att = _attn_prompt(z3, slopes)
        keep = min(WINDOW_MAX, seq)
        k_win = z3[:, seq - keep:, C_HEADS:2 * C_HEADS].reshape(b, keep, N_HEADS, HEAD_DIM)
        v_win = z3[:, seq - keep:, 2 * C_HEADS:3 * C_HEADS].reshape(b, keep, N_HEADS, HEAD_DIM)
    else:
        cache_len = cache_k.shape[1]
        to_cm = lambda c: jnp.transpose(c, (0, 2, 3, 1)).reshape(b, C_HEADS, cache_len)
        from_cm = lambda c: jnp.transpose(c.reshape(b, N_HEADS, HEAD_DIM, cache_len), (0, 3, 1, 2))
        att, k_win, v_win = _attn_sample(z3, to_cm(cache_k), to_cm(cache_v), slopes)
        k_win = from_cm(k_win)
        v_win = from_cm(v_win)
    if seq % SUPER == 0:
        nsb = RWKV_BLOCKS if seq % (RWKV_BLOCKS * SUPER) == 0 else 1
        rw, m_last = _rwkv(z3, z3, shift_prev, _state_to_pairs(wkv_prev), wts, seq, nsb, CHUNK,
                           False)
    else:
        chunk = 8
        per_block = SUPER // chunk
        assert seq <= chunk and b % per_block == 0
        first_prev = jnp.concatenate([jnp.zeros((b, 1, 3 * C_HEADS), F32), shift_prev[:, None, :]],
                                     axis=-1)
        prev = jnp.concatenate([first_prev, z3[:, :seq - 1]], axis=1)
        blocks = lambda a: jnp.pad(a, ((0, 0), (0, chunk - seq), (0, 0))).reshape(
            b // per_block, SUPER, D_IN)
        rw, m_last = _rwkv(blocks(z3), blocks(prev), jnp.zeros((b // per_block, D_B_IN), F32),
                           _state_to_pairs(wkv_prev), wts, seq, 1, chunk, True)
        rw = rw.reshape(b, chunk, C_HEADS)[:, :seq]
    y = _post(att.reshape(b * seq, C_HEADS), rw.reshape(b * seq, C_HEADS), x2d,
              p_l.reshape(b * seq, PLE_DIM), wts, alpha)
    shift_new = z3[:, seq - 1, 3 * C_HEADS:]
    return y.reshape(b, seq, D_MODEL), k_win, v_win, shift_new, _pairs_to_state(m_last)


def kernel(x_prompt, x_sample, p_prompt, p_sample, cache_k_win, cache_v_win, state_wkv, state_shift, w_in, mu_shift, w0, w2, a0, a2, g2, k_k, k_a, r_k, lnx_w, lnx_b, w_out, ln1_w, ln1_b, w_gate, w_up, w_down, ln2_w, ln2_b, w_ple_gate, w_ple_proj, ln3_w, ln3_b):
    depth = w_in.shape[0]
    alpha = float((2 * depth) ** 0.25)
    xp, xs = x_prompt, x_sample
    bp = x_prompt.shape[0]
    shift0 = jnp.zeros((bp, D_B_IN), x_prompt.dtype)
    wkv0 = jnp.zeros((bp, N_HEADS, HEAD_DIM, HEAD_DIM), state_wkv.dtype)
    outs = [[] for _ in range(8)]
    for l in range(depth):
        wts = _pack_layer(w_in[l], mu_shift[l], w0[l], w2[l], a0[l], a2[l], g2[l], k_k[l], k_a[l],
                          r_k[l], lnx_w[l], lnx_b[l], w_out[l], ln1_w[l], ln1_b[l], w_gate[l],
                          w_up[l], w_down[l], ln2_w[l], ln2_b[l], w_ple_gate[l], w_ple_proj[l],
                          ln3_w[l], ln3_b[l])
        xp, kw, vw, sh, wk = _layer(xp, p_prompt[l], wts, alpha, shift0, wkv0)
        for lst, val in zip(outs[0:4], (kw, vw, wk, sh)):
            lst.append(val)
        xs, kw, vw, sh, wk = _layer(xs, p_sample[l], wts, alpha, state_shift[l], state_wkv[l],
                                    cache_k_win[l], cache_v_win[l])
        for lst, val in zip(outs[4:8], (kw, vw, wk, sh)):
            lst.append(val)
    return (xp, xs) + tuple(jnp.stack(o) for o in outs)
```

```python
import functools

import jax
import jax.numpy as jnp
from jax import lax
from jax.experimental import pallas as pl
from jax.experimental.pallas import tpu as pltpu

BF = jnp.bfloat16
F32 = jnp.float32

D_MODEL = 1024
HEAD_DIM = 64
N_HEADS = 8
C_HEADS = N_HEADS * HEAD_DIM
PAIR = 2 * HEAD_DIM
N_PAIRS = N_HEADS // 2
DILATIONS = ((128, 1), (512, 4), (2048, 16))
WINDOW_MAX = 2048
Q_BLOCK = 128
DIL_MAX = max(d for _, d in DILATIONS)
DIL_STEP = 4
D_LORA_WA = 128
D_G_LORA = 128
D_B_IN = 3 * C_HEADS + D_LORA_WA + D_G_LORA
D_IN = 3 * C_HEADS + D_B_IN
D_FF = 2816
PLE_DIM = 256
LN_EPS = 1e-5
GN_EPS = 64e-5
NEG_BIG = -1e30

CHUNK = 16
SUPER = 64
N_CHUNKS = SUPER // CHUNK
RWKV_BLOCKS = 8
RWKV_GROUP = 2
FF_CHUNK = 256
SEG_W = 256
ATTN_UNROLL = 4
IN_CHUNK = 256
VMEM_LIMIT_BYTES = 56 * 1024 * 1024

(_V_MU_R, _V_MU_K, _V_MU_V, _V_W0, _V_A0, _V_KK, _V_KA, _V_RK, _V_LNW, _V_LNB) = range(10)
_V_ROWS = 16


def _params(*sem):
    return pltpu.CompilerParams(dimension_semantics=sem, vmem_limit_bytes=VMEM_LIMIT_BYTES)


def _const_spec(shape):
    nd = len(shape)
    return pl.BlockSpec(shape, lambda *_: (0,) * nd, pipeline_mode=pl.Buffered(1))


def _dot(a, b):
    return jnp.dot(a.astype(BF), b.astype(BF), preferred_element_type=F32)


def _dot_nt(a, b):
    return lax.dot_general(a.astype(BF), b.astype(BF), (((1,), (1,)), ((), ())),
                           preferred_element_type=F32)


def _split(x):
    hi = x.astype(BF)
    lo = (x - hi.astype(F32)).astype(BF)
    return hi, lo


def _dot_exact_lhs(mask_bf, x):
    hi, lo = _split(x)
    return (jnp.dot(mask_bf, hi, preferred_element_type=F32)
            + jnp.dot(mask_bf, lo, preferred_element_type=F32))


def _dot_exact_rhs(x, mask_bf):
    hi, lo = _split(x)
    return (jnp.dot(hi, mask_bf, preferred_element_type=F32)
            + jnp.dot(lo, mask_bf, preferred_element_type=F32))


def _layer_norm(x, w, b):
    mu = jnp.mean(x, axis=-1, keepdims=True)
    xc = x - mu
    var = jnp.mean(xc * xc, axis=-1, keepdims=True)
    return xc * lax.rsqrt(var + LN_EPS) * w + b


def _in_proj_kernel(x_ref, w_ref, o_ref):
    xb = x_ref[...].astype(BF)
    for c in range(D_IN // IN_CHUNK):
        o_ref[:, c * IN_CHUNK:(c + 1) * IN_CHUNK] = jnp.dot(xb, w_ref[c],
                                                            preferred_element_type=F32)


def _in_proj(x2d, w_bf):
    m = x2d.shape[0]
    tm = min(512, m)
    assert m % tm == 0
    return pl.pallas_call(
        _in_proj_kernel,
        out_shape=jax.ShapeDtypeStruct((m, D_IN), F32),
        grid=(m // tm,),
        in_specs=[pl.BlockSpec((tm, D_MODEL), lambda i: (i, 0)),
                  _const_spec((D_IN // IN_CHUNK, D_MODEL, IN_CHUNK))],
        out_specs=pl.BlockSpec((tm, D_IN), lambda i: (i, 0)),
        compiler_params=_params("arbitrary"),
        name="in_proj",
    )(x2d, w_bf)


def _post_kernel(att_ref, rw_ref, x_ref, p_ref, wo_ref, wg_ref, wu_ref, wd_ref, wpg_ref,
                 wpp_ref, ln_ref, o_ref, *, alpha):
    mix = (_dot(att_ref[...], wo_ref[0:C_HEADS, :])
           + _dot(rw_ref[...], wo_ref[C_HEADS:2 * C_HEADS, :]))
    h = _layer_norm(alpha * x_ref[...] + mix, ln_ref[0:1, :], ln_ref[1:2, :])
    hb = h.astype(BF)
    ffn = jnp.zeros_like(h)
    for c in range(D_FF // FF_CHUNK):
        g = jnp.dot(hb, wg_ref[c], preferred_element_type=F32)
        u = jnp.dot(hb, wu_ref[c], preferred_element_type=F32)
        act = g * jax.nn.sigmoid(g) * u
        ffn = ffn + jnp.dot(act.astype(BF), wd_ref[c * FF_CHUNK:(c + 1) * FF_CHUNK, :],
                            preferred_element_type=F32)
    h = _layer_norm(alpha * h + ffn, ln_ref[2:3, :], ln_ref[3:4, :])
    ple = jax.nn.sigmoid(_dot(h, wpg_ref[...])) * _dot(p_ref[...], wpp_ref[...])
    o_ref[...] = _layer_norm(alpha * h + ple, ln_ref[4:5, :], ln_ref[5:6, :])


def _post(att, rw, x2d, p2d, wts, alpha):
    m = x2d.shape[0]
    tm = min(512, m)
    assert m % tm == 0
    row = lambda w: pl.BlockSpec((tm, w), lambda i: (i, 0))
    return pl.pallas_call(
        functools.partial(_post_kernel, alpha=alpha),
        out_shape=jax.ShapeDtypeStruct((m, D_MODEL), F32),
        grid=(m // tm,),
        in_specs=[row(C_HEADS), row(C_HEADS), row(D_MODEL), row(PLE_DIM),
                  _const_spec((2 * C_HEADS, D_MODEL)),
                  _const_spec((D_FF // FF_CHUNK, D_MODEL, FF_CHUNK)),
                  _const_spec((D_FF // FF_CHUNK, D_MODEL, FF_CHUNK)),
                  _const_spec((D_FF, D_MODEL)),
                  _const_spec((D_MODEL, D_MODEL)), _const_spec((PLE_DIM, D_MODEL)),
                  _const_spec((8, D_MODEL))],
        out_specs=row(D_MODEL),
        compiler_params=_params("arbitrary"),
        name="post",
    )(att, rw, x2d, p2d, wts["w_out"], wts["w_gate"], wts["w_up"], wts["w_down"],
      wts["w_ple_gate"], wts["w_ple_proj"], wts["ln"])


def _attn_prompt_kernel(sl_ref, q_ref, k_ref, v_ref, o_ref, kw_ref, vw_ref, q_scr, k_scr, v_scr,
                        a_scr, m_scr, l_scr, mb_scr, *, seq, keep):
    pair = pl.program_id(1)
    slopes = (sl_ref[2 * pair], sl_ref[2 * pair + 1])
    lane = lax.broadcasted_iota(jnp.int32, (1, PAIR), 1)
    head0 = lane < HEAD_DIM
    kw_ref[...] = k_ref[seq - keep:seq, :].T
    vw_ref[...] = v_ref[seq - keep:seq, :].T
    qi = lax.broadcasted_iota(jnp.int32, (Q_BLOCK, 1), 0)
    ki = lax.broadcasted_iota(jnp.int32, (1, 2 * Q_BLOCK), 1)

    per = seq // DIL_MAX
    quarter = seq // DIL_STEP

    def to_residue_major(src_ref, dst_ref, convert):
        for rho in range(DIL_STEP):
            a_scr[rho * quarter:(rho + 1) * quarter, :] = src_ref[
                pl.ds(rho, quarter, stride=DIL_STEP), :]
        for rho in range(DIL_STEP):
            for j in range(DIL_MAX // DIL_STEP):
                r = DIL_STEP * j + rho
                dst_ref[r * per:(r + 1) * per, :] = convert(
                    a_scr[pl.ds(rho * quarter + j, per, stride=DIL_STEP), :])

    to_residue_major(q_ref, q_scr, lambda x: x * (HEAD_DIM ** -0.5))
    to_residue_major(k_ref, k_scr, lambda x: x.astype(BF))
    to_residue_major(v_ref, v_scr, lambda x: x.astype(BF))

    def gather(ref, runs):
        return jnp.concatenate([ref[rr, :] for rr in runs], axis=0)

    for window, dil in DILATIONS:
        n_runs = DIL_MAX // dil
        w = Q_BLOCK // n_runs
        nb = seq // dil // Q_BLOCK
        n_units = nb * dil
        assert nb >= 2 and n_units * Q_BLOCK == seq and n_units % ATTN_UNROLL == 0 and w % 8 == 0
        tq = (qi & (w - 1)) * DIL_MAX + (qi >> (w.bit_length() - 1)) * dil
        tk = (ki & (2 * w - 1)) * DIL_MAX + (ki >> (w.bit_length())) * dil
        if dil == 1:
            tk = ki
        for case in range(2):
            dist = case * Q_BLOCK * dil + tq - tk
            valid = (dist >= 0) & (dist <= window)
            distf = dist.astype(F32)
            mb_scr[case] = jnp.concatenate(
                [jnp.where(valid, -slopes[e] * distf, NEG_BIG) for e in range(2)], axis=0)

        def group(g, carry, dil=dil, n_runs=n_runs, w=w, first=(window, dil) == DILATIONS[0]):
            loaded = []
            for j in range(ATTN_UNROLL):
                u = g * ATTN_UNROLL + j
                n = u >> (dil.bit_length() - 1)
                rho = u & (dil - 1)
                nprev = jnp.maximum(n - 1, 0)
                base = [(dil * jj + rho) * per for jj in range(n_runs)]
                qruns = [pl.ds(pl.multiple_of(b + w * n, 8), w) for b in base]
                q = gather(q_scr, qruns)
                qq = jnp.concatenate([jnp.where(head0, q, 0.0), jnp.where(head0, 0.0, q)], axis=0)
                if dil == 1:
                    keys = pl.ds(pl.multiple_of(nprev * Q_BLOCK, Q_BLOCK), 2 * Q_BLOCK)
                    k = k_ref[keys, :].astype(BF)
                    v = v_ref[keys, :].astype(BF)
                else:
                    kruns = [pl.ds(pl.multiple_of(b + w * nprev, 16), 2 * w) for b in base]
                    k = gather(k_scr, kruns)
                    v = gather(v_scr, kruns)
                old = None if first else (
                    jnp.concatenate([gather(m_scr.at[0], qruns), gather(m_scr.at[1], qruns)], axis=0),
                    jnp.concatenate([gather(l_scr.at[0], qruns), gather(l_scr.at[1], qruns)], axis=0),
                    gather(a_scr, qruns))
                loaded.append((qruns, jnp.minimum(n, 1), qq.astype(BF), k,
                               jnp.concatenate([v, jnp.ones_like(v)], axis=1),
                               old))
            scores = [_dot_nt(qq, k) + mb_scr[case] for (_, case, qq, k, _, _) in loaded]
            stats = []
            for (_, _, _, _, _, old), s in zip(loaded, scores):
                m_new = jnp.max(s, axis=1, keepdims=True)
                m_new = (jnp.broadcast_to(m_new, (2 * Q_BLOCK, PAIR)) if first
                         else jnp.maximum(old[0], m_new))
                p = jnp.exp(s - jnp.concatenate([m_new, m_new], axis=1)).astype(BF)
                stats.append((m_new, p, None if first else jnp.exp(old[0] - m_new)))
            pvs = [jnp.dot(p, vo, preferred_element_type=F32)
                   for (_, _, _, _, vo, _), (_, p, _) in zip(loaded, stats)]
            for (qruns, _, _, _, _, old), (m_new, _, al), pv in zip(loaded, stats, pvs):
                l_new = pv[:, PAIR:2 * PAIR]
                a_new = (pv[0:Q_BLOCK, 0:PAIR], pv[Q_BLOCK:, 0:PAIR])
                if not first:
                    l_new = al * old[1] + l_new
                    a_new = (al[0:Q_BLOCK, :] * old[2] + a_new[0], al[Q_BLOCK:, :] * old[2] + a_new[1])
                a_new = jnp.where(head0, a_new[0], a_new[1])
                for jj, rr in enumerate(qruns):
                    a_scr[rr, :] = a_new[jj * w:(jj + 1) * w, :]
                    for e in range(2):
                        m_scr[e, rr, :] = m_new[e * Q_BLOCK + jj * w:e * Q_BLOCK + (jj + 1) * w, :]
                        l_scr[e, rr, :] = l_new[e * Q_BLOCK + jj * w:e * Q_BLOCK + (jj + 1) * w, :]
            return carry

        lax.fori_loop(0, n_units // ATTN_UNROLL, group, 0)

    for rho in range(DIL_STEP):
        for j in range(DIL_MAX // DIL_STEP):
            src = slice((DIL_STEP * j + rho) * per, (DIL_STEP * j + rho + 1) * per)
            q_scr[pl.ds(rho * quarter + j, per, stride=DIL_STEP), :] = a_scr[src, :] / jnp.where(
                head0, l_scr[0, src, :], l_scr[1, src, :])
    for rho in range(DIL_STEP):
        o_ref[pl.ds(rho, quarter, stride=DIL_STEP), :] = q_scr[rho * quarter:(rho + 1) * quarter, :]


def _attn_prompt(z3, slopes, keep):
    b, seq, _ = z3.shape
    win = pl.BlockSpec((None, PAIR, keep), lambda i, p: (i, p, 0))
    col = lambda off: pl.BlockSpec((None, seq, PAIR), lambda i, p: (i, 0, off + p))
    return pl.pallas_call(
        functools.partial(_attn_prompt_kernel, seq=seq, keep=keep),
        out_shape=(jax.ShapeDtypeStruct((b, seq, C_HEADS), F32),
                   jax.ShapeDtypeStruct((b, C_HEADS, keep), F32),
                   jax.ShapeDtypeStruct((b, C_HEADS, keep), F32)),
        grid=(b, N_PAIRS),
        in_specs=[pl.BlockSpec(memory_space=pltpu.SMEM),
                  col(0), col(N_PAIRS), col(2 * N_PAIRS)],
        out_specs=(pl.BlockSpec((None, seq, PAIR), lambda i, p: (i, 0, p)), win, win),
        scratch_shapes=[pltpu.VMEM((seq, PAIR), F32), pltpu.VMEM((seq, PAIR), BF),
                        pltpu.VMEM((seq, PAIR), BF), pltpu.VMEM((seq, PAIR), F32)]
        + [pltpu.VMEM((2, seq, PAIR), F32)] * 2
        + [pltpu.VMEM((2, 2 * Q_BLOCK, 2 * Q_BLOCK), F32)],
        compiler_params=_params("arbitrary", "arbitrary"),
        name="attn_prompt",
    )(slopes, z3, z3, z3)


def _attn_sample_kernel(sl_ref, q_ref, kn_ref, vn_ref, ck_ref, cv_ref, o_ref, ko_ref, vo_ref,
                        *, s_new, cache_len):
    lane_t = lax.broadcasted_iota(jnp.int32, (1, PAIR), 1)

    def shift_in(c_ref, new_ref, out_ref):
        rolled = pltpu.roll(c_ref[...], cache_len - s_new, 1)
        new_t = jnp.concatenate([new_ref[...], jnp.zeros((PAIR - s_new, C_HEADS), F32)], axis=0).T
        tail = jnp.where(lane_t >= PAIR - s_new, pltpu.roll(new_t, PAIR - s_new, 1),
                         rolled[:, cache_len - PAIR:cache_len])
        out_ref[:, 0:cache_len - PAIR] = rolled[:, 0:cache_len - PAIR]
        out_ref[:, cache_len - PAIR:cache_len] = tail

    shift_in(ck_ref, kn_ref, ko_ref)
    shift_in(cv_ref, vn_ref, vo_ref)

    rows = s_new * N_HEADS
    ri = lax.broadcasted_iota(jnp.int32, (rows, 1), 0)
    lane = lax.broadcasted_iota(jnp.int32, (1, C_HEADS), 1)
    own = (ri & (N_HEADS - 1)) == (lane >> 6)
    q = q_ref[...]
    qe = jnp.broadcast_to(q[:, None, :], (s_new, N_HEADS, C_HEADS)).reshape(rows, C_HEADS)
    qe = jnp.where(own, qe, 0.0)
    slope = sl_ref[:, 0:1]
    spos = ri >> 3

    def weights(dist):
        mult = jnp.zeros(dist.shape, F32)
        for window, dil in DILATIONS:
            hit = (dist >= 0) & (dist <= window) & ((dist & (dil - 1)) == 0)
            mult = mult + hit.astype(F32)
        return mult

    dist_c = cache_len + spos - lax.broadcasted_iota(jnp.int32, (1, cache_len), 1)
    dist_n = spos - lax.broadcasted_iota(jnp.int32, (1, s_new), 1)
    mult_c = weights(dist_c)
    mult_n = weights(dist_n)
    sc = _dot(qe, ck_ref[...]) * (HEAD_DIM ** -0.5) - slope * dist_c.astype(F32)
    sn = _dot_nt(qe, kn_ref[...]) * (HEAD_DIM ** -0.5) - slope * dist_n.astype(F32)
    sc = jnp.where(mult_c > 0, sc, NEG_BIG)
    sn = jnp.where(mult_n > 0, sn, NEG_BIG)
    mx = jnp.maximum(jnp.max(sc, axis=1, keepdims=True), jnp.max(sn, axis=1, keepdims=True))
    pc = mult_c * jnp.exp(sc - mx)
    pn = mult_n * jnp.exp(sn - mx)
    den = jnp.sum(pc, axis=1, keepdims=True) + jnp.sum(pn, axis=1, keepdims=True)
    num = _dot_nt(pc, cv_ref[...]) + _dot(pn, vn_ref[...])
    num = jnp.where(own, num, 0.0).reshape(s_new, N_HEADS, C_HEADS).sum(axis=1)
    den = jnp.where(own, den, 0.0).reshape(s_new, N_HEADS, C_HEADS).sum(axis=1)
    o_ref[...] = num / den


def _attn_sample(z3, cache_k, cache_v, slopes):
    b, s_new, _ = z3.shape
    cache_len = cache_k.shape[2]
    assert cache_len % PAIR == 0 and cache_len > PAIR and s_new < PAIR
    new = lambda c: pl.BlockSpec((None, s_new, C_HEADS), lambda i: (i, 0, c))
    cache = pl.BlockSpec((None, C_HEADS, cache_len), lambda i: (i, 0, 0))
    slope_rows = jnp.broadcast_to(jnp.tile(slopes, s_new)[:, None], (s_new * N_HEADS, PAIR))
    return pl.pallas_call(
        functools.partial(_attn_sample_kernel, s_new=s_new, cache_len=cache_len),
        out_shape=(jax.ShapeDtypeStruct((b, s_new, C_HEADS), F32),
                   jax.ShapeDtypeStruct(cache_k.shape, F32),
                   jax.ShapeDtypeStruct(cache_v.shape, F32)),
        grid=(b,),
        in_specs=[_const_spec((s_new * N_HEADS, PAIR)), new(0), new(1), new(2), cache, cache],
        out_specs=(new(0), cache, cache),
        compiler_params=_params("arbitrary"),
        name="attn_sample",
    )(slope_rows, z3, z3, z3, cache_k, cache_v)


def _rwkv_kernel(r_ref, k_ref, v_ref, wag_ref, pr_ref, pk_ref, pv_ref, pwag_ref,
                 sr_ref, sk_ref, sv_ref, swag_ref, m0_ref, vec_ref, muwag_ref, lora_ref, seg_ref,
                 o_ref, mout_ref, m_scr, y_scr, *, t_valid, nsb, chunk, per_chunk_state):
    tb = pl.program_id(1)
    first = tb == 0
    n_rows = nsb * SUPER
    shift = chunk.bit_length() - 1
    n_chunks = SUPER // chunk

    def pair_state(ref, p):
        zero = jnp.zeros((HEAD_DIM, HEAD_DIM), F32)
        return jnp.concatenate([jnp.concatenate([ref[2 * p], zero], axis=1),
                                jnp.concatenate([zero, ref[2 * p + 1]], axis=1)], axis=0)

    if not per_chunk_state:
        @pl.when(first)
        def _():
            for p in range(N_PAIRS):
                m_scr[p] = pair_state(m0_ref.at[0], p)

    rows = lax.broadcasted_iota(jnp.int32, (n_rows, 1), 0)

    def token_shift(cur_ref, prev_ref, carry_ref, mu):
        cur = cur_ref[...]
        if per_chunk_state:
            prev = prev_ref[...]
        else:
            last = jnp.where(first, carry_ref[...], prev_ref[7:8, :])
            prev = jnp.where(rows == 0, last, pltpu.roll(cur, 1, 0))
        return cur + (prev - cur) * mu

    vec = vec_ref[...]
    row = lambda i: vec[i:i + 1, :]
    zr = token_shift(r_ref, pr_ref, sr_ref, row(_V_MU_R))
    zk = token_shift(k_ref, pk_ref, sk_ref, row(_V_MU_K))
    zv = token_shift(v_ref, pv_ref, sv_ref, row(_V_MU_V))
    zwag = token_shift(wag_ref, pwag_ref, swag_ref, muwag_ref[...])
    wa = zwag[:, 0:D_LORA_WA]
    gi = zwag[:, D_LORA_WA:D_LORA_WA + D_G_LORA]

    wlin = row(_V_W0) + _dot(jnp.tanh(wa), lora_ref[0])
    softplus = jnp.maximum(-wlin, 0.0) + jnp.log(1.0 + jnp.exp(-jnp.abs(wlin)))
    w_log = -softplus - 0.5
    ld = -jnp.exp(w_log)
    lr = jax.nn.sigmoid(row(_V_A0) + _dot(wa, lora_ref[1]))
    gate = _dot(jax.nn.sigmoid(gi), lora_ref[2])
    seg = seg_ref[...]

    def head_sum(x):
        return jnp.concatenate([_dot_exact_rhs(x[:, i:i + SEG_W], seg)
                                for i in range(0, C_HEADS, SEG_W)], axis=1)

    kk = zk * row(_V_KK)
    kk = kk / jnp.maximum(jnp.sqrt(head_sum(kk * kk)), 1e-12)
    kmod = zk * (1.0 + (lr - 1.0) * row(_V_KA))
    vv = zv
    if t_valid is not None:
        live = ((rows & (chunk - 1)) if per_chunk_state else (rows + tb * n_rows)) < t_valid
        ld = jnp.where(live, ld, 0.0)
        kk = jnp.where(live, kk, 0.0)
        kmod = jnp.where(live, kmod, 0.0)
        vv = jnp.where(live, vv, 0.0)

    span = min(PAIR, n_rows)
    ti = lax.broadcasted_iota(jnp.int32, (span, span), 0)
    tj = lax.broadcasted_iota(jnp.int32, (span, span), 1)
    same_chunk = (ti >> shift) == (tj >> shift)
    sums = jnp.concatenate([same_chunk & (ti >= tj), same_chunk], axis=0).astype(BF)
    cum_tot = [_dot_exact_lhs(sums, ld[i:i + span, :]) for i in range(0, n_rows, span)]
    cum = jnp.concatenate([x[0:span, :] for x in cum_tot], axis=0)
    tot = jnp.concatenate([x[span:2 * span, :] for x in cum_tot], axis=0)
    dec_in = jnp.exp(cum)
    dec_ex = jnp.exp(cum - ld)
    dec_inv = jnp.exp(-cum)
    dec_end = jnp.exp(tot - cum)
    dec_tot = jnp.exp(tot)
    beta = kk * lr
    abar = -(kk * dec_ex)
    rbar = zr * dec_in
    bt = beta * dec_inv
    kt = kmod * dec_inv
    bh = beta * dec_end
    kh = kmod * dec_end

    ri = lax.broadcasted_iota(jnp.int32, (PAIR, PAIR), 0)
    ci = lax.broadcasted_iota(jnp.int32, (PAIR, PAIR), 1)
    same_blk = (ri >> shift) == (ci >> shift)
    strict = same_blk & (ri > ci)
    incl = same_blk & (ri >= ci)
    same_head = (ri >> 6) == (ci >> 6)
    eye = ri == ci
    lane = lax.broadcasted_iota(jnp.int32, (1, PAIR), 1)
    head0 = lane < HEAD_DIM
    zeros_sp = jnp.zeros((SUPER, PAIR), F32)
    zeros_pp = jnp.zeros((PAIR, PAIR), F32)

    def stack(x):
        return jnp.concatenate([jnp.where(head0, x, 0.0), jnp.where(head0, 0.0, x)], axis=0)

    def unstack(x):
        return x[0:SUPER, :] + x[SUPER:PAIR, :]

    def both_heads(x, keep):
        return jnp.where(keep, jnp.concatenate([x, x], axis=0), 0.0)

    eye_f = jnp.where(eye, 1.0, 0.0)
    in_chunk = [((lane & (SUPER - 1)) >> shift) == c for c in range(n_chunks)]

    def phase1(blocks, out):
        units = [(s, p) for s in blocks for p in range(N_PAIRS)]
        tile = lambda x, u: x[u[0] * SUPER:(u[0] + 1) * SUPER, u[1] * PAIR:(u[1] + 1) * PAIR]
        ab = [tile(abar, u) for u in units]
        rb = [tile(rbar, u) for u in units]
        v_p = [tile(vv, u) for u in units]
        v_s = [stack(x) for x in v_p]
        a_all = [_dot_nt(jnp.concatenate([a, r], axis=0),
                         jnp.concatenate([stack(tile(bt, u)), stack(tile(kt, u))], axis=0))
                 for a, r, u in zip(ab, rb, units)]
        yield
        n_ab = [both_heads(a[0:SUPER, 0:PAIR], strict) for a in a_all]
        a_ak = [both_heads(a[0:SUPER, PAIR:2 * PAIR], strict) for a in a_all]
        a_rbk = [jnp.concatenate([both_heads(a[SUPER:PAIR, 0:PAIR], incl),
                                  both_heads(a[SUPER:PAIR, PAIR:2 * PAIR], incl)], axis=1)
                 for a in a_all]
        tinv = [eye_f + n for n in n_ab]
        power = n_ab
        for _ in range(shift - 1):
            power = [_dot(x, x) for x in power]
            yield
            tinv = [t + _dot(x, t) for x, t in zip(power, tinv)]
            yield
        u_s = [_dot(a, v) for a, v in zip(a_ak, v_s)]
        yield
        ta = [_dot(t, jnp.concatenate([stack(a), u], axis=1))
              for t, a, u in zip(tinv, ab, u_s)]
        yield
        ry = [_dot(a, jnp.concatenate([t, jnp.concatenate([zeros_pp, v], axis=1)], axis=0))
              for a, t, v in zip(a_rbk, ta, v_s)]
        yield
        r1 = [unstack(stack(r) + y[:, 0:PAIR]) for r, y in zip(rb, ry)]
        y0 = [unstack(y[:, PAIR:2 * PAIR]) for y in ry]
        bk_t = [jnp.concatenate([tile(bh, u), tile(kh, u)], axis=0).astype(BF).T
                for u in units]
        rhs3 = [jnp.concatenate([jnp.concatenate([unstack(t[:, 0:PAIR]),
                                                  unstack(t[:, PAIR:2 * PAIR])], axis=1),
                                 jnp.concatenate([zeros_sp, v], axis=1)], axis=0)
                for t, v in zip(ta, v_p)]
        gh = [_dot(jnp.concatenate([jnp.where(in_chunk[c], b, jnp.zeros_like(b))
                                    for c in range(n_chunks)], axis=0), r)
              for b, r in zip(bk_t, rhs3)]
        for i, u in enumerate(units):
            out[u] = (r1[i], y0[i], gh[i])
        yield

    m = None if per_chunk_state else [m_scr[p] for p in range(N_PAIRS)]

    def phase2(blocks, fac):
        for s in blocks:
            for c in range(n_chunks):
                tok = slice(c * chunk, (c + 1) * chunk)
                r0 = s * SUPER + c * chunk
                seq_i = s * n_chunks + c
                for p in range(N_PAIRS):
                    r1, y0, gh = fac[(s, p)]
                    sl = slice(p * PAIR, (p + 1) * PAIR)
                    gh_c = gh[c * PAIR:(c + 1) * PAIR, :]
                    g_c = (jnp.where(eye, dec_tot[r0:r0 + 1, sl], 0.0)
                           + jnp.where(same_head, gh_c[:, 0:PAIR], 0.0))
                    h_c = jnp.where(same_head, gh_c[:, PAIR:2 * PAIR], 0.0)
                    m_in = pair_state(m0_ref.at[seq_i], p) if per_chunk_state else m[p]
                    y_scr[r0:r0 + chunk, sl] = _dot(r1[tok, :], m_in) + y0[tok, :]
                    m_out = _dot(g_c, m_in) + h_c
                    if per_chunk_state:
                        mout_ref[seq_i, 2 * p] = m_out[0:HEAD_DIM, 0:HEAD_DIM]
                        mout_ref[seq_i, 2 * p + 1] = m_out[HEAD_DIM:PAIR, HEAD_DIM:PAIR]
                    else:
                        m[p] = m_out
                yield

    groups = [list(range(g, min(g + RWKV_GROUP, nsb))) for g in range(0, nsb, RWKV_GROUP)]
    fac = {}
    for _ in phase1(groups[0], fac):
        pass
    for done, nxt in zip(groups, groups[1:]):
        steps = phase2(done, fac)
        for _ in phase1(nxt, fac):
            next(steps, None)
        for _ in steps:
            pass
    for _ in phase2(groups[-1], fac):
        pass
    if not per_chunk_state:
        for p in range(N_PAIRS):
            m_scr[p] = m[p]

    y = y_scr[...]
    mean = head_sum(y) * (1.0 / HEAD_DIM)
    yc = y - mean
    var = head_sum(yc * yc) * (1.0 / HEAD_DIM)
    yn = yc * lax.rsqrt(var + GN_EPS) * row(_V_LNW) + row(_V_LNB)
    bonus = head_sum(zr * kmod * row(_V_RK)) * zv
    o_ref[...] = (yn + bonus) * gate

    if not per_chunk_state:
        @pl.when(tb == pl.num_programs(1) - 1)
        def _():
            for p in range(N_PAIRS):
                mout_ref[0, 2 * p] = m_scr[p, 0:HEAD_DIM, 0:HEAD_DIM]
                mout_ref[0, 2 * p + 1] = m_scr[p, HEAD_DIM:PAIR, HEAD_DIM:PAIR]


def _rwkv(z3, prev3, shift_prev, m0, wts, t_valid, nsb, chunk, per_chunk_state):
    b, seq, _ = z3.shape
    n_rows = nsb * SUPER
    assert seq % n_rows == 0 and SUPER % chunk == 0 and chunk % 8 == 0
    nt = seq // n_rows
    n_state = n_rows // chunk if per_chunk_state else 1
    assert m0.shape[0] == b * n_state and (nt == 1 or not per_chunk_state)
    cur = lambda w, c: pl.BlockSpec((None, n_rows, w), lambda i, t: (i, t, c))
    if per_chunk_state:
        prev = cur
    else:
        prev = lambda w, c: pl.BlockSpec(
            (None, 8, w), lambda i, t: (i, jnp.maximum(t * (n_rows // 8) - 1, 0), c))
    carry = lambda w: pl.BlockSpec((None, 1, w), lambda i, t: (i, 0, 0))
    state = pl.BlockSpec((n_state, N_HEADS, HEAD_DIM, HEAD_DIM), lambda i, t: (i, 0, 0, 0))
    wag_w = D_LORA_WA + D_G_LORA
    sp = shift_prev[:, None, :]
    all_valid = t_valid == (chunk if per_chunk_state else seq)
    return pl.pallas_call(
        functools.partial(_rwkv_kernel, t_valid=None if all_valid else t_valid, nsb=nsb,
                          chunk=chunk, per_chunk_state=per_chunk_state),
        out_shape=(jax.ShapeDtypeStruct((b, seq, C_HEADS), F32),
                   jax.ShapeDtypeStruct(m0.shape, F32)),
        grid=(b, nt),
        in_specs=[cur(C_HEADS, 3), cur(C_HEADS, 4), cur(C_HEADS, 5), cur(wag_w, 12),
                  prev(C_HEADS, 3), prev(C_HEADS, 4), prev(C_HEADS, 5), prev(wag_w, 12),
                  carry(C_HEADS), carry(C_HEADS), carry(C_HEADS), carry(wag_w),
                  state,
                  _const_spec((_V_ROWS, C_HEADS)), _const_spec((1, wag_w)),
                  _const_spec((3, PAIR, C_HEADS)), _const_spec((SEG_W, SEG_W))],
        out_specs=(pl.BlockSpec((None, n_rows, C_HEADS), lambda i, t: (i, t, 0)), state),
        scratch_shapes=[pltpu.VMEM((N_PAIRS, PAIR, PAIR), F32),
                        pltpu.VMEM((n_rows, C_HEADS), F32)],
        compiler_params=_params("arbitrary", "arbitrary"),
        name="rwkv",
    )(z3, z3, z3, z3, prev3, prev3, prev3, prev3,
      sp[:, :, 0:C_HEADS], sp[:, :, C_HEADS:2 * C_HEADS], sp[:, :, 2 * C_HEADS:3 * C_HEADS],
      sp[:, :, 3 * C_HEADS:], m0, wts["vec"], wts["mu_wag"], wts["lora"], wts["seg"])


def _state_to_pairs(s):
    return jnp.swapaxes(s, -1, -2)


def _pairs_to_state(m):
    return jnp.swapaxes(m, -1, -2)


def _pack_layer(w_in, mu_shift, w0, w2, a0, a2, g2, k_k, k_a, r_k, lnx_w, lnx_b, w_out, ln1_w,
                ln1_b, w_gate, w_up, w_down, ln2_w, ln2_b, w_ple_gate, w_ple_proj, ln3_w, ln3_b):
    mu_r, mu_k, mu_v = (mu_shift[i * C_HEADS:(i + 1) * C_HEADS] for i in range(3))
    vec = jnp.stack([mu_r, mu_k, mu_v, w0, a0, k_k, k_a, r_k.reshape(-1), lnx_w, lnx_b])
    vec = jnp.concatenate([vec, jnp.zeros((_V_ROWS - vec.shape[0], C_HEADS), F32)], axis=0)
    half = D_LORA_WA // 2
    zeros = jnp.zeros((half, C_HEADS), F32)
    lora = jnp.stack([jnp.concatenate([w2, zeros], axis=0),
                      jnp.concatenate([zeros, a2], axis=0),
                      g2]).astype(BF)
    head_of = jnp.arange(SEG_W) // HEAD_DIM
    seg = (head_of[:, None] == head_of[None, :]).astype(BF)
    ln = jnp.stack([ln1_w, ln1_b, ln2_w, ln2_b, ln3_w, ln3_b,
                    jnp.zeros_like(ln1_w), jnp.zeros_like(ln1_w)])

    def col_chunks(w, width):
        k, n = w.shape
        return jnp.swapaxes(w.astype(BF).reshape(k, n // width, width), 0, 1)

    return {
        "w_in": col_chunks(w_in, IN_CHUNK), "vec": vec, "mu_wag": mu_shift[None, 3 * C_HEADS:],
        "lora": lora, "seg": seg, "w_out": w_out.astype(BF),
        "w_gate": col_chunks(w_gate, FF_CHUNK), "w_up": col_chunks(w_up, FF_CHUNK),
        "w_down": w_down.astype(BF),
        "w_ple_gate": w_ple_gate.astype(BF), "w_ple_proj": w_ple_proj.astype(BF), "ln": ln,
    }


def _alibi_slopes():
    h = jnp.arange(1, N_HEADS + 1, dtype=F32)
    return jnp.exp2(-8.0 * h / N_HEADS)


def _layer(x, p_l, wts, alpha, shift_prev, wkv_prev, cache_k=None, cache_v=None):
    b, seq, _ = x.shape
    x2d = x.reshape(b * seq, D_MODEL)
    z = _in_proj(x2d, wts["w_in"])
    z3 = z.reshape(b, seq, D_IN)
    slopes = _alibi_slopes()
    if cache_k is None:
        keep = min(WINDOW_MAX, seq)
        att, k_win, v_win = _attn_prompt(z3, slopes, keep)
        from_cm = lambda c: jnp.transpose(c.reshape(b, N_HEADS, HEAD_DIM, keep), (0, 3, 1, 2))
        k_win = from_cm(k_win)
        v_win = from_cm(v_win)
    else:
        cache_len = cache_k.shape[1]
        to_cm = lambda c: jnp.transpose(c, (0, 2, 3, 1)).reshape(b, C_HEADS, cache_len)
        from_cm = lambda c: jnp.transpose(c.reshape(b, N_HEADS, HEAD_DIM, cache_len), (0, 3, 1, 2))
        att, k_win, v_win = _attn_sample(z3, to_cm(cache_k), to_cm(cache_v), slopes)
        k_win = from_cm(k_win)
        v_win = from_cm(v_win)
    if seq % SUPER == 0:
        nsb = RWKV_BLOCKS if seq % (RWKV_BLOCKS * SUPER) == 0 else 1
        rw, m_last = _rwkv(z3, z3, shift_prev, _state_to_pairs(wkv_prev), wts, seq, nsb, CHUNK,
                           False)
    else:
        chunk = 8
        per_block = SUPER // chunk
        assert seq <= chunk and b % per_block == 0
        first_prev = jnp.concatenate([jnp.zeros((b, 1, 3 * C_HEADS), F32), shift_prev[:, None, :]],
                                     axis=-1)
        prev = jnp.concatenate([first_prev, z3[:, :seq - 1]], axis=1)
        blocks = lambda a: jnp.pad(a, ((0, 0), (0, chunk - seq), (0, 0))).reshape(
            b // per_block, SUPER, D_IN)
        rw, m_last = _rwkv(blocks(z3), blocks(prev), jnp.zeros((b // per_block, D_B_IN), F32),
                           _state_to_pairs(wkv_prev), wts, seq, 1, chunk, True)
        rw = rw.reshape(b, chunk, C_HEADS)[:, :seq]
    y = _post(att.reshape(b * seq, C_HEADS), rw.reshape(b * seq, C_HEADS), x2d,
              p_l.reshape(b * seq, PLE_DIM), wts, alpha)
    shift_new = z3[:, seq - 1, 3 * C_HEADS:]
    return y.reshape(b, seq, D_MODEL), k_win, v_win, shift_new, _pairs_to_state(m_last)


def kernel(x_prompt, x_sample, p_prompt, p_sample, cache_k_win, cache_v_win, state_wkv, state_shift, w_in, mu_shift, w0, w2, a0, a2, g2, k_k, k_a, r_k, lnx_w, lnx_b, w_out, ln1_w, ln1_b, w_gate, w_up, w_down, ln2_w, ln2_b, w_ple_gate, w_ple_proj, ln3_w, ln3_b):
    depth = w_in.shape[0]
    alpha = float((2 * depth) ** 0.25)
    xp, xs = x_prompt, x_sample
    bp = x_prompt.shape[0]
    shift0 = jnp.zeros((bp, D_B_IN), x_prompt.dtype)
    wkv0 = jnp.zeros((bp, N_HEADS, HEAD_DIM, HEAD_DIM), state_wkv.dtype)
    outs = [[] for _ in range(8)]
    for l in range(depth):
        wts = _pack_layer(w_in[l], mu_shift[l], w0[l], w2[l], a0[l], a2[l], g2[l], k_k[l], k_a[l],
                          r_k[l], lnx_w[l], lnx_b[l], w_out[l], ln1_w[l], ln1_b[l], w_gate[l],
                          w_up[l], w_down[l], ln2_w[l], ln2_b[l], w_ple_gate[l], w_ple_proj[l],
                          ln3_w[l], ln3_b[l])
        xp, kw, vw, sh, wk = _layer(xp, p_prompt[l], wts, alpha, shift0, wkv0)
        for lst, val in zip(outs[0:4], (kw, vw, wk, sh)):
            lst.append(val)
        xs, kw, vw, sh, wk = _layer(xs, p_sample[l], wts, alpha, state_shift[l], state_wkv[l],
                                    cache_k_win[l], cache_v_win[l])
        for lst, val in zip(outs[4:8], (kw, vw, wk, sh)):
            lst.append(val)
    return (xp, xs) + tuple(jnp.stack(o) for o in outs)
```

```python
import functools

import jax
import jax.numpy as jnp
from jax import lax
from jax.experimental import pallas as pl
from jax.experimental.pallas import tpu as pltpu

BF = jnp.bfloat16
F32 = jnp.float32

D_MODEL = 1024
HEAD_DIM = 64
N_HEADS = 8
C_HEADS = N_HEADS * HEAD_DIM
PAIR = 2 * HEAD_DIM
N_PAIRS = N_HEADS // 2
DILATIONS = ((128, 1), (512, 4), (2048, 16))
WINDOW_MAX = 2048
Q_BLOCK = 128
DIL_MAX = max(d for _, d in DILATIONS)
DIL_STEP = 4
D_LORA_WA = 128
D_G_LORA = 128
D_B_IN = 3 * C_HEADS + D_LORA_WA + D_G_LORA
D_IN = 3 * C_HEADS + D_B_IN
D_FF = 2816
PLE_DIM = 256
LN_EPS = 1e-5
GN_EPS = 64e-5
NEG_BIG = -1e30

CHUNK = 16
SUPER = 64
N_CHUNKS = SUPER // CHUNK
RWKV_BLOCKS = 8
RWKV_GROUP = 2
FF_CHUNK = 256
SEG_W = 256
ATTN_UNROLL = 4
IN_CHUNK = 256
VMEM_LIMIT_BYTES = 56 * 1024 * 1024

(_V_MU_R, _V_MU_K, _V_MU_V, _V_W0, _V_A0, _V_KK, _V_KA, _V_RK, _V_LNW, _V_LNB) = range(10)
_V_ROWS = 16


def _params(*sem):
    return pltpu.CompilerParams(dimension_semantics=sem, vmem_limit_bytes=VMEM_LIMIT_BYTES)


def _const_spec(shape):
    nd = len(shape)
    return pl.BlockSpec(shape, lambda *_: (0,) * nd, pipeline_mode=pl.Buffered(1))


def _dot(a, b):
    return jnp.dot(a.astype(BF), b.astype(BF), preferred_element_type=F32)


def _dot_nt(a, b):
    return lax.dot_general(a.astype(BF), b.astype(BF), (((1,), (1,)), ((), ())),
                           preferred_element_type=F32)


def _split(x):
    hi = x.astype(BF)
    lo = (x - hi.astype(F32)).astype(BF)
    return hi, lo


def _dot_exact_lhs(mask_bf, x):
    hi, lo = _split(x)
    return (jnp.dot(mask_bf, hi, preferred_element_type=F32)
            + jnp.dot(mask_bf, lo, preferred_element_type=F32))


def _dot_exact_rhs(x, mask_bf):
    hi, lo = _split(x)
    return (jnp.dot(hi, mask_bf, preferred_element_type=F32)
            + jnp.dot(lo, mask_bf, preferred_element_type=F32))


def _layer_norm(x, w, b):
    mu = jnp.mean(x, axis=-1, keepdims=True)
    xc = x - mu
    var = jnp.mean(xc * xc, axis=-1, keepdims=True)
    return xc * lax.rsqrt(var + LN_EPS) * w + b


def _in_proj_kernel(x_ref, w_ref, o_ref):
    xb = x_ref[...].astype(BF)
    for c in range(D_IN // IN_CHUNK):
        o_ref[:, c * IN_CHUNK:(c + 1) * IN_CHUNK] = jnp.dot(xb, w_ref[c],
                                                            preferred_element_type=F32)


def _in_proj(x2d, w_bf):
    m = x2d.shape[0]
    tm = min(1024, m)
    assert m % tm == 0
    return pl.pallas_call(
        _in_proj_kernel,
        out_shape=jax.ShapeDtypeStruct((m, D_IN), F32),
        grid=(m // tm,),
        in_specs=[pl.BlockSpec((tm, D_MODEL), lambda i: (i, 0)),
                  _const_spec((D_IN // IN_CHUNK, D_MODEL, IN_CHUNK))],
        out_specs=pl.BlockSpec((tm, D_IN), lambda i: (i, 0)),
        compiler_params=_params("arbitrary"),
        name="in_proj",
    )(x2d, w_bf)


def _post_kernel(att_ref, rw_ref, x_ref, p_ref, wo_ref, wg_ref, wu_ref, wd_ref, wpg_ref,
                 wpp_ref, ln_ref, o_ref, *, alpha):
    mix = (_dot(att_ref[...], wo_ref[0:C_HEADS, :])
           + _dot(rw_ref[...], wo_ref[C_HEADS:2 * C_HEADS, :]))
    h = _layer_norm(alpha * x_ref[...] + mix, ln_ref[0:1, :], ln_ref[1:2, :])
    hb = h.astype(BF)
    ffn = jnp.zeros_like(h)
    for c in range(D_FF // FF_CHUNK):
        g = jnp.dot(hb, wg_ref[c], preferred_element_type=F32)
        u = jnp.dot(hb, wu_ref[c], preferred_element_type=F32)
        act = g * jax.nn.sigmoid(g) * u
        ffn = ffn + jnp.dot(act.astype(BF), wd_ref[c * FF_CHUNK:(c + 1) * FF_CHUNK, :],
                            preferred_element_type=F32)
    h = _layer_norm(alpha * h + ffn, ln_ref[2:3, :], ln_ref[3:4, :])
    ple = jax.nn.sigmoid(_dot(h, wpg_ref[...])) * _dot(p_ref[...], wpp_ref[...])
    o_ref[...] = _layer_norm(alpha * h + ple, ln_ref[4:5, :], ln_ref[5:6, :])


def _post(att, rw, x2d, p2d, wts, alpha):
    m = x2d.shape[0]
    tm = min(512, m)
    assert m % tm == 0
    row = lambda w: pl.BlockSpec((tm, w), lambda i: (i, 0))
    return pl.pallas_call(
        functools.partial(_post_kernel, alpha=alpha),
        out_shape=jax.ShapeDtypeStruct((m, D_MODEL), F32),
        grid=(m // tm,),
        in_specs=[row(C_HEADS), row(C_HEADS), row(D_MODEL), row(PLE_DIM),
                  _const_spec((2 * C_HEADS, D_MODEL)),
                  _const_spec((D_FF // FF_CHUNK, D_MODEL, FF_CHUNK)),
                  _const_spec((D_FF // FF_CHUNK, D_MODEL, FF_CHUNK)),
                  _const_spec((D_FF, D_MODEL)),
                  _const_spec((D_MODEL, D_MODEL)), _const_spec((PLE_DIM, D_MODEL)),
                  _const_spec((8, D_MODEL))],
        out_specs=row(D_MODEL),
        compiler_params=_params("arbitrary"),
        name="post",
    )(att, rw, x2d, p2d, wts["w_out"], wts["w_gate"], wts["w_up"], wts["w_down"],
      wts["w_ple_gate"], wts["w_ple_proj"], wts["ln"])


def _attn_prompt_kernel(sl_ref, q_ref, k_ref, v_ref, o_ref, kw_ref, vw_ref, q_scr, k_scr, v_scr,
                        a_scr, m_scr, l_scr, mb_scr, *, seq, keep):
    pair = pl.program_id(1)
    slopes = (sl_ref[2 * pair], sl_ref[2 * pair + 1])
    lane = lax.broadcasted_iota(jnp.int32, (1, PAIR), 1)
    head0 = lane < HEAD_DIM
    kw_ref[...] = k_ref[seq - keep:seq, :].T
    vw_ref[...] = v_ref[seq - keep:seq, :].T
    qi = lax.broadcasted_iota(jnp.int32, (Q_BLOCK, 1), 0)
    ki = lax.broadcasted_iota(jnp.int32, (1, 2 * Q_BLOCK), 1)

    per = seq // DIL_MAX
    quarter = seq // DIL_STEP

    def to_residue_major(src_ref, dst_ref, convert):
        for rho in range(DIL_STEP):
            a_scr[rho * quarter:(rho + 1) * quarter, :] = src_ref[
                pl.ds(rho, quarter, stride=DIL_STEP), :]
        for rho in range(DIL_STEP):
            for j in range(DIL_MAX // DIL_STEP):
                r = DIL_STEP * j + rho
                dst_ref[r * per:(r + 1) * per, :] = convert(
                    a_scr[pl.ds(rho * quarter + j, per, stride=DIL_STEP), :])

    to_residue_major(q_ref, q_scr, lambda x: x * (HEAD_DIM ** -0.5))
    to_residue_major(k_ref, k_scr, lambda x: x.astype(BF))
    to_residue_major(v_ref, v_scr, lambda x: x.astype(BF))

    def gather(ref, runs):
        return jnp.concatenate([ref[rr, :] for rr in runs], axis=0)

    for window, dil in DILATIONS:
        n_runs = DIL_MAX // dil
        w = Q_BLOCK // n_runs
        nb = seq // dil // Q_BLOCK
        n_units = nb * dil
        assert nb >= 2 and n_units * Q_BLOCK == seq and n_units % ATTN_UNROLL == 0 and w % 8 == 0
        tq = (qi & (w - 1)) * DIL_MAX + (qi >> (w.bit_length() - 1)) * dil
        tk = (ki & (2 * w - 1)) * DIL_MAX + (ki >> (w.bit_length())) * dil
        if dil == 1:
            tk = ki
        for case in range(2):
            dist = case * Q_BLOCK * dil + tq - tk
            valid = (dist >= 0) & (dist <= window)
            distf = dist.astype(F32)
            mb_scr[case] = jnp.concatenate(
                [jnp.where(valid, -slopes[e] * distf, NEG_BIG) for e in range(2)], axis=0)

        def group(g, carry, dil=dil, n_runs=n_runs, w=w, first=(window, dil) == DILATIONS[0]):
            loaded = []
            for j in range(ATTN_UNROLL):
                u = g * ATTN_UNROLL + j
                n = u >> (dil.bit_length() - 1)
                rho = u & (dil - 1)
                nprev = jnp.maximum(n - 1, 0)
                base = [(dil * jj + rho) * per for jj in range(n_runs)]
                qruns = [pl.ds(pl.multiple_of(b + w * n, 8), w) for b in base]
                q = gather(q_scr, qruns)
                qq = jnp.concatenate([jnp.where(head0, q, 0.0), jnp.where(head0, 0.0, q)], axis=0)
                if dil == 1:
                    keys = pl.ds(pl.multiple_of(nprev * Q_BLOCK, Q_BLOCK), 2 * Q_BLOCK)
                    k = k_ref[keys, :].astype(BF)
                    v = v_ref[keys, :].astype(BF)
                else:
                    kruns = [pl.ds(pl.multiple_of(b + w * nprev, 16), 2 * w) for b in base]
                    k = gather(k_scr, kruns)
                    v = gather(v_scr, kruns)
                old = None if first else (
                    jnp.concatenate([gather(m_scr.at[0], qruns), gather(m_scr.at[1], qruns)], axis=0),
                    jnp.concatenate([gather(l_scr.at[0], qruns), gather(l_scr.at[1], qruns)], axis=0),
                    gather(a_scr, qruns))
                loaded.append((qruns, jnp.minimum(n, 1), qq.astype(BF), k,
                               jnp.concatenate([v, jnp.ones_like(v)], axis=1),
                               old))
            scores = [_dot_nt(qq, k) + mb_scr[case] for (_, case, qq, k, _, _) in loaded]
            stats = []
            for (_, _, _, _, _, old), s in zip(loaded, scores):
                m_new = jnp.max(s, axis=1, keepdims=True)
                m_new = (jnp.broadcast_to(m_new, (2 * Q_BLOCK, PAIR)) if first
                         else jnp.maximum(old[0], m_new))
                p = jnp.exp(s - jnp.concatenate([m_new, m_new], axis=1)).astype(BF)
                stats.append((m_new, p, None if first else jnp.exp(old[0] - m_new)))
            pvs = [jnp.dot(p, vo, preferred_element_type=F32)
                   for (_, _, _, _, vo, _), (_, p, _) in zip(loaded, stats)]
            for (qruns, _, _, _, _, old), (m_new, _, al), pv in zip(loaded, stats, pvs):
                l_new = pv[:, PAIR:2 * PAIR]
                a_new = (pv[0:Q_BLOCK, 0:PAIR], pv[Q_BLOCK:, 0:PAIR])
                if not first:
                    l_new = al * old[1] + l_new
                    a_new = (al[0:Q_BLOCK, :] * old[2] + a_new[0], al[Q_BLOCK:, :] * old[2] + a_new[1])
                a_new = jnp.where(head0, a_new[0], a_new[1])
                for jj, rr in enumerate(qruns):
                    a_scr[rr, :] = a_new[jj * w:(jj + 1) * w, :]
                    for e in range(2):
                        m_scr[e, rr, :] = m_new[e * Q_BLOCK + jj * w:e * Q_BLOCK + (jj + 1) * w, :]
                        l_scr[e, rr, :] = l_new[e * Q_BLOCK + jj * w:e * Q_BLOCK + (jj + 1) * w, :]
            return carry

        lax.fori_loop(0, n_units // ATTN_UNROLL, group, 0)

    for rho in range(DIL_STEP):
        for j in range(DIL_MAX // DIL_STEP):
            src = slice((DIL_STEP * j + rho) * per, (DIL_STEP * j + rho + 1) * per)
            q_scr[pl.ds(rho * quarter + j, per, stride=DIL_STEP), :] = a_scr[src, :] / jnp.where(
                head0, l_scr[0, src, :], l_scr[1, src, :])
    for rho in range(DIL_STEP):
        o_ref[pl.ds(rho, quarter, stride=DIL_STEP), :] = q_scr[rho * quarter:(rho + 1) * quarter, :]


def _attn_prompt(z3, slopes, keep):
    b, seq, _ = z3.shape
    win = pl.BlockSpec((None, PAIR, keep), lambda i, p: (i, p, 0))
    col = lambda off: pl.BlockSpec((None, seq, PAIR), lambda i, p: (i, 0, off + p))
    return pl.pallas_call(
        functools.partial(_attn_prompt_kernel, seq=seq, keep=keep),
        out_shape=(jax.ShapeDtypeStruct((b, seq, C_HEADS), F32),
                   jax.ShapeDtypeStruct((b, C_HEADS, keep), F32),
                   jax.ShapeDtypeStruct((b, C_HEADS, keep), F32)),
        grid=(b, N_PAIRS),
        in_specs=[pl.BlockSpec(memory_space=pltpu.SMEM),
                  col(0), col(N_PAIRS), col(2 * N_PAIRS)],
        out_specs=(pl.BlockSpec((None, seq, PAIR), lambda i, p: (i, 0, p)), win, win),
        scratch_shapes=[pltpu.VMEM((seq, PAIR), F32), pltpu.VMEM((seq, PAIR), BF),
                        pltpu.VMEM((seq, PAIR), BF), pltpu.VMEM((seq, PAIR), F32)]
        + [pltpu.VMEM((2, seq, PAIR), F32)] * 2
        + [pltpu.VMEM((2, 2 * Q_BLOCK, 2 * Q_BLOCK), F32)],
        compiler_params=_params("arbitrary", "arbitrary"),
        name="attn_prompt",
    )(slopes, z3, z3, z3)


def _attn_sample_kernel(sl_ref, q_ref, kn_ref, vn_ref, ck_ref, cv_ref, o_ref, ko_ref, vo_ref,
                        *, s_new, cache_len):
    lane_t = lax.broadcasted_iota(jnp.int32, (1, PAIR), 1)

    def shift_in(c_ref, new_ref, out_ref):
        rolled = pltpu.roll(c_ref[...], cache_len - s_new, 1)
        new_t = jnp.concatenate([new_ref[...], jnp.zeros((PAIR - s_new, C_HEADS), F32)], axis=0).T
        tail = jnp.where(lane_t >= PAIR - s_new, pltpu.roll(new_t, PAIR - s_new, 1),
                         rolled[:, cache_len - PAIR:cache_len])
        out_ref[:, 0:cache_len - PAIR] = rolled[:, 0:cache_len - PAIR]
        out_ref[:, cache_len - PAIR:cache_len] = tail

    shift_in(ck_ref, kn_ref, ko_ref)
    shift_in(cv_ref, vn_ref, vo_ref)

    rows = s_new * N_HEADS
    ri = lax.broadcasted_iota(jnp.int32, (rows, 1), 0)
    lane = lax.broadcasted_iota(jnp.int32, (1, C_HEADS), 1)
    own = (ri & (N_HEADS - 1)) == (lane >> 6)
    q = q_ref[...]
    qe = jnp.broadcast_to(q[:, None, :], (s_new, N_HEADS, C_HEADS)).reshape(rows, C_HEADS)
    qe = jnp.where(own, qe, 0.0)
    slope = sl_ref[:, 0:1]
    spos = ri >> 3

    def weights(dist):
        mult = jnp.zeros(dist.shape, F32)
        for window, dil in DILATIONS:
            hit = (dist >= 0) & (dist <= window) & ((dist & (dil - 1)) == 0)
            mult = mult + hit.astype(F32)
        return mult

    dist_c = cache_len + spos - lax.broadcasted_iota(jnp.int32, (1, cache_len), 1)
    dist_n = spos - lax.broadcasted_iota(jnp.int32, (1, s_new), 1)
    mult_c = weights(dist_c)
    mult_n = weights(dist_n)
    sc = _dot(qe, ck_ref[...]) * (HEAD_DIM ** -0.5) - slope * dist_c.astype(F32)
    sn = _dot_nt(qe, kn_ref[...]) * (HEAD_DIM ** -0.5) - slope * dist_n.astype(F32)
    sc = jnp.where(mult_c > 0, sc, NEG_BIG)
    sn = jnp.where(mult_n > 0, sn, NEG_BIG)
    mx = jnp.maximum(jnp.max(sc, axis=1, keepdims=True), jnp.max(sn, axis=1, keepdims=True))
    pc = mult_c * jnp.exp(sc - mx)
    pn = mult_n * jnp.exp(sn - mx)
    den = jnp.sum(pc, axis=1, keepdims=True) + jnp.sum(pn, axis=1, keepdims=True)
    num = _dot_nt(pc, cv_ref[...]) + _dot(pn, vn_ref[...])
    num = jnp.where(own, num, 0.0).reshape(s_new, N_HEADS, C_HEADS).sum(axis=1)
    den = jnp.where(own, den, 0.0).reshape(s_new, N_HEADS, C_HEADS).sum(axis=1)
    o_ref[...] = num / den


def _attn_sample(z3, cache_k, cache_v, slopes):
    b, s_new, _ = z3.shape
    cache_len = cache_k.shape[2]
    assert cache_len % PAIR == 0 and cache_len > PAIR and s_new < PAIR
    new = lambda c: pl.BlockSpec((None, s_new, C_HEADS), lambda i: (i, 0, c))
    cache = pl.BlockSpec((None, C_HEADS, cache_len), lambda i: (i, 0, 0))
    slope_rows = jnp.broadcast_to(jnp.tile(slopes, s_new)[:, None], (s_new * N_HEADS, PAIR))
    return pl.pallas_call(
        functools.partial(_attn_sample_kernel, s_new=s_new, cache_len=cache_len),
        out_shape=(jax.ShapeDtypeStruct((b, s_new, C_HEADS), F32),
                   jax.ShapeDtypeStruct(cache_k.shape, F32),
                   jax.ShapeDtypeStruct(cache_v.shape, F32)),
        grid=(b,),
        in_specs=[_const_spec((s_new * N_HEADS, PAIR)), new(0), new(1), new(2), cache, cache],
        out_specs=(new(0), cache, cache),
        compiler_params=_params("arbitrary"),
        name="attn_sample",
    )(slope_rows, z3, z3, z3, cache_k, cache_v)


def _rwkv_kernel(r_ref, k_ref, v_ref, wag_ref, pr_ref, pk_ref, pv_ref, pwag_ref,
                 sr_ref, sk_ref, sv_ref, swag_ref, m0_ref, vec_ref, muwag_ref, lora_ref, seg_ref,
                 o_ref, mout_ref, m_scr, y_scr, *, t_valid, nsb, chunk, per_chunk_state):
    tb = pl.program_id(1)
    first = tb == 0
    n_rows = nsb * SUPER
    shift = chunk.bit_length() - 1
    n_chunks = SUPER // chunk

    lane = lax.broadcasted_iota(jnp.int32, (1, PAIR), 1)
    head0 = lane < HEAD_DIM

    def stack(x):
        return jnp.concatenate([jnp.where(head0, x, 0.0), jnp.where(head0, 0.0, x)], axis=0)

    def transposed(x):
        n = x.shape[1]
        eye_n = (lax.broadcasted_iota(jnp.int32, (n, n), 0)
                 == lax.broadcasted_iota(jnp.int32, (n, n), 1)).astype(BF)
        hi, lo = _split(x)
        return _dot_nt(eye_n, hi) + _dot_nt(eye_n, lo)

    def states_in(ref):
        return transposed(ref[...].reshape(N_HEADS * HEAD_DIM, HEAD_DIM))

    def pair_state(t, p):
        return stack(t[:, p * PAIR:(p + 1) * PAIR])

    def pair_out(m_pair):
        return transposed(m_pair[0:HEAD_DIM, :] + m_pair[HEAD_DIM:PAIR, :])

    if not per_chunk_state:
        @pl.when(first)
        def _():
            t_in = states_in(m0_ref.at[0])
            for p in range(N_PAIRS):
                m_scr[p] = pair_state(t_in, p)

    rows = lax.broadcasted_iota(jnp.int32, (n_rows, 1), 0)

    def token_shift(cur_ref, prev_ref, carry_ref, mu):
        cur = cur_ref[...]
        if per_chunk_state:
            prev = prev_ref[...]
        else:
            last = jnp.where(first, carry_ref[...], prev_ref[7:8, :])
            prev = jnp.where(rows == 0, last, pltpu.roll(cur, 1, 0))
        return cur + (prev - cur) * mu

    vec = vec_ref[...]
    row = lambda i: vec[i:i + 1, :]
    zr = token_shift(r_ref, pr_ref, sr_ref, row(_V_MU_R))
    zk = token_shift(k_ref, pk_ref, sk_ref, row(_V_MU_K))
    zv = token_shift(v_ref, pv_ref, sv_ref, row(_V_MU_V))
    zwag = token_shift(wag_ref, pwag_ref, swag_ref, muwag_ref[...])
    wa = zwag[:, 0:D_LORA_WA]
    gi = zwag[:, D_LORA_WA:D_LORA_WA + D_G_LORA]

    wlin = row(_V_W0) + _dot(jnp.tanh(wa), lora_ref[0])
    softplus = jnp.maximum(-wlin, 0.0) + jnp.log(1.0 + jnp.exp(-jnp.abs(wlin)))
    w_log = -softplus - 0.5
    ld = -jnp.exp(w_log)
    lr = jax.nn.sigmoid(row(_V_A0) + _dot(wa, lora_ref[1]))
    gate = _dot(jax.nn.sigmoid(gi), lora_ref[2])
    seg = seg_ref[...]

    def head_sum(x):
        return jnp.concatenate([_dot_exact_rhs(x[:, i:i + SEG_W], seg)
                                for i in range(0, C_HEADS, SEG_W)], axis=1)

    kk = zk * row(_V_KK)
    kk = kk / jnp.maximum(jnp.sqrt(head_sum(kk * kk)), 1e-12)
    kmod = zk * (1.0 + (lr - 1.0) * row(_V_KA))
    vv = zv
    if t_valid is not None:
        live = ((rows & (chunk - 1)) if per_chunk_state else (rows + tb * n_rows)) < t_valid
        ld = jnp.where(live, ld, 0.0)
        kk = jnp.where(live, kk, 0.0)
        kmod = jnp.where(live, kmod, 0.0)
        vv = jnp.where(live, vv, 0.0)

    span = min(PAIR, n_rows)
    ti = lax.broadcasted_iota(jnp.int32, (span, span), 0)
    tj = lax.broadcasted_iota(jnp.int32, (span, span), 1)
    same_chunk = (ti >> shift) == (tj >> shift)
    sums = jnp.concatenate([same_chunk & (ti >= tj), same_chunk], axis=0).astype(BF)
    cum_tot = [_dot_exact_lhs(sums, ld[i:i + span, :]) for i in range(0, n_rows, span)]
    cum = jnp.concatenate([x[0:span, :] for x in cum_tot], axis=0)
    tot = jnp.concatenate([x[span:2 * span, :] for x in cum_tot], axis=0)
    dec_in = jnp.exp(cum)
    dec_ex = jnp.exp(cum - ld)
    dec_inv = jnp.exp(-cum)
    dec_end = jnp.exp(tot - cum)
    dec_tot = jnp.exp(tot)
    beta = kk * lr
    abar = -(kk * dec_ex)
    rbar = zr * dec_in
    bt = beta * dec_inv
    kt = kmod * dec_inv
    bh = beta * dec_end
    kh = kmod * dec_end

    ri = lax.broadcasted_iota(jnp.int32, (PAIR, PAIR), 0)
    ci = lax.broadcasted_iota(jnp.int32, (PAIR, PAIR), 1)
    same_blk = (ri >> shift) == (ci >> shift)
    strict = same_blk & (ri > ci)
    incl = same_blk & (ri >= ci)
    same_head = (ri >> 6) == (ci >> 6)
    eye = ri == ci
    zeros_sp = jnp.zeros((SUPER, PAIR), F32)
    zeros_pp = jnp.zeros((PAIR, PAIR), F32)

    def unstack(x):
        return x[0:SUPER, :] + x[SUPER:PAIR, :]

    def both_heads(x, keep):
        return jnp.where(keep, jnp.concatenate([x, x], axis=0), 0.0)

    eye_f = jnp.where(eye, 1.0, 0.0)
    in_chunk = [((lane & (SUPER - 1)) >> shift) == c for c in range(n_chunks)]

    def phase1(blocks, out):
        units = [(s, p) for s in blocks for p in range(N_PAIRS)]
        tile = lambda x, u: x[u[0] * SUPER:(u[0] + 1) * SUPER, u[1] * PAIR:(u[1] + 1) * PAIR]
        ab = [tile(abar, u) for u in units]
        rb = [tile(rbar, u) for u in units]
        v_p = [tile(vv, u) for u in units]
        v_s = [stack(x) for x in v_p]
        a_all = [_dot_nt(jnp.concatenate([a, r], axis=0),
                         jnp.concatenate([stack(tile(bt, u)), stack(tile(kt, u))], axis=0))
                 for a, r, u in zip(ab, rb, units)]
        yield
        n_ab = [both_heads(a[0:SUPER, 0:PAIR], strict) for a in a_all]
        a_ak = [both_heads(a[0:SUPER, PAIR:2 * PAIR], strict) for a in a_all]
        a_rbk = [jnp.concatenate([both_heads(a[SUPER:PAIR, 0:PAIR], incl),
                                  both_heads(a[SUPER:PAIR, PAIR:2 * PAIR], incl)], axis=1)
                 for a in a_all]
        tinv = [eye_f + n for n in n_ab]
        power = n_ab
        for _ in range(shift - 1):
            power = [_dot(x, x) for x in power]
            yield
            tinv = [t + _dot(x, t) for x, t in zip(power, tinv)]
            yield
        u_s = [_dot(a, v) for a, v in zip(a_ak, v_s)]
        yield
        ta = [_dot(t, jnp.concatenate([stack(a), u], axis=1))
              for t, a, u in zip(tinv, ab, u_s)]
        yield
        ry = [_dot(a, jnp.concatenate([t, jnp.concatenate([zeros_pp, v], axis=1)], axis=0))
              for a, t, v in zip(a_rbk, ta, v_s)]
        yield
        r1 = [unstack(stack(r) + y[:, 0:PAIR]) for r, y in zip(rb, ry)]
        y0 = [unstack(y[:, PAIR:2 * PAIR]) for y in ry]
        bk_t = [jnp.concatenate([tile(bh, u), tile(kh, u)], axis=0).astype(BF).T
                for u in units]
        rhs3 = [jnp.concatenate([jnp.concatenate([unstack(t[:, 0:PAIR]),
                                                  unstack(t[:, PAIR:2 * PAIR])], axis=1),
                                 jnp.concatenate([zeros_sp, v], axis=1)], axis=0)
                for t, v in zip(ta, v_p)]
        gh = [_dot(jnp.concatenate([jnp.where(in_chunk[c], b, jnp.zeros_like(b))
                                    for c in range(n_chunks)], axis=0), r)
              for b, r in zip(bk_t, rhs3)]
        for i, u in enumerate(units):
            out[u] = (r1[i], y0[i], gh[i])
        yield

    m = None if per_chunk_state else [m_scr[p] for p in range(N_PAIRS)]

    def phase2(blocks, fac):
        for s in blocks:
            for c in range(n_chunks):
                tok = slice(c * chunk, (c + 1) * chunk)
                r0 = s * SUPER + c * chunk
                seq_i = s * n_chunks + c
                t_in = states_in(m0_ref.at[seq_i]) if per_chunk_state else None
                for p in range(N_PAIRS):
                    r1, y0, gh = fac[(s, p)]
                    sl = slice(p * PAIR, (p + 1) * PAIR)
                    gh_c = gh[c * PAIR:(c + 1) * PAIR, :]
                    g_c = (jnp.where(eye, dec_tot[r0:r0 + 1, sl], 0.0)
                           + jnp.where(same_head, gh_c[:, 0:PAIR], 0.0))
                    h_c = jnp.where(same_head, gh_c[:, PAIR:2 * PAIR], 0.0)
                    m_in = pair_state(t_in, p) if per_chunk_state else m[p]
                    y_scr[r0:r0 + chunk, sl] = _dot(r1[tok, :], m_in) + y0[tok, :]
                    m_out = _dot(g_c, m_in) + h_c
                    if per_chunk_state:
                        mout_ref[seq_i, 2 * p:2 * p + 2] = pair_out(m_out).reshape(
                            2, HEAD_DIM, HEAD_DIM)
                    else:
                        m[p] = m_out
                yield

    groups = [list(range(g, min(g + RWKV_GROUP, nsb))) for g in range(0, nsb, RWKV_GROUP)]
    fac = {}
    for _ in phase1(groups[0], fac):
        pass
    for done, nxt in zip(groups, groups[1:]):
        steps = phase2(done, fac)
        for _ in phase1(nxt, fac):
            next(steps, None)
        for _ in steps:
            pass
    for _ in phase2(groups[-1], fac):
        pass
    if not per_chunk_state:
        for p in range(N_PAIRS):
            m_scr[p] = m[p]

    y = y_scr[...]
    mean = head_sum(y) * (1.0 / HEAD_DIM)
    yc = y - mean
    var = head_sum(yc * yc) * (1.0 / HEAD_DIM)
    yn = yc * lax.rsqrt(var + GN_EPS) * row(_V_LNW) + row(_V_LNB)
    bonus = head_sum(zr * kmod * row(_V_RK)) * zv
    o_ref[...] = (yn + bonus) * gate

    if not per_chunk_state:
        @pl.when(tb == pl.num_programs(1) - 1)
        def _():
            for p in range(N_PAIRS):
                mout_ref[0, 2 * p:2 * p + 2] = pair_out(m_scr[p]).reshape(2, HEAD_DIM, HEAD_DIM)


def _rwkv(z3, prev3, shift_prev, m0, wts, t_valid, nsb, chunk, per_chunk_state):
    b, seq, _ = z3.shape
    n_rows = nsb * SUPER
    assert seq % n_rows == 0 and SUPER % chunk == 0 and chunk % 8 == 0
    nt = seq // n_rows
    n_state = n_rows // chunk if per_chunk_state else 1
    assert m0.shape[0] == b * n_state and (nt == 1 or not per_chunk_state)
    cur = lambda w, c: pl.BlockSpec((None, n_rows, w), lambda i, t: (i, t, c))
    if per_chunk_state:
        prev = cur
    else:
        prev = lambda w, c: pl.BlockSpec(
            (None, 8, w), lambda i, t: (i, jnp.maximum(t * (n_rows // 8) - 1, 0), c))
    carry = lambda w: pl.BlockSpec((None, 1, w), lambda i, t: (i, 0, 0))
    state = pl.BlockSpec((n_state, N_HEADS, HEAD_DIM, HEAD_DIM), lambda i, t: (i, 0, 0, 0))
    wag_w = D_LORA_WA + D_G_LORA
    sp = shift_prev[:, None, :]
    all_valid = t_valid == (chunk if per_chunk_state else seq)
    return pl.pallas_call(
        functools.partial(_rwkv_kernel, t_valid=None if all_valid else t_valid, nsb=nsb,
                          chunk=chunk, per_chunk_state=per_chunk_state),
        out_shape=(jax.ShapeDtypeStruct((b, seq, C_HEADS), F32),
                   jax.ShapeDtypeStruct(m0.shape, F32)),
        grid=(b, nt),
        in_specs=[cur(C_HEADS, 3), cur(C_HEADS, 4), cur(C_HEADS, 5), cur(wag_w, 12),
                  prev(C_HEADS, 3), prev(C_HEADS, 4), prev(C_HEADS, 5), prev(wag_w, 12),
                  carry(C_HEADS), carry(C_HEADS), carry(C_HEADS), carry(wag_w),
                  state,
                  _const_spec((_V_ROWS, C_HEADS)), _const_spec((1, wag_w)),
                  _const_spec((3, PAIR, C_HEADS)), _const_spec((SEG_W, SEG_W))],
        out_specs=(pl.BlockSpec((None, n_rows, C_HEADS), lambda i, t: (i, t, 0)), state),
        scratch_shapes=[pltpu.VMEM((N_PAIRS, PAIR, PAIR), F32),
                        pltpu.VMEM((n_rows, C_HEADS), F32)],
        compiler_params=_params("arbitrary", "arbitrary"),
        name="rwkv",
    )(z3, z3, z3, z3, prev3, prev3, prev3, prev3,
      sp[:, :, 0:C_HEADS], sp[:, :, C_HEADS:2 * C_HEADS], sp[:, :, 2 * C_HEADS:3 * C_HEADS],
      sp[:, :, 3 * C_HEADS:], m0, wts["vec"], wts["mu_wag"], wts["lora"], wts["seg"])


def _pack_layer(w_in, mu_shift, w0, w2, a0, a2, g2, k_k, k_a, r_k, lnx_w, lnx_b, w_out, ln1_w,
                ln1_b, w_gate, w_up, w_down, ln2_w, ln2_b, w_ple_gate, w_ple_proj, ln3_w, ln3_b):
    mu_r, mu_k, mu_v = (mu_shift[i * C_HEADS:(i + 1) * C_HEADS] for i in range(3))
    vec = jnp.stack([mu_r, mu_k, mu_v, w0, a0, k_k, k_a, r_k.reshape(-1), lnx_w, lnx_b])
    vec = jnp.concatenate([vec, jnp.zeros((_V_ROWS - vec.shape[0], C_HEADS), F32)], axis=0)
    half = D_LORA_WA // 2
    zeros = jnp.zeros((half, C_HEADS), F32)
    lora = jnp.stack([jnp.concatenate([w2, zeros], axis=0),
                      jnp.concatenate([zeros, a2], axis=0),
                      g2]).astype(BF)
    head_of = jnp.arange(SEG_W) // HEAD_DIM
    seg = (head_of[:, None] == head_of[None, :]).astype(BF)
    ln = jnp.stack([ln1_w, ln1_b, ln2_w, ln2_b, ln3_w, ln3_b,
                    jnp.zeros_like(ln1_w), jnp.zeros_like(ln1_w)])

    def col_chunks(w, width):
        k, n = w.shape
        return jnp.swapaxes(w.astype(BF).reshape(k, n // width, width), 0, 1)

    return {
        "w_in": col_chunks(w_in, IN_CHUNK), "vec": vec, "mu_wag": mu_shift[None, 3 * C_HEADS:],
        "lora": lora, "seg": seg, "w_out": w_out.astype(BF),
        "w_gate": col_chunks(w_gate, FF_CHUNK), "w_up": col_chunks(w_up, FF_CHUNK),
        "w_down": w_down.astype(BF),
        "w_ple_gate": w_ple_gate.astype(BF), "w_ple_proj": w_ple_proj.astype(BF), "ln": ln,
    }


def _alibi_slopes():
    h = jnp.arange(1, N_HEADS + 1, dtype=F32)
    return jnp.exp2(-8.0 * h / N_HEADS)


def _layer(x, p_l, wts, alpha, shift_prev, wkv_prev, cache_k=None, cache_v=None):
    b, seq, _ = x.shape
    x2d = x.reshape(b * seq, D_MODEL)
    z = _in_proj(x2d, wts["w_in"])
    z3 = z.reshape(b, seq, D_IN)
    slopes = _alibi_slopes()
    if cache_k is None:
        keep = min(WINDOW_MAX, seq)
        att, k_win, v_win = _attn_prompt(z3, slopes, keep)
        from_cm = lambda c: jnp.transpose(c.reshape(b, N_HEADS, HEAD_DIM, keep), (0, 3, 1, 2))
        k_win = from_cm(k_win)
        v_win = from_cm(v_win)
    else:
        cache_len = cache_k.shape[1]
        to_cm = lambda c: jnp.transpose(c, (0, 2, 3, 1)).reshape(b, C_HEADS, cache_len)
        from_cm = lambda c: jnp.transpose(c.reshape(b, N_HEADS, HEAD_DIM, cache_len), (0, 3, 1, 2))
        att, k_win, v_win = _attn_sample(z3, to_cm(cache_k), to_cm(cache_v), slopes)
        k_win = from_cm(k_win)
        v_win = from_cm(v_win)
    if seq % SUPER == 0:
        nsb = RWKV_BLOCKS if seq % (RWKV_BLOCKS * SUPER) == 0 else 1
        rw, m_last = _rwkv(z3, z3, shift_prev, wkv_prev, wts, seq, nsb, CHUNK,
                           False)
    else:
        chunk = 8
        per_block = SUPER // chunk
        assert seq <= chunk and b % per_block == 0
        first_prev = jnp.concatenate([jnp.zeros((b, 1, 3 * C_HEADS), F32), shift_prev[:, None, :]],
                                     axis=-1)
        prev = jnp.concatenate([first_prev, z3[:, :seq - 1]], axis=1)
        blocks = lambda a: jnp.pad(a, ((0, 0), (0, chunk - seq), (0, 0))).reshape(
            b // per_block, SUPER, D_IN)
        rw, m_last = _rwkv(blocks(z3), blocks(prev), jnp.zeros((b // per_block, D_B_IN), F32),
                           wkv_prev, wts, seq, 1, chunk, True)
        rw = rw.reshape(b, chunk, C_HEADS)[:, :seq]
    y = _post(att.reshape(b * seq, C_HEADS), rw.reshape(b * seq, C_HEADS), x2d,
              p_l.reshape(b * seq, PLE_DIM), wts, alpha)
    shift_new = z3[:, seq - 1, 3 * C_HEADS:]
    return y.reshape(b, seq, D_MODEL), k_win, v_win, shift_new, m_last


def kernel(x_prompt, x_sample, p_prompt, p_sample, cache_k_win, cache_v_win, state_wkv, state_shift, w_in, mu_shift, w0, w2, a0, a2, g2, k_k, k_a, r_k, lnx_w, lnx_b, w_out, ln1_w, ln1_b, w_gate, w_up, w_down, ln2_w, ln2_b, w_ple_gate, w_ple_proj, ln3_w, ln3_b):
    depth = w_in.shape[0]
    alpha = float((2 * depth) ** 0.25)
    xp, xs = x_prompt, x_sample
    bp = x_prompt.shape[0]
    shift0 = jnp.zeros((bp, D_B_IN), x_prompt.dtype)
    wkv0 = jnp.zeros((bp, N_HEADS, HEAD_DIM, HEAD_DIM), state_wkv.dtype)
    outs = [[] for _ in range(8)]
    for l in range(depth):
        wts = _pack_layer(w_in[l], mu_shift[l], w0[l], w2[l], a0[l], a2[l], g2[l], k_k[l], k_a[l],
                          r_k[l], lnx_w[l], lnx_b[l], w_out[l], ln1_w[l], ln1_b[l], w_gate[l],
                          w_up[l], w_down[l], ln2_w[l], ln2_b[l], w_ple_gate[l], w_ple_proj[l],
                          ln3_w[l], ln3_b[l])
        xp, kw, vw, sh, wk = _layer(xp, p_prompt[l], wts, alpha, shift0, wkv0)
        for lst, val in zip(outs[0:4], (kw, vw, wk, sh)):
            lst.append(val)
        xs, kw, vw, sh, wk = _layer(xs, p_sample[l], wts, alpha, state_shift[l], state_wkv[l],
                                    cache_k_win[l], cache_v_win[l])
        for lst, val in zip(outs[4:8], (kw, vw, wk, sh)):
            lst.append(val)
    return (xp, xs) + tuple(jnp.stack(o) for o in outs)
```

```python
import functools

import jax
import jax.numpy as jnp
from jax import lax
from jax.experimental import pallas as pl
from jax.experimental.pallas import tpu as pltpu

BF = jnp.bfloat16
F32 = jnp.float32

D_MODEL = 1024
HEAD_DIM = 64
N_HEADS = 8
C_HEADS = N_HEADS * HEAD_DIM
PAIR = 2 * HEAD_DIM
N_PAIRS = N_HEADS // 2
DILATIONS = ((128, 1), (512, 4), (2048, 16))
WINDOW_MAX = 2048
Q_BLOCK = 128
DIL_MAX = max(d for _, d in DILATIONS)
DIL_STEP = 4
D_LORA_WA = 128
D_G_LORA = 128
D_B_IN = 3 * C_HEADS + D_LORA_WA + D_G_LORA
D_IN = 3 * C_HEADS + D_B_IN
D_FF = 2816
PLE_DIM = 256
LN_EPS = 1e-5
GN_EPS = 64e-5
NEG_BIG = -1e30

CHUNK = 16
SUPER = 64
N_CHUNKS = SUPER // CHUNK
RWKV_BLOCKS = 8
RWKV_GROUP = 2
FF_CHUNK = 256
POST_PARTS = 2
SEG_W = 256
ATTN_UNROLL = 4
IN_CHUNK = 256
VMEM_LIMIT_BYTES = 56 * 1024 * 1024

(_V_MU_R, _V_MU_K, _V_MU_V, _V_W0, _V_A0, _V_KK, _V_KA, _V_RK, _V_LNW, _V_LNB) = range(10)
_V_ROWS = 16


def _params(*sem):
    return pltpu.CompilerParams(dimension_semantics=sem, vmem_limit_bytes=VMEM_LIMIT_BYTES)


def _const_spec(shape):
    nd = len(shape)
    return pl.BlockSpec(shape, lambda *_: (0,) * nd, pipeline_mode=pl.Buffered(1))


def _dot(a, b):
    return jnp.dot(a.astype(BF), b.astype(BF), preferred_element_type=F32)


def _dot_nt(a, b):
    return lax.dot_general(a.astype(BF), b.astype(BF), (((1,), (1,)), ((), ())),
                           preferred_element_type=F32)


def _split(x):
    hi = x.astype(BF)
    lo = (x - hi.astype(F32)).astype(BF)
    return hi, lo


def _dot_exact_lhs(mask_bf, x):
    hi, lo = _split(x)
    return (jnp.dot(mask_bf, hi, preferred_element_type=F32)
            + jnp.dot(mask_bf, lo, preferred_element_type=F32))


def _dot_exact_rhs(x, mask_bf):
    hi, lo = _split(x)
    return (jnp.dot(hi, mask_bf, preferred_element_type=F32)
            + jnp.dot(lo, mask_bf, preferred_element_type=F32))


def _layer_norm(x, w, b):
    mu = jnp.mean(x, axis=-1, keepdims=True)
    xc = x - mu
    var = jnp.mean(xc * xc, axis=-1, keepdims=True)
    return xc * lax.rsqrt(var + LN_EPS) * w + b


def _in_proj_kernel(x_ref, w_ref, o_ref):
    xb = x_ref[...].astype(BF)
    for c in range(D_IN // IN_CHUNK):
        o_ref[:, c * IN_CHUNK:(c + 1) * IN_CHUNK] = jnp.dot(xb, w_ref[c],
                                                            preferred_element_type=F32)


def _in_proj(x2d, w_bf):
    m = x2d.shape[0]
    tm = min(1024, m)
    assert m % tm == 0
    return pl.pallas_call(
        _in_proj_kernel,
        out_shape=jax.ShapeDtypeStruct((m, D_IN), F32),
        grid=(m // tm,),
        in_specs=[pl.BlockSpec((tm, D_MODEL), lambda i: (i, 0)),
                  _const_spec((D_IN // IN_CHUNK, D_MODEL, IN_CHUNK))],
        out_specs=pl.BlockSpec((tm, D_IN), lambda i: (i, 0)),
        compiler_params=_params("arbitrary"),
        name="in_proj",
    )(x2d, w_bf)


def _post_kernel(att_ref, rw_ref, x_ref, p_ref, wo_ref, wg_ref, wu_ref, wd_ref, wpg_ref,
                 wpp_ref, ln_ref, o_ref, *, alpha):
    part = x_ref.shape[0] // POST_PARTS
    parts = [slice(i * part, (i + 1) * part) for i in range(POST_PARTS)]
    mix = [_dot(att_ref[r, :], wo_ref[0:C_HEADS, :])
           + _dot(rw_ref[r, :], wo_ref[C_HEADS:2 * C_HEADS, :]) for r in parts]
    h = jnp.concatenate([_layer_norm(alpha * x_ref[r, :] + m, ln_ref[0:1, :], ln_ref[1:2, :])
                         for r, m in zip(parts, mix)], axis=0)
    hb = h.astype(BF)
    ffn = jnp.zeros_like(h)
    for c in range(D_FF // FF_CHUNK):
        g = jnp.dot(hb, wg_ref[c], preferred_element_type=F32)
        u = jnp.dot(hb, wu_ref[c], preferred_element_type=F32)
        act = g * jax.nn.sigmoid(g) * u
        ffn = ffn + jnp.dot(act.astype(BF), wd_ref[c * FF_CHUNK:(c + 1) * FF_CHUNK, :],
                            preferred_element_type=F32)
    emb = _dot(p_ref[...], wpp_ref[...])
    h2 = [_layer_norm(alpha * h[r, :] + ffn[r, :], ln_ref[2:3, :], ln_ref[3:4, :]) for r in parts]
    ple = [jax.nn.sigmoid(_dot(hh, wpg_ref[...])) * emb[r, :] for r, hh in zip(parts, h2)]
    for r, hh, e in zip(parts, h2, ple):
        o_ref[r, :] = _layer_norm(alpha * hh + e, ln_ref[4:5, :], ln_ref[5:6, :])


def _post(att, rw, x2d, p2d, wts, alpha):
    m = x2d.shape[0]
    tm = min(512, m)
    assert m % tm == 0
    row = lambda w: pl.BlockSpec((tm, w), lambda i: (i, 0))
    return pl.pallas_call(
        functools.partial(_post_kernel, alpha=alpha),
        out_shape=jax.ShapeDtypeStruct((m, D_MODEL), F32),
        grid=(m // tm,),
        in_specs=[row(C_HEADS), row(C_HEADS), row(D_MODEL), row(PLE_DIM),
                  _const_spec((2 * C_HEADS, D_MODEL)),
                  _const_spec((D_FF // FF_CHUNK, D_MODEL, FF_CHUNK)),
                  _const_spec((D_FF // FF_CHUNK, D_MODEL, FF_CHUNK)),
                  _const_spec((D_FF, D_MODEL)),
                  _const_spec((D_MODEL, D_MODEL)), _const_spec((PLE_DIM, D_MODEL)),
                  _const_spec((8, D_MODEL))],
        out_specs=row(D_MODEL),
        compiler_params=_params("arbitrary"),
        name="post",
    )(att, rw, x2d, p2d, wts["w_out"], wts["w_gate"], wts["w_up"], wts["w_down"],
      wts["w_ple_gate"], wts["w_ple_proj"], wts["ln"])


def _attn_prompt_kernel(sl_ref, q_ref, k_ref, v_ref, o_ref, kw_ref, vw_ref, q_scr, k_scr, v_scr,
                        a_scr, m_scr, l_scr, mb_scr, *, seq, keep):
    pair = pl.program_id(1)
    slopes = (sl_ref[2 * pair], sl_ref[2 * pair + 1])
    lane = lax.broadcasted_iota(jnp.int32, (1, PAIR), 1)
    head0 = lane < HEAD_DIM
    kw_ref[...] = k_ref[seq - keep:seq, :].T
    vw_ref[...] = v_ref[seq - keep:seq, :].T
    qi = lax.broadcasted_iota(jnp.int32, (Q_BLOCK, 1), 0)
    ki = lax.broadcasted_iota(jnp.int32, (1, 2 * Q_BLOCK), 1)

    per = seq // DIL_MAX
    quarter = seq // DIL_STEP

    def to_residue_major(src_ref, dst_ref, convert):
        for rho in range(DIL_STEP):
            a_scr[rho * quarter:(rho + 1) * quarter, :] = src_ref[
                pl.ds(rho, quarter, stride=DIL_STEP), :]
        for rho in range(DIL_STEP):
            for j in range(DIL_MAX // DIL_STEP):
                r = DIL_STEP * j + rho
                dst_ref[r * per:(r + 1) * per, :] = convert(
                    a_scr[pl.ds(rho * quarter + j, per, stride=DIL_STEP), :])

    to_residue_major(q_ref, q_scr, lambda x: x * (HEAD_DIM ** -0.5))
    to_residue_major(k_ref, k_scr, lambda x: x.astype(BF))
    to_residue_major(v_ref, v_scr, lambda x: x.astype(BF))

    def gather(ref, runs):
        return jnp.concatenate([ref[rr, :] for rr in runs], axis=0)

    for window, dil in DILATIONS:
        n_runs = DIL_MAX // dil
        w = Q_BLOCK // n_runs
        nb = seq // dil // Q_BLOCK
        n_units = nb * dil
        assert nb >= 2 and n_units * Q_BLOCK == seq and n_units % ATTN_UNROLL == 0 and w % 8 == 0
        tq = (qi & (w - 1)) * DIL_MAX + (qi >> (w.bit_length() - 1)) * dil
        tk = (ki & (2 * w - 1)) * DIL_MAX + (ki >> (w.bit_length())) * dil
        if dil == 1:
            tk = ki
        for case in range(2):
            dist = case * Q_BLOCK * dil + tq - tk
            valid = (dist >= 0) & (dist <= window)
            distf = dist.astype(F32)
            mb_scr[case] = jnp.concatenate(
                [jnp.where(valid, -slopes[e] * distf, NEG_BIG) for e in range(2)], axis=0)

        def group(g, carry, dil=dil, n_runs=n_runs, w=w, first=(window, dil) == DILATIONS[0]):
            loaded = []
            for j in range(ATTN_UNROLL):
                u = g * ATTN_UNROLL + j
                n = u >> (dil.bit_length() - 1)
                rho = u & (dil - 1)
                nprev = jnp.maximum(n - 1, 0)
                base = [(dil * jj + rho) * per for jj in range(n_runs)]
                qruns = [pl.ds(pl.multiple_of(b + w * n, 8), w) for b in base]
                q = gather(q_scr, qruns)
                qq = jnp.concatenate([jnp.where(head0, q, 0.0), jnp.where(head0, 0.0, q)], axis=0)
                if dil == 1:
                    keys = pl.ds(pl.multiple_of(nprev * Q_BLOCK, Q_BLOCK), 2 * Q_BLOCK)
                    k = k_ref[keys, :].astype(BF)
                    v = v_ref[keys, :].astype(BF)
                else:
                    kruns = [pl.ds(pl.multiple_of(b + w * nprev, 16), 2 * w) for b in base]
                    k = gather(k_scr, kruns)
                    v = gather(v_scr, kruns)
                old = None if first else (
                    jnp.concatenate([gather(m_scr.at[0], qruns), gather(m_scr.at[1], qruns)], axis=0),
                    jnp.concatenate([gather(l_scr.at[0], qruns), gather(l_scr.at[1], qruns)], axis=0),
                    gather(a_scr, qruns))
                loaded.append((qruns, jnp.minimum(n, 1), qq.astype(BF), k,
                               jnp.concatenate([v, jnp.ones_like(v)], axis=1),
                               old))
            scores = [_dot_nt(qq, k) + mb_scr[case] for (_, case, qq, k, _, _) in loaded]
            stats = []
            for (_, _, _, _, _, old), s in zip(loaded, scores):
                m_new = jnp.max(s, axis=1, keepdims=True)
                m_new = (jnp.broadcast_to(m_new, (2 * Q_BLOCK, PAIR)) if first
                         else jnp.maximum(old[0], m_new))
                p = jnp.exp(s - jnp.concatenate([m_new, m_new], axis=1)).astype(BF)
                stats.append((m_new, p, None if first else jnp.exp(old[0] - m_new)))
            pvs = [jnp.dot(p, vo, preferred_element_type=F32)
                   for (_, _, _, _, vo, _), (_, p, _) in zip(loaded, stats)]
            for (qruns, _, _, _, _, old), (m_new, _, al), pv in zip(loaded, stats, pvs):
                l_new = pv[:, PAIR:2 * PAIR]
                a_new = (pv[0:Q_BLOCK, 0:PAIR], pv[Q_BLOCK:, 0:PAIR])
                if not first:
                    l_new = al * old[1] + l_new
                    a_new = (al[0:Q_BLOCK, :] * old[2] + a_new[0], al[Q_BLOCK:, :] * old[2] + a_new[1])
                a_new = jnp.where(head0, a_new[0], a_new[1])
                for jj, rr in enumerate(qruns):
                    a_scr[rr, :] = a_new[jj * w:(jj + 1) * w, :]
                    for e in range(2):
                        m_scr[e, rr, :] = m_new[e * Q_BLOCK + jj * w:e * Q_BLOCK + (jj + 1) * w, :]
                        l_scr[e, rr, :] = l_new[e * Q_BLOCK + jj * w:e * Q_BLOCK + (jj + 1) * w, :]
            return carry

        lax.fori_loop(0, n_units // ATTN_UNROLL, group, 0)

    for rho in range(DIL_STEP):
        for j in range(DIL_MAX // DIL_STEP):
            src = slice((DIL_STEP * j + rho) * per, (DIL_STEP * j + rho + 1) * per)
            q_scr[pl.ds(rho * quarter + j, per, stride=DIL_STEP), :] = a_scr[src, :] / jnp.where(
                head0, l_scr[0, src, :], l_scr[1, src, :])
    for rho in range(DIL_STEP):
        o_ref[pl.ds(rho, quarter, stride=DIL_STEP), :] = q_scr[rho * quarter:(rho + 1) * quarter, :]


def _attn_prompt(z3, slopes, keep):
    b, seq, _ = z3.shape
    win = pl.BlockSpec((None, PAIR, keep), lambda i, p: (i, p, 0))
    col = lambda off: pl.BlockSpec((None, seq, PAIR), lambda i, p: (i, 0, off + p))
    return pl.pallas_call(
        functools.partial(_attn_prompt_kernel, seq=seq, keep=keep),
        out_shape=(jax.ShapeDtypeStruct((b, seq, C_HEADS), F32),
                   jax.ShapeDtypeStruct((b, C_HEADS, keep), F32),
                   jax.ShapeDtypeStruct((b, C_HEADS, keep), F32)),
        grid=(b, N_PAIRS),
        in_specs=[pl.BlockSpec(memory_space=pltpu.SMEM),
                  col(0), col(N_PAIRS), col(2 * N_PAIRS)],
        out_specs=(pl.BlockSpec((None, seq, PAIR), lambda i, p: (i, 0, p)), win, win),
        scratch_shapes=[pltpu.VMEM((seq, PAIR), F32), pltpu.VMEM((seq, PAIR), BF),
                        pltpu.VMEM((seq, PAIR), BF), pltpu.VMEM((seq, PAIR), F32)]
        + [pltpu.VMEM((2, seq, PAIR), F32)] * 2
        + [pltpu.VMEM((2, 2 * Q_BLOCK, 2 * Q_BLOCK), F32)],
        compiler_params=_params("arbitrary", "arbitrary"),
        name="attn_prompt",
    )(slopes, z3, z3, z3)


def _attn_sample_kernel(sl_ref, q_ref, kn_ref, vn_ref, ck_ref, cv_ref, o_ref, ko_ref, vo_ref,
                        *, s_new, cache_len):
    lane_t = lax.broadcasted_iota(jnp.int32, (1, PAIR), 1)

    def shift_in(c_ref, new_ref, out_ref):
        rolled = pltpu.roll(c_ref[...], cache_len - s_new, 1)
        new_t = jnp.concatenate([new_ref[...], jnp.zeros((PAIR - s_new, C_HEADS), F32)], axis=0).T
        tail = jnp.where(lane_t >= PAIR - s_new, pltpu.roll(new_t, PAIR - s_new, 1),
                         rolled[:, cache_len - PAIR:cache_len])
        out_ref[:, 0:cache_len - PAIR] = rolled[:, 0:cache_len - PAIR]
        out_ref[:, cache_len - PAIR:cache_len] = tail

    shift_in(ck_ref, kn_ref, ko_ref)
    shift_in(cv_ref, vn_ref, vo_ref)

    rows = s_new * N_HEADS
    ri = lax.broadcasted_iota(jnp.int32, (rows, 1), 0)
    lane = lax.broadcasted_iota(jnp.int32, (1, C_HEADS), 1)
    own = (ri & (N_HEADS - 1)) == (lane >> 6)
    q = q_ref[...]
    qe = jnp.broadcast_to(q[:, None, :], (s_new, N_HEADS, C_HEADS)).reshape(rows, C_HEADS)
    qe = jnp.where(own, qe, 0.0)
    slope = sl_ref[:, 0:1]
    spos = ri >> 3

    def weights(dist):
        mult = jnp.zeros(dist.shape, F32)
        for window, dil in DILATIONS:
            hit = (dist >= 0) & (dist <= window) & ((dist & (dil - 1)) == 0)
            mult = mult + hit.astype(F32)
        return mult

    dist_c = cache_len + spos - lax.broadcasted_iota(jnp.int32, (1, cache_len), 1)
    dist_n = spos - lax.broadcasted_iota(jnp.int32, (1, s_new), 1)
    mult_c = weights(dist_c)
    mult_n = weights(dist_n)
    sc = _dot(qe, ck_ref[...]) * (HEAD_DIM ** -0.5) - slope * dist_c.astype(F32)
    sn = _dot_nt(qe, kn_ref[...]) * (HEAD_DIM ** -0.5) - slope * dist_n.astype(F32)
    sc = jnp.where(mult_c > 0, sc, NEG_BIG)
    sn = jnp.where(mult_n > 0, sn, NEG_BIG)
    mx = jnp.maximum(jnp.max(sc, axis=1, keepdims=True), jnp.max(sn, axis=1, keepdims=True))
    pc = mult_c * jnp.exp(sc - mx)
    pn = mult_n * jnp.exp(sn - mx)
    den = jnp.sum(pc, axis=1, keepdims=True) + jnp.sum(pn, axis=1, keepdims=True)
    num = _dot_nt(pc, cv_ref[...]) + _dot(pn, vn_ref[...])
    num = jnp.where(own, num, 0.0).reshape(s_new, N_HEADS, C_HEADS).sum(axis=1)
    den = jnp.where(own, den, 0.0).reshape(s_new, N_HEADS, C_HEADS).sum(axis=1)
    o_ref[...] = num / den


def _attn_sample(z3, cache_k, cache_v, slopes):
    b, s_new, _ = z3.shape
    cache_len = cache_k.shape[2]
    assert cache_len % PAIR == 0 and cache_len > PAIR and s_new < PAIR
    new = lambda c: pl.BlockSpec((None, s_new, C_HEADS), lambda i: (i, 0, c))
    cache = pl.BlockSpec((None, C_HEADS, cache_len), lambda i: (i, 0, 0))
    slope_rows = jnp.broadcast_to(jnp.tile(slopes, s_new)[:, None], (s_new * N_HEADS, PAIR))
    return pl.pallas_call(
        functools.partial(_attn_sample_kernel, s_new=s_new, cache_len=cache_len),
        out_shape=(jax.ShapeDtypeStruct((b, s_new, C_HEADS), F32),
                   jax.ShapeDtypeStruct(cache_k.shape, F32),
                   jax.ShapeDtypeStruct(cache_v.shape, F32)),
        grid=(b,),
        in_specs=[_const_spec((s_new * N_HEADS, PAIR)), new(0), new(1), new(2), cache, cache],
        out_specs=(new(0), cache, cache),
        compiler_params=_params("arbitrary"),
        name="attn_sample",
    )(slope_rows, z3, z3, z3, cache_k, cache_v)


def _rwkv_kernel(r_ref, k_ref, v_ref, wag_ref, pr_ref, pk_ref, pv_ref, pwag_ref,
                 sr_ref, sk_ref, sv_ref, swag_ref, m0_ref, vec_ref, muwag_ref, lora_ref, seg_ref,
                 o_ref, mout_ref, m_scr, y_scr, *, t_valid, nsb, chunk, per_chunk_state):
    tb = pl.program_id(1)
    first = tb == 0
    n_rows = nsb * SUPER
    shift = chunk.bit_length() - 1
    n_chunks = SUPER // chunk

    lane = lax.broadcasted_iota(jnp.int32, (1, PAIR), 1)
    head0 = lane < HEAD_DIM

    def stack(x):
        return jnp.concatenate([jnp.where(head0, x, 0.0), jnp.where(head0, 0.0, x)], axis=0)

    def transposed(x):
        n = x.shape[1]
        eye_n = (lax.broadcasted_iota(jnp.int32, (n, n), 0)
                 == lax.broadcasted_iota(jnp.int32, (n, n), 1)).astype(BF)
        hi, lo = _split(x)
        return _dot_nt(eye_n, hi) + _dot_nt(eye_n, lo)

    def states_in(ref):
        return transposed(ref[...].reshape(N_HEADS * HEAD_DIM, HEAD_DIM))

    def pair_state(t, p):
        return stack(t[:, p * PAIR:(p + 1) * PAIR])

    def pair_out(m_pair):
        return transposed(m_pair[0:HEAD_DIM, :] + m_pair[HEAD_DIM:PAIR, :])

    if not per_chunk_state:
        @pl.when(first)
        def _():
            t_in = states_in(m0_ref.at[0])
            for p in range(N_PAIRS):
                m_scr[p] = pair_state(t_in, p)

    rows = lax.broadcasted_iota(jnp.int32, (n_rows, 1), 0)

    def token_shift(cur_ref, prev_ref, carry_ref, mu):
        cur = cur_ref[...]
        if per_chunk_state:
            prev = prev_ref[...]
        else:
            last = jnp.where(first, carry_ref[...], prev_ref[7:8, :])
            prev = jnp.where(rows == 0, last, pltpu.roll(cur, 1, 0))
        return cur + (prev - cur) * mu

    vec = vec_ref[...]
    row = lambda i: vec[i:i + 1, :]
    zr = token_shift(r_ref, pr_ref, sr_ref, row(_V_MU_R))
    zk = token_shift(k_ref, pk_ref, sk_ref, row(_V_MU_K))
    zv = token_shift(v_ref, pv_ref, sv_ref, row(_V_MU_V))
    zwag = token_shift(wag_ref, pwag_ref, swag_ref, muwag_ref[...])
    wa = zwag[:, 0:D_LORA_WA]
    gi = zwag[:, D_LORA_WA:D_LORA_WA + D_G_LORA]

    wlin = row(_V_W0) + _dot(jnp.tanh(wa), lora_ref[0])
    softplus = jnp.maximum(-wlin, 0.0) + jnp.log(1.0 + jnp.exp(-jnp.abs(wlin)))
    w_log = -softplus - 0.5
    ld = -jnp.exp(w_log)
    lr = jax.nn.sigmoid(row(_V_A0) + _dot(wa, lora_ref[1]))
    gate = _dot(jax.nn.sigmoid(gi), lora_ref[2])
    seg = seg_ref[...]

    def head_sum(x):
        return jnp.concatenate([_dot_exact_rhs(x[:, i:i + SEG_W], seg)
                                for i in range(0, C_HEADS, SEG_W)], axis=1)

    kk = zk * row(_V_KK)
    kk = kk / jnp.maximum(jnp.sqrt(head_sum(kk * kk)), 1e-12)
    kmod = zk * (1.0 + (lr - 1.0) * row(_V_KA))
    vv = zv
    if t_valid is not None:
        live = ((rows & (chunk - 1)) if per_chunk_state else (rows + tb * n_rows)) < t_valid
        ld = jnp.where(live, ld, 0.0)
        kk = jnp.where(live, kk, 0.0)
        kmod = jnp.where(live, kmod, 0.0)
        vv = jnp.where(live, vv, 0.0)

    span = min(PAIR, n_rows)
    ti = lax.broadcasted_iota(jnp.int32, (span, span), 0)
    tj = lax.broadcasted_iota(jnp.int32, (span, span), 1)
    same_chunk = (ti >> shift) == (tj >> shift)
    sums = jnp.concatenate([same_chunk & (ti >= tj), same_chunk], axis=0).astype(BF)
    cum_tot = [_dot_exact_lhs(sums, ld[i:i + span, :]) for i in range(0, n_rows, span)]
    cum = jnp.concatenate([x[0:span, :] for x in cum_tot], axis=0)
    tot = jnp.concatenate([x[span:2 * span, :] for x in cum_tot], axis=0)
    dec_in = jnp.exp(cum)
    dec_ex = jnp.exp(cum - ld)
    dec_inv = jnp.exp(-cum)
    dec_end = jnp.exp(tot - cum)
    dec_tot = jnp.exp(tot)
    beta = kk * lr
    abar = -(kk * dec_ex)
    rbar = zr * dec_in
    bt = beta * dec_inv
    kt = kmod * dec_inv
    bh = beta * dec_end
    kh = kmod * dec_end

    ri = lax.broadcasted_iota(jnp.int32, (PAIR, PAIR), 0)
    ci = lax.broadcasted_iota(jnp.int32, (PAIR, PAIR), 1)
    same_blk = (ri >> shift) == (ci >> shift)
    strict = same_blk & (ri > ci)
    incl = same_blk & (ri >= ci)
    same_head = (ri >> 6) == (ci >> 6)
    eye = ri == ci
    zeros_sp = jnp.zeros((SUPER, PAIR), F32)
    zeros_pp = jnp.zeros((PAIR, PAIR), F32)

    def unstack(x):
        return x[0:SUPER, :] + x[SUPER:PAIR, :]

    def both_heads(x, keep):
        return jnp.where(keep, jnp.concatenate([x, x], axis=0), 0.0)

    eye_f = jnp.where(eye, 1.0, 0.0)
    in_chunk = [((lane & (SUPER - 1)) >> shift) == c for c in range(n_chunks)]

    def phase1(blocks, out):
        units = [(s, p) for s in blocks for p in range(N_PAIRS)]
        tile = lambda x, u: x[u[0] * SUPER:(u[0] + 1) * SUPER, u[1] * PAIR:(u[1] + 1) * PAIR]
        ab = [tile(abar, u) for u in units]
        rb = [tile(rbar, u) for u in units]
        v_p = [tile(vv, u) for u in units]
        v_s = [stack(x) for x in v_p]
        a_all = [_dot_nt(jnp.concatenate([a, r], axis=0),
                         jnp.concatenate([stack(tile(bt, u)), stack(tile(kt, u))], axis=0))
                 for a, r, u in zip(ab, rb, units)]
        yield
        n_ab = [both_heads(a[0:SUPER, 0:PAIR], strict) for a in a_all]
        a_ak = [both_heads(a[0:SUPER, PAIR:2 * PAIR], strict) for a in a_all]
        a_rbk = [jnp.concatenate([both_heads(a[SUPER:PAIR, 0:PAIR], incl),
                                  both_heads(a[SUPER:PAIR, PAIR:2 * PAIR], incl)], axis=1)
                 for a in a_all]
        tinv = [eye_f + n for n in n_ab]
        power = n_ab
        for _ in range(shift - 1):
            power = [_dot(x, x) for x in power]
            yield
            tinv = [t + _dot(x, t) for x, t in zip(power, tinv)]
            yield
        u_s = [_dot(a, v) for a, v in zip(a_ak, v_s)]
        yield
        ta = [_dot(t, jnp.concatenate([stack(a), u], axis=1))
              for t, a, u in zip(tinv, ab, u_s)]
        yield
        ry = [_dot(a, jnp.concatenate([t, jnp.concatenate([zeros_pp, v], axis=1)], axis=0))
              for a, t, v in zip(a_rbk, ta, v_s)]
        yield
        r1 = [unstack(stack(r) + y[:, 0:PAIR]) for r, y in zip(rb, ry)]
        y0 = [unstack(y[:, PAIR:2 * PAIR]) for y in ry]
        bk_t = [jnp.concatenate([tile(bh, u), tile(kh, u)], axis=0).astype(BF).T
                for u in units]
        rhs3 = [jnp.concatenate([jnp.concatenate([unstack(t[:, 0:PAIR]),
                                                  unstack(t[:, PAIR:2 * PAIR])], axis=1),
                                 jnp.concatenate([zeros_sp, v], axis=1)], axis=0)
                for t, v in zip(ta, v_p)]
        gh = [_dot(jnp.concatenate([jnp.where(in_chunk[c], b, jnp.zeros_like(b))
                                    for c in range(n_chunks)], axis=0), r)
              for b, r in zip(bk_t, rhs3)]
        for i, u in enumerate(units):
            out[u] = (r1[i], y0[i], gh[i])
        yield

    m = None if per_chunk_state else [m_scr[p] for p in range(N_PAIRS)]

    def phase2(blocks, fac):
        for s in blocks:
            for c in range(n_chunks):
                tok = slice(c * chunk, (c + 1) * chunk)
                r0 = s * SUPER + c * chunk
                seq_i = s * n_chunks + c
                t_in = states_in(m0_ref.at[seq_i]) if per_chunk_state else None
                for p in range(N_PAIRS):
                    r1, y0, gh = fac[(s, p)]
                    sl = slice(p * PAIR, (p + 1) * PAIR)
                    gh_c = gh[c * PAIR:(c + 1) * PAIR, :]
                    g_c = (jnp.where(eye, dec_tot[r0:r0 + 1, sl], 0.0)
                           + jnp.where(same_head, gh_c[:, 0:PAIR], 0.0))
                    h_c = jnp.where(same_head, gh_c[:, PAIR:2 * PAIR], 0.0)
                    m_in = pair_state(t_in, p) if per_chunk_state else m[p]
                    y_scr[r0:r0 + chunk, sl] = _dot(r1[tok, :], m_in) + y0[tok, :]
                    m_out = _dot(g_c, m_in) + h_c
                    if per_chunk_state:
                        mout_ref[seq_i, 2 * p:2 * p + 2] = pair_out(m_out).reshape(
                            2, HEAD_DIM, HEAD_DIM)
                    else:
                        m[p] = m_out
                yield

    groups = [list(range(g, min(g + RWKV_GROUP, nsb))) for g in range(0, nsb, RWKV_GROUP)]
    fac = {}
    for _ in phase1(groups[0], fac):
        pass
    for done, nxt in zip(groups, groups[1:]):
        steps = phase2(done, fac)
        for _ in phase1(nxt, fac):
            next(steps, None)
        for _ in steps:
            pass
    for _ in phase2(groups[-1], fac):
        pass
    if not per_chunk_state:
        for p in range(N_PAIRS):
            m_scr[p] = m[p]

    y = y_scr[...]
    mean = head_sum(y) * (1.0 / HEAD_DIM)
    yc = y - mean
    var = head_sum(yc * yc) * (1.0 / HEAD_DIM)
    yn = yc * lax.rsqrt(var + GN_EPS) * row(_V_LNW) + row(_V_LNB)
    bonus = head_sum(zr * kmod * row(_V_RK)) * zv
    o_ref[...] = (yn + bonus) * gate

    if not per_chunk_state:
        @pl.when(tb == pl.num_programs(1) - 1)
        def _():
            for p in range(N_PAIRS):
                mout_ref[0, 2 * p:2 * p + 2] = pair_out(m_scr[p]).reshape(2, HEAD_DIM, HEAD_DIM)


def _rwkv(z3, prev3, shift_prev, m0, wts, t_valid, nsb, chunk, per_chunk_state):
    b, seq, _ = z3.shape
    n_rows = nsb * SUPER
    assert seq % n_rows == 0 and SUPER % chunk == 0 and chunk % 8 == 0
    nt = seq // n_rows
    n_state = n_rows // chunk if per_chunk_state else 1
    assert m0.shape[0] == b * n_state and (nt == 1 or not per_chunk_state)
    cur = lambda w, c: pl.BlockSpec((None, n_rows, w), lambda i, t: (i, t, c))
    if per_chunk_state:
        prev = cur
    else:
        prev = lambda w, c: pl.BlockSpec(
            (None, 8, w), lambda i, t: (i, jnp.maximum(t * (n_rows // 8) - 1, 0), c))
    carry = lambda w: pl.BlockSpec((None, 1, w), lambda i, t: (i, 0, 0))
    state = pl.BlockSpec((n_state, N_HEADS, HEAD_DIM, HEAD_DIM), lambda i, t: (i, 0, 0, 0))
    wag_w = D_LORA_WA + D_G_LORA
    sp = shift_prev[:, None, :]
    all_valid = t_valid == (chunk if per_chunk_state else seq)
    return pl.pallas_call(
        functools.partial(_rwkv_kernel, t_valid=None if all_valid else t_valid, nsb=nsb,
                          chunk=chunk, per_chunk_state=per_chunk_state),
        out_shape=(jax.ShapeDtypeStruct((b, seq, C_HEADS), F32),
                   jax.ShapeDtypeStruct(m0.shape, F32)),
        grid=(b, nt),
        in_specs=[cur(C_HEADS, 3), cur(C_HEADS, 4), cur(C_HEADS, 5), cur(wag_w, 12),
                  prev(C_HEADS, 3), prev(C_HEADS, 4), prev(C_HEADS, 5), prev(wag_w, 12),
                  carry(C_HEADS), carry(C_HEADS), carry(C_HEADS), carry(wag_w),
                  state,
                  _const_spec((_V_ROWS, C_HEADS)), _const_spec((1, wag_w)),
                  _const_spec((3, PAIR, C_HEADS)), _const_spec((SEG_W, SEG_W))],
        out_specs=(pl.BlockSpec((None, n_rows, C_HEADS), lambda i, t: (i, t, 0)), state),
        scratch_shapes=[pltpu.VMEM((N_PAIRS, PAIR, PAIR), F32),
                        pltpu.VMEM((n_rows, C_HEADS), F32)],
        compiler_params=_params("arbitrary", "arbitrary"),
        name="rwkv",
    )(z3, z3, z3, z3, prev3, prev3, prev3, prev3,
      sp[:, :, 0:C_HEADS], sp[:, :, C_HEADS:2 * C_HEADS], sp[:, :, 2 * C_HEADS:3 * C_HEADS],
      sp[:, :, 3 * C_HEADS:], m0, wts["vec"], wts["mu_wag"], wts["lora"], wts["seg"])


def _pack_layer(w_in, mu_shift, w0, w2, a0, a2, g2, k_k, k_a, r_k, lnx_w, lnx_b, w_out, ln1_w,
                ln1_b, w_gate, w_up, w_down, ln2_w, ln2_b, w_ple_gate, w_ple_proj, ln3_w, ln3_b):
    mu_r, mu_k, mu_v = (mu_shift[i * C_HEADS:(i + 1) * C_HEADS] for i in range(3))
    vec = jnp.stack([mu_r, mu_k, mu_v, w0, a0, k_k, k_a, r_k.reshape(-1), lnx_w, lnx_b])
    vec = jnp.concatenate([vec, jnp.zeros((_V_ROWS - vec.shape[0], C_HEADS), F32)], axis=0)
    half = D_LORA_WA // 2
    zeros = jnp.zeros((half, C_HEADS), F32)
    lora = jnp.stack([jnp.concatenate([w2, zeros], axis=0),
                      jnp.concatenate([zeros, a2], axis=0),
                      g2]).astype(BF)
    head_of = jnp.arange(SEG_W) // HEAD_DIM
    seg = (head_of[:, None] == head_of[None, :]).astype(BF)
    ln = jnp.stack([ln1_w, ln1_b, ln2_w, ln2_b, ln3_w, ln3_b,
                    jnp.zeros_like(ln1_w), jnp.zeros_like(ln1_w)])

    def col_chunks(w, width):
        k, n = w.shape
        return jnp.swapaxes(w.astype(BF).reshape(k, n // width, width), 0, 1)

    return {
        "w_in": col_chunks(w_in, IN_CHUNK), "vec": vec, "mu_wag": mu_shift[None, 3 * C_HEADS:],
        "lora": lora, "seg": seg, "w_out": w_out.astype(BF),
        "w_gate": col_chunks(w_gate, FF_CHUNK), "w_up": col_chunks(w_up, FF_CHUNK),
        "w_down": w_down.astype(BF),
        "w_ple_gate": w_ple_gate.astype(BF), "w_ple_proj": w_ple_proj.astype(BF), "ln": ln,
    }


def _alibi_slopes():
    h = jnp.arange(1, N_HEADS + 1, dtype=F32)
    return jnp.exp2(-8.0 * h / N_HEADS)


def _layer(x, p_l, wts, alpha, shift_prev, wkv_prev, cache_k=None, cache_v=None):
    b, seq, _ = x.shape
    x2d = x.reshape(b * seq, D_MODEL)
    z = _in_proj(x2d, wts["w_in"])
    z3 = z.reshape(b, seq, D_IN)
    slopes = _alibi_slopes()
    if cache_k is None:
        keep = min(WINDOW_MAX, seq)
        att, k_win, v_win = _attn_prompt(z3, slopes, keep)
        from_cm = lambda c: jnp.transpose(c.reshape(b, N_HEADS, HEAD_DIM, keep), (0, 3, 1, 2))
        k_win = from_cm(k_win)
        v_win = from_cm(v_win)
    else:
        cache_len = cache_k.shape[1]
        to_cm = lambda c: jnp.transpose(c, (0, 2, 3, 1)).reshape(b, C_HEADS, cache_len)
        from_cm = lambda c: jnp.transpose(c.reshape(b, N_HEADS, HEAD_DIM, cache_len), (0, 3, 1, 2))
        att, k_win, v_win = _attn_sample(z3, to_cm(cache_k), to_cm(cache_v), slopes)
        k_win = from_cm(k_win)
        v_win = from_cm(v_win)
    if seq % SUPER == 0:
        nsb = RWKV_BLOCKS if seq % (RWKV_BLOCKS * SUPER) == 0 else 1
        rw, m_last = _rwkv(z3, z3, shift_prev, wkv_prev, wts, seq, nsb, CHUNK,
                           False)
    else:
        chunk = 8
        per_block = SUPER // chunk
        assert seq <= chunk and b % per_block == 0
        first_prev = jnp.concatenate([jnp.zeros((b, 1, 3 * C_HEADS), F32), shift_prev[:, None, :]],
                                     axis=-1)
        prev = jnp.concatenate([first_prev, z3[:, :seq - 1]], axis=1)
        blocks = lambda a: jnp.pad(a, ((0, 0), (0, chunk - seq), (0, 0))).reshape(
            b // per_block, SUPER, D_IN)
        rw, m_last = _rwkv(blocks(z3), blocks(prev), jnp.zeros((b // per_block, D_B_IN), F32),
                           wkv_prev, wts, seq, 1, chunk, True)
        rw = rw.reshape(b, chunk, C_HEADS)[:, :seq]
    y = _post(att.reshape(b * seq, C_HEADS), rw.reshape(b * seq, C_HEADS), x2d,
              p_l.reshape(b * seq, PLE_DIM), wts, alpha)
    shift_new = z3[:, seq - 1, 3 * C_HEADS:]
    return y.reshape(b, seq, D_MODEL), k_win, v_win, shift_new, m_last


def kernel(x_prompt, x_sample, p_prompt, p_sample, cache_k_win, cache_v_win, state_wkv, state_shift, w_in, mu_shift, w0, w2, a0, a2, g2, k_k, k_a, r_k, lnx_w, lnx_b, w_out, ln1_w, ln1_b, w_gate, w_up, w_down, ln2_w, ln2_b, w_ple_gate, w_ple_proj, ln3_w, ln3_b):
    depth = w_in.shape[0]
    alpha = float((2 * depth) ** 0.25)
    xp, xs = x_prompt, x_sample
    bp = x_prompt.shape[0]
    shift0 = jnp.zeros((bp, D_B_IN), x_prompt.dtype)
    wkv0 = jnp.zeros((bp, N_HEADS, HEAD_DIM, HEAD_DIM), state_wkv.dtype)
    outs = [[] for _ in range(8)]
    for l in range(depth):
        wts = _pack_layer(w_in[l], mu_shift[l], w0[l], w2[l], a0[l], a2[l], g2[l], k_k[l], k_a[l],
                          r_k[l], lnx_w[l], lnx_b[l], w_out[l], ln1_w[l], ln1_b[l], w_gate[l],
                          w_up[l], w_down[l], ln2_w[l], ln2_b[l], w_ple_gate[l], w_ple_proj[l],
                          ln3_w[l], ln3_b[l])
        xp, kw, vw, sh, wk = _layer(xp, p_prompt[l], wts, alpha, shift0, wkv0)
        for lst, val in zip(outs[0:4], (kw, vw, wk, sh)):
            lst.append(val)
        xs, kw, vw, sh, wk = _layer(xs, p_sample[l], wts, alpha, state_shift[l], state_wkv[l],
                                    cache_k_win[l], cache_v_win[l])
        for lst, val in zip(outs[4:8], (kw, vw, wk, sh)):
            lst.append(val)
    return (xp, xs) + tuple(jnp.stack(o) for o in outs)
```

```python
import functools

import jax
import jax.numpy as jnp
from jax import lax
from jax.experimental import pallas as pl
from jax.experimental.pallas import tpu as pltpu

BF = jnp.bfloat16
F32 = jnp.float32

D_MODEL = 1024
HEAD_DIM = 64
N_HEADS = 8
C_HEADS = N_HEADS * HEAD_DIM
PAIR = 2 * HEAD_DIM
N_PAIRS = N_HEADS // 2
DILATIONS = ((128, 1), (512, 4), (2048, 16))
WINDOW_MAX = 2048
Q_BLOCK = 128
DIL_MAX = max(d for _, d in DILATIONS)
DIL_STEP = 4
D_LORA_WA = 128
D_G_LORA = 128
D_B_IN = 3 * C_HEADS + D_LORA_WA + D_G_LORA
D_IN = 3 * C_HEADS + D_B_IN
D_FF = 2816
PLE_DIM = 256
LN_EPS = 1e-5
GN_EPS = 64e-5
NEG_BIG = -1e30

CHUNK = 16
SUPER = 64
N_CHUNKS = SUPER // CHUNK
RWKV_BLOCKS = 8
RWKV_GROUP = 2
FF_CHUNK = 256
POST_PARTS = 2
SEG_W = 256
ATTN_UNROLL = 4
IN_CHUNK = 256
VMEM_LIMIT_BYTES = 56 * 1024 * 1024

(_V_MU_R, _V_MU_K, _V_MU_V, _V_W0, _V_A0, _V_KK, _V_KA, _V_RK, _V_LNW, _V_LNB) = range(10)
_V_ROWS = 16


def _params(*sem):
    return pltpu.CompilerParams(dimension_semantics=sem, vmem_limit_bytes=VMEM_LIMIT_BYTES)


def _const_spec(shape):
    nd = len(shape)
    return pl.BlockSpec(shape, lambda *_: (0,) * nd, pipeline_mode=pl.Buffered(1))


def _dot(a, b):
    return jnp.dot(a.astype(BF), b.astype(BF), preferred_element_type=F32)


def _dot_nt(a, b):
    return lax.dot_general(a.astype(BF), b.astype(BF), (((1,), (1,)), ((), ())),
                           preferred_element_type=F32)


def _split(x):
    hi = x.astype(BF)
    lo = (x - hi.astype(F32)).astype(BF)
    return hi, lo


def _dot_exact_lhs(mask_bf, x):
    hi, lo = _split(x)
    return (jnp.dot(mask_bf, hi, preferred_element_type=F32)
            + jnp.dot(mask_bf, lo, preferred_element_type=F32))


def _dot_exact_rhs(x, mask_bf):
    hi, lo = _split(x)
    return (jnp.dot(hi, mask_bf, preferred_element_type=F32)
            + jnp.dot(lo, mask_bf, preferred_element_type=F32))


def _layer_norm(x, w, b):
    mu = jnp.mean(x, axis=-1, keepdims=True)
    xc = x - mu
    var = jnp.mean(xc * xc, axis=-1, keepdims=True)
    return xc * lax.rsqrt(var + LN_EPS) * w + b


def _in_proj_kernel(x_ref, w_ref, o_ref):
    xb = x_ref[...].astype(BF)
    for c in range(D_IN // IN_CHUNK):
        o_ref[:, c * IN_CHUNK:(c + 1) * IN_CHUNK] = jnp.dot(xb, w_ref[:, c * IN_CHUNK:(c + 1) * IN_CHUNK],
                                                            preferred_element_type=F32)


def _in_proj(x2d, w_bf):
    m = x2d.shape[0]
    tm = min(1024, m)
    assert m % tm == 0
    return pl.pallas_call(
        _in_proj_kernel,
        out_shape=jax.ShapeDtypeStruct((m, D_IN), F32),
        grid=(m // tm,),
        in_specs=[pl.BlockSpec((tm, D_MODEL), lambda i: (i, 0)),
                  _const_spec((D_MODEL, D_IN))],
        out_specs=pl.BlockSpec((tm, D_IN), lambda i: (i, 0)),
        compiler_params=_params("arbitrary"),
        name="in_proj",
    )(x2d, w_bf)


def _post_kernel(att_ref, rw_ref, x_ref, p_ref, wo_ref, wg_ref, wu_ref, wd_ref, wpg_ref,
                 wpp_ref, ln_ref, o_ref, *, alpha):
    part = x_ref.shape[0] // POST_PARTS
    parts = [slice(i * part, (i + 1) * part) for i in range(POST_PARTS)]
    mix = [_dot(att_ref[r, :], wo_ref[0:C_HEADS, :])
           + _dot(rw_ref[r, :], wo_ref[C_HEADS:2 * C_HEADS, :]) for r in parts]
    h = jnp.concatenate([_layer_norm(alpha * x_ref[r, :] + m, ln_ref[0:1, :], ln_ref[1:2, :])
                         for r, m in zip(parts, mix)], axis=0)
    hb = h.astype(BF)
    ffn = jnp.zeros_like(h)
    for c in range(D_FF // FF_CHUNK):
        cols = slice(c * FF_CHUNK, (c + 1) * FF_CHUNK)
        g = jnp.dot(hb, wg_ref[:, cols], preferred_element_type=F32)
        u = jnp.dot(hb, wu_ref[:, cols], preferred_element_type=F32)
        act = g * jax.nn.sigmoid(g) * u
        ffn = ffn + jnp.dot(act.astype(BF), wd_ref[cols, :],
                            preferred_element_type=F32)
    emb = _dot(p_ref[...], wpp_ref[...])
    h2 = [_layer_norm(alpha * h[r, :] + ffn[r, :], ln_ref[2:3, :], ln_ref[3:4, :]) for r in parts]
    ple = [jax.nn.sigmoid(_dot(hh, wpg_ref[...])) * emb[r, :] for r, hh in zip(parts, h2)]
    for r, hh, e in zip(parts, h2, ple):
        o_ref[r, :] = _layer_norm(alpha * hh + e, ln_ref[4:5, :], ln_ref[5:6, :])


def _post(att, rw, x2d, p2d, wts, alpha):
    m = x2d.shape[0]
    tm = min(512, m)
    assert m % tm == 0
    row = lambda w: pl.BlockSpec((tm, w), lambda i: (i, 0))
    return pl.pallas_call(
        functools.partial(_post_kernel, alpha=alpha),
        out_shape=jax.ShapeDtypeStruct((m, D_MODEL), F32),
        grid=(m // tm,),
        in_specs=[row(C_HEADS), row(C_HEADS), row(D_MODEL), row(PLE_DIM),
                  _const_spec((2 * C_HEADS, D_MODEL)),
                  _const_spec((D_MODEL, D_FF)), _const_spec((D_MODEL, D_FF)),
                  _const_spec((D_FF, D_MODEL)),
                  _const_spec((D_MODEL, D_MODEL)), _const_spec((PLE_DIM, D_MODEL)),
                  _const_spec((8, D_MODEL))],
        out_specs=row(D_MODEL),
        compiler_params=_params("arbitrary"),
        name="post",
    )(att, rw, x2d, p2d, wts["w_out"], wts["w_gate"], wts["w_up"], wts["w_down"],
      wts["w_ple_gate"], wts["w_ple_proj"], wts["ln"])


def _attn_prompt_kernel(sl_ref, q_ref, k_ref, v_ref, o_ref, kw_ref, vw_ref, q_scr, k_scr, v_scr,
                        a_scr, m_scr, l_scr, mb_scr, *, seq, keep):
    pair = pl.program_id(1)
    slopes = (sl_ref[2 * pair], sl_ref[2 * pair + 1])
    lane = lax.broadcasted_iota(jnp.int32, (1, PAIR), 1)
    head0 = lane < HEAD_DIM
    kw_ref[...] = k_ref[seq - keep:seq, :].T
    vw_ref[...] = v_ref[seq - keep:seq, :].T
    qi = lax.broadcasted_iota(jnp.int32, (Q_BLOCK, 1), 0)
    ki = lax.broadcasted_iota(jnp.int32, (1, 2 * Q_BLOCK), 1)

    per = seq // DIL_MAX
    quarter = seq // DIL_STEP

    def to_residue_major(src_ref, dst_ref, convert):
        for rho in range(DIL_STEP):
            a_scr[rho * quarter:(rho + 1) * quarter, :] = src_ref[
                pl.ds(rho, quarter, stride=DIL_STEP), :]
        for rho in range(DIL_STEP):
            for j in range(DIL_MAX // DIL_STEP):
                r = DIL_STEP * j + rho
                dst_ref[r * per:(r + 1) * per, :] = convert(
                    a_scr[pl.ds(rho * quarter + j, per, stride=DIL_STEP), :])

    to_residue_major(q_ref, q_scr, lambda x: x * (HEAD_DIM ** -0.5))
    to_residue_major(k_ref, k_scr, lambda x: x.astype(BF))
    to_residue_major(v_ref, v_scr, lambda x: x.astype(BF))

    def gather(ref, runs):
        return jnp.concatenate([ref[rr, :] for rr in runs], axis=0)

    for window, dil in DILATIONS:
        n_runs = DIL_MAX // dil
        w = Q_BLOCK // n_runs
        nb = seq // dil // Q_BLOCK
        n_units = nb * dil
        assert nb >= 2 and n_units * Q_BLOCK == seq and n_units % ATTN_UNROLL == 0 and w % 8 == 0
        tq = (qi & (w - 1)) * DIL_MAX + (qi >> (w.bit_length() - 1)) * dil
        tk = (ki & (2 * w - 1)) * DIL_MAX + (ki >> (w.bit_length())) * dil
        if dil == 1:
            tk = ki
        for case in range(2):
            dist = case * Q_BLOCK * dil + tq - tk
            valid = (dist >= 0) & (dist <= window)
            distf = dist.astype(F32)
            mb_scr[case] = jnp.concatenate(
                [jnp.where(valid, -slopes[e] * distf, NEG_BIG) for e in range(2)], axis=0)

        def group(g, carry, dil=dil, n_runs=n_runs, w=w, first=(window, dil) == DILATIONS[0]):
            loaded = []
            for j in range(ATTN_UNROLL):
                u = g * ATTN_UNROLL + j
                n = u >> (dil.bit_length() - 1)
                rho = u & (dil - 1)
                nprev = jnp.maximum(n - 1, 0)
                base = [(dil * jj + rho) * per for jj in range(n_runs)]
                qruns = [pl.ds(pl.multiple_of(b + w * n, 8), w) for b in base]
                q = gather(q_scr, qruns)
                qq = jnp.concatenate([jnp.where(head0, q, 0.0), jnp.where(head0, 0.0, q)], axis=0)
                if dil == 1:
                    keys = pl.ds(pl.multiple_of(nprev * Q_BLOCK, Q_BLOCK), 2 * Q_BLOCK)
                    k = k_ref[keys, :].astype(BF)
                    v = v_ref[keys, :].astype(BF)
                else:
                    kruns = [pl.ds(pl.multiple_of(b + w * nprev, 16), 2 * w) for b in base]
                    k = gather(k_scr, kruns)
                    v = gather(v_scr, kruns)
                old = None if first else (
                    jnp.concatenate([gather(m_scr.at[0], qruns), gather(m_scr.at[1], qruns)], axis=0),
                    jnp.concatenate([gather(l_scr.at[0], qruns), gather(l_scr.at[1], qruns)], axis=0),
                    gather(a_scr, qruns))
                loaded.append((qruns, jnp.minimum(n, 1), qq.astype(BF), k,
                               jnp.concatenate([v, jnp.ones_like(v)], axis=1),
                               old))
            scores = [_dot_nt(qq, k) + mb_scr[case] for (_, case, qq, k, _, _) in loaded]
            stats = []
            for (_, _, _, _, _, old), s in zip(loaded, scores):
                m_new = jnp.max(s, axis=1, keepdims=True)
                m_new = (jnp.broadcast_to(m_new, (2 * Q_BLOCK, PAIR)) if first
                         else jnp.maximum(old[0], m_new))
                p = jnp.exp(s - jnp.concatenate([m_new, m_new], axis=1)).astype(BF)
                stats.append((m_new, p, None if first else jnp.exp(old[0] - m_new)))
            pvs = [jnp.dot(p, vo, preferred_element_type=F32)
                   for (_, _, _, _, vo, _), (_, p, _) in zip(loaded, stats)]
            for (qruns, _, _, _, _, old), (m_new, _, al), pv in zip(loaded, stats, pvs):
                l_new = pv[:, PAIR:2 * PAIR]
                a_new = (pv[0:Q_BLOCK, 0:PAIR], pv[Q_BLOCK:, 0:PAIR])
                if not first:
                    l_new = al * old[1] + l_new
                    a_new = (al[0:Q_BLOCK, :] * old[2] + a_new[0], al[Q_BLOCK:, :] * old[2] + a_new[1])
                a_new = jnp.where(head0, a_new[0], a_new[1])
                for jj, rr in enumerate(qruns):
                    a_scr[rr, :] = a_new[jj * w:(jj + 1) * w, :]
                    for e in range(2):
                        m_scr[e, rr, :] = m_new[e * Q_BLOCK + jj * w:e * Q_BLOCK + (jj + 1) * w, :]
                        l_scr[e, rr, :] = l_new[e * Q_BLOCK + jj * w:e * Q_BLOCK + (jj + 1) * w, :]
            return carry

        lax.fori_loop(0, n_units // ATTN_UNROLL, group, 0)

    for rho in range(DIL_STEP):
        for j in range(DIL_MAX // DIL_STEP):
            src = slice((DIL_STEP * j + rho) * per, (DIL_STEP * j + rho + 1) * per)
            q_scr[pl.ds(rho * quarter + j, per, stride=DIL_STEP), :] = a_scr[src, :] / jnp.where(
                head0, l_scr[0, src, :], l_scr[1, src, :])
    for rho in range(DIL_STEP):
        o_ref[pl.ds(rho, quarter, stride=DIL_STEP), :] = q_scr[rho * quarter:(rho + 1) * quarter, :]


def _attn_prompt(z3, slopes, keep):
    b, seq, _ = z3.shape
    win = pl.BlockSpec((None, PAIR, keep), lambda i, p: (i, p, 0))
    col = lambda off: pl.BlockSpec((None, seq, PAIR), lambda i, p: (i, 0, off + p))
    return pl.pallas_call(
        functools.partial(_attn_prompt_kernel, seq=seq, keep=keep),
        out_shape=(jax.ShapeDtypeStruct((b, seq, C_HEADS), F32),
                   jax.ShapeDtypeStruct((b, C_HEADS, keep), F32),
                   jax.ShapeDtypeStruct((b, C_HEADS, keep), F32)),
        grid=(b, N_PAIRS),
        in_specs=[pl.BlockSpec(memory_space=pltpu.SMEM),
                  col(0), col(N_PAIRS), col(2 * N_PAIRS)],
        out_specs=(pl.BlockSpec((None, seq, PAIR), lambda i, p: (i, 0, p)), win, win),
        scratch_shapes=[pltpu.VMEM((seq, PAIR), F32), pltpu.VMEM((seq, PAIR), BF),
                        pltpu.VMEM((seq, PAIR), BF), pltpu.VMEM((seq, PAIR), F32)]
        + [pltpu.VMEM((2, seq, PAIR), F32)] * 2
        + [pltpu.VMEM((2, 2 * Q_BLOCK, 2 * Q_BLOCK), F32)],
        compiler_params=_params("arbitrary", "arbitrary"),
        name="attn_prompt",
    )(slopes, z3, z3, z3)


def _attn_sample_kernel(sl_ref, q_ref, kn_ref, vn_ref, ck_ref, cv_ref, o_ref, ko_ref, vo_ref,
                        *, s_new, cache_len):
    lane_t = lax.broadcasted_iota(jnp.int32, (1, PAIR), 1)

    def shift_in(c_ref, new_ref, out_ref):
        rolled = pltpu.roll(c_ref[...], cache_len - s_new, 1)
        new_t = jnp.concatenate([new_ref[...], jnp.zeros((PAIR - s_new, C_HEADS), F32)], axis=0).T
        tail = jnp.where(lane_t >= PAIR - s_new, pltpu.roll(new_t, PAIR - s_new, 1),
                         rolled[:, cache_len - PAIR:cache_len])
        out_ref[:, 0:cache_len - PAIR] = rolled[:, 0:cache_len - PAIR]
        out_ref[:, cache_len - PAIR:cache_len] = tail

    shift_in(ck_ref, kn_ref, ko_ref)
    shift_in(cv_ref, vn_ref, vo_ref)

    rows = s_new * N_HEADS
    ri = lax.broadcasted_iota(jnp.int32, (rows, 1), 0)
    lane = lax.broadcasted_iota(jnp.int32, (1, C_HEADS), 1)
    own = (ri & (N_HEADS - 1)) == (lane >> 6)
    q = q_ref[...]
    qe = jnp.broadcast_to(q[:, None, :], (s_new, N_HEADS, C_HEADS)).reshape(rows, C_HEADS)
    qe = jnp.where(own, qe, 0.0)
    slope = sl_ref[:, 0:1]
    spos = ri >> 3

    def weights(dist):
        mult = jnp.zeros(dist.shape, F32)
        for window, dil in DILATIONS:
            hit = (dist >= 0) & (dist <= window) & ((dist & (dil - 1)) == 0)
            mult = mult + hit.astype(F32)
        return mult

    dist_c = cache_len + spos - lax.broadcasted_iota(jnp.int32, (1, cache_len), 1)
    dist_n = spos - lax.broadcasted_iota(jnp.int32, (1, s_new), 1)
    mult_c = weights(dist_c)
    mult_n = weights(dist_n)
    sc = _dot(qe, ck_ref[...]) * (HEAD_DIM ** -0.5) - slope * dist_c.astype(F32)
    sn = _dot_nt(qe, kn_ref[...]) * (HEAD_DIM ** -0.5) - slope * dist_n.astype(F32)
    sc = jnp.where(mult_c > 0, sc, NEG_BIG)
    sn = jnp.where(mult_n > 0, sn, NEG_BIG)
    mx = jnp.maximum(jnp.max(sc, axis=1, keepdims=True), jnp.max(sn, axis=1, keepdims=True))
    pc = mult_c * jnp.exp(sc - mx)
    pn = mult_n * jnp.exp(sn - mx)
    den = jnp.sum(pc, axis=1, keepdims=True) + jnp.sum(pn, axis=1, keepdims=True)
    num = _dot_nt(pc, cv_ref[...]) + _dot(pn, vn_ref[...])
    num = jnp.where(own, num, 0.0).reshape(s_new, N_HEADS, C_HEADS).sum(axis=1)
    den = jnp.where(own, den, 0.0).reshape(s_new, N_HEADS, C_HEADS).sum(axis=1)
    o_ref[...] = num / den


def _attn_sample(z3, cache_k, cache_v, slopes):
    b, s_new, _ = z3.shape
    cache_len = cache_k.shape[2]
    assert cache_len % PAIR == 0 and cache_len > PAIR and s_new < PAIR
    new = lambda c: pl.BlockSpec((None, s_new, C_HEADS), lambda i: (i, 0, c))
    cache = pl.BlockSpec((None, C_HEADS, cache_len), lambda i: (i, 0, 0))
    slope_rows = jnp.broadcast_to(jnp.tile(slopes, s_new)[:, None], (s_new * N_HEADS, PAIR))
    return pl.pallas_call(
        functools.partial(_attn_sample_kernel, s_new=s_new, cache_len=cache_len),
        out_shape=(jax.ShapeDtypeStruct((b, s_new, C_HEADS), F32),
                   jax.ShapeDtypeStruct(cache_k.shape, F32),
                   jax.ShapeDtypeStruct(cache_v.shape, F32)),
        grid=(b,),
        in_specs=[_const_spec((s_new * N_HEADS, PAIR)), new(0), new(1), new(2), cache, cache],
        out_specs=(new(0), cache, cache),
        compiler_params=_params("arbitrary"),
        name="attn_sample",
    )(slope_rows, z3, z3, z3, cache_k, cache_v)


def _rwkv_kernel(r_ref, k_ref, v_ref, wag_ref, pr_ref, pk_ref, pv_ref, pwag_ref,
                 sr_ref, sk_ref, sv_ref, swag_ref, m0_ref, vec_ref, muwag_ref, lora_ref, seg_ref,
                 o_ref, mout_ref, m_scr, y_scr, *, t_valid, nsb, chunk, per_chunk_state):
    tb = pl.program_id(1)
    first = tb == 0
    n_rows = nsb * SUPER
    shift = chunk.bit_length() - 1
    n_chunks = SUPER // chunk

    lane = lax.broadcasted_iota(jnp.int32, (1, PAIR), 1)
    head0 = lane < HEAD_DIM

    def stack(x):
        return jnp.concatenate([jnp.where(head0, x, 0.0), jnp.where(head0, 0.0, x)], axis=0)

    def transposed(x):
        n = x.shape[1]
        eye_n = (lax.broadcasted_iota(jnp.int32, (n, n), 0)
                 == lax.broadcasted_iota(jnp.int32, (n, n), 1)).astype(BF)
        hi, lo = _split(x)
        return _dot_nt(eye_n, hi) + _dot_nt(eye_n, lo)

    def states_in(ref):
        return transposed(ref[...].reshape(N_HEADS * HEAD_DIM, HEAD_DIM))

    def pair_state(t, p):
        return stack(t[:, p * PAIR:(p + 1) * PAIR])

    def pair_out(m_pair):
        return transposed(m_pair[0:HEAD_DIM, :] + m_pair[HEAD_DIM:PAIR, :])

    if not per_chunk_state:
        @pl.when(first)
        def _():
            t_in = states_in(m0_ref.at[0])
            for p in range(N_PAIRS):
                m_scr[p] = pair_state(t_in, p)

    rows = lax.broadcasted_iota(jnp.int32, (n_rows, 1), 0)

    def token_shift(cur_ref, prev_ref, carry_ref, mu):
        cur = cur_ref[...]
        if per_chunk_state:
            prev = prev_ref[...]
        else:
            last = jnp.where(first, carry_ref[...], prev_ref[7:8, :])
            prev = jnp.where(rows == 0, last, pltpu.roll(cur, 1, 0))
        return cur + (prev - cur) * mu

    vec = vec_ref[...]
    row = lambda i: vec[i:i + 1, :]
    zr = token_shift(r_ref, pr_ref, sr_ref, row(_V_MU_R))
    zk = token_shift(k_ref, pk_ref, sk_ref, row(_V_MU_K))
    zv = token_shift(v_ref, pv_ref, sv_ref, row(_V_MU_V))
    zwag = token_shift(wag_ref, pwag_ref, swag_ref, muwag_ref[...])
    wa = zwag[:, 0:D_LORA_WA]
    gi = zwag[:, D_LORA_WA:D_LORA_WA + D_G_LORA]

    wlin = row(_V_W0) + _dot(jnp.tanh(wa), lora_ref[0])
    softplus = jnp.maximum(-wlin, 0.0) + jnp.log(1.0 + jnp.exp(-jnp.abs(wlin)))
    w_log = -softplus - 0.5
    ld = -jnp.exp(w_log)
    lr = jax.nn.sigmoid(row(_V_A0) + _dot(wa, lora_ref[1]))
    gate = _dot(jax.nn.sigmoid(gi), lora_ref[2])
    seg = seg_ref[...]

    def head_sum(x):
        return jnp.concatenate([_dot_exact_rhs(x[:, i:i + SEG_W], seg)
                                for i in range(0, C_HEADS, SEG_W)], axis=1)

    kk = zk * row(_V_KK)
    kk = kk / jnp.maximum(jnp.sqrt(head_sum(kk * kk)), 1e-12)
    kmod = zk * (1.0 + (lr - 1.0) * row(_V_KA))
    vv = zv
    if t_valid is not None:
        live = ((rows & (chunk - 1)) if per_chunk_state else (rows + tb * n_rows)) < t_valid
        ld = jnp.where(live, ld, 0.0)
        kk = jnp.where(live, kk, 0.0)
        kmod = jnp.where(live, kmod, 0.0)
        vv = jnp.where(live, vv, 0.0)

    span = min(PAIR, n_rows)
    ti = lax.broadcasted_iota(jnp.int32, (span, span), 0)
    tj = lax.broadcasted_iota(jnp.int32, (span, span), 1)
    same_chunk = (ti >> shift) == (tj >> shift)
    sums = jnp.concatenate([same_chunk & (ti >= tj), same_chunk], axis=0).astype(BF)
    cum_tot = [_dot_exact_lhs(sums, ld[i:i + span, :]) for i in range(0, n_rows, span)]
    cum = jnp.concatenate([x[0:span, :] for x in cum_tot], axis=0)
    tot = jnp.concatenate([x[span:2 * span, :] for x in cum_tot], axis=0)
    dec_in = jnp.exp(cum)
    dec_ex = jnp.exp(cum - ld)
    dec_inv = jnp.exp(-cum)
    dec_end = jnp.exp(tot - cum)
    dec_tot = jnp.exp(tot)
    beta = kk * lr
    abar = -(kk * dec_ex)
    rbar = zr * dec_in
    bt = beta * dec_inv
    kt = kmod * dec_inv
    bh = beta * dec_end
    kh = kmod * dec_end

    ri = lax.broadcasted_iota(jnp.int32, (PAIR, PAIR), 0)
    ci = lax.broadcasted_iota(jnp.int32, (PAIR, PAIR), 1)
    same_blk = (ri >> shift) == (ci >> shift)
    strict = same_blk & (ri > ci)
    incl = same_blk & (ri >= ci)
    same_head = (ri >> 6) == (ci >> 6)
    eye = ri == ci
    zeros_sp = jnp.zeros((SUPER, PAIR), F32)
    zeros_pp = jnp.zeros((PAIR, PAIR), F32)

    def unstack(x):
        return x[0:SUPER, :] + x[SUPER:PAIR, :]

    def both_heads(x, keep):
        return jnp.where(keep, jnp.concatenate([x, x], axis=0), 0.0)

    eye_f = jnp.where(eye, 1.0, 0.0)
    in_chunk = [((lane & (SUPER - 1)) >> shift) == c for c in range(n_chunks)]

    def phase1(blocks, out):
        units = [(s, p) for s in blocks for p in range(N_PAIRS)]
        tile = lambda x, u: x[u[0] * SUPER:(u[0] + 1) * SUPER, u[1] * PAIR:(u[1] + 1) * PAIR]
        ab = [tile(abar, u) for u in units]
        rb = [tile(rbar, u) for u in units]
        v_p = [tile(vv, u) for u in units]
        v_s = [stack(x) for x in v_p]
        a_all = [_dot_nt(jnp.concatenate([a, r], axis=0),
                         jnp.concatenate([stack(tile(bt, u)), stack(tile(kt, u))], axis=0))
                 for a, r, u in zip(ab, rb, units)]
        yield
        n_ab = [both_heads(a[0:SUPER, 0:PAIR], strict) for a in a_all]
        a_ak = [both_heads(a[0:SUPER, PAIR:2 * PAIR], strict) for a in a_all]
        a_rbk = [jnp.concatenate([both_heads(a[SUPER:PAIR, 0:PAIR], incl),
                                  both_heads(a[SUPER:PAIR, PAIR:2 * PAIR], incl)], axis=1)
                 for a in a_all]
        tinv = [eye_f + n for n in n_ab]
        power = n_ab
        for _ in range(shift - 1):
            power = [_dot(x, x) for x in power]
            yield
            tinv = [t + _dot(x, t) for x, t in zip(power, tinv)]
            yield
        u_s = [_dot(a, v) for a, v in zip(a_ak, v_s)]
        yield
        ta = [_dot(t, jnp.concatenate([stack(a), u], axis=1))
              for t, a, u in zip(tinv, ab, u_s)]
        yield
        ry = [_dot(a, jnp.concatenate([t, jnp.concatenate([zeros_pp, v], axis=1)], axis=0))
              for a, t, v in zip(a_rbk, ta, v_s)]
        yield
        r1 = [unstack(stack(r) + y[:, 0:PAIR]) for r, y in zip(rb, ry)]
        y0 = [unstack(y[:, PAIR:2 * PAIR]) for y in ry]
        bk_t = [jnp.concatenate([tile(bh, u), tile(kh, u)], axis=0).astype(BF).T
                for u in units]
        rhs3 = [jnp.concatenate([jnp.concatenate([unstack(t[:, 0:PAIR]),
                                                  unstack(t[:, PAIR:2 * PAIR])], axis=1),
                                 jnp.concatenate([zeros_sp, v], axis=1)], axis=0)
                for t, v in zip(ta, v_p)]
        gh = [_dot(jnp.concatenate([jnp.where(in_chunk[c], b, jnp.zeros_like(b))
                                    for c in range(n_chunks)], axis=0), r)
              for b, r in zip(bk_t, rhs3)]
        for i, u in enumerate(units):
            out[u] = (r1[i], y0[i], gh[i])
        yield

    m = None if per_chunk_state else [m_scr[p] for p in range(N_PAIRS)]

    def phase2(blocks, fac):
        for s in blocks:
            for c in range(n_chunks):
                tok = slice(c * chunk, (c + 1) * chunk)
                r0 = s * SUPER + c * chunk
                seq_i = s * n_chunks + c
                t_in = states_in(m0_ref.at[seq_i]) if per_chunk_state else None
                for p in range(N_PAIRS):
                    r1, y0, gh = fac[(s, p)]
                    sl = slice(p * PAIR, (p + 1) * PAIR)
                    gh_c = gh[c * PAIR:(c + 1) * PAIR, :]
                    g_c = (jnp.where(eye, dec_tot[r0:r0 + 1, sl], 0.0)
                           + jnp.where(same_head, gh_c[:, 0:PAIR], 0.0))
                    h_c = jnp.where(same_head, gh_c[:, PAIR:2 * PAIR], 0.0)
                    m_in = pair_state(t_in, p) if per_chunk_state else m[p]
                    y_scr[r0:r0 + chunk, sl] = _dot(r1[tok, :], m_in) + y0[tok, :]
                    m_out = _dot(g_c, m_in) + h_c
                    if per_chunk_state:
                        mout_ref[seq_i, 2 * p:2 * p + 2] = pair_out(m_out).reshape(
                            2, HEAD_DIM, HEAD_DIM)
                    else:
                        m[p] = m_out
                yield

    groups = [list(range(g, min(g + RWKV_GROUP, nsb))) for g in range(0, nsb, RWKV_GROUP)]
    fac = {}
    for _ in phase1(groups[0], fac):
        pass
    for done, nxt in zip(groups, groups[1:]):
        steps = phase2(done, fac)
        for _ in phase1(nxt, fac):
            next(steps, None)
        for _ in steps:
            pass
    for _ in phase2(groups[-1], fac):
        pass
    if not per_chunk_state:
        for p in range(N_PAIRS):
            m_scr[p] = m[p]

    y = y_scr[...]
    mean = head_sum(y) * (1.0 / HEAD_DIM)
    yc = y - mean
    var = head_sum(yc * yc) * (1.0 / HEAD_DIM)
    yn = yc * lax.rsqrt(var + GN_EPS) * row(_V_LNW) + row(_V_LNB)
    bonus = head_sum(zr * kmod * row(_V_RK)) * zv
    o_ref[...] = (yn + bonus) * gate

    if not per_chunk_state:
        @pl.when(tb == pl.num_programs(1) - 1)
        def _():
            for p in range(N_PAIRS):
                mout_ref[0, 2 * p:2 * p + 2] = pair_out(m_scr[p]).reshape(2, HEAD_DIM, HEAD_DIM)


def _rwkv(z3, prev3, shift_prev, m0, wts, t_valid, nsb, chunk, per_chunk_state):
    b, seq, _ = z3.shape
    n_rows = nsb * SUPER
    assert seq % n_rows == 0 and SUPER % chunk == 0 and chunk % 8 == 0
    nt = seq // n_rows
    n_state = n_rows // chunk if per_chunk_state else 1
    assert m0.shape[0] == b * n_state and (nt == 1 or not per_chunk_state)
    cur = lambda w, c: pl.BlockSpec((None, n_rows, w), lambda i, t: (i, t, c))
    if per_chunk_state:
        prev = cur
    else:
        prev = lambda w, c: pl.BlockSpec(
            (None, 8, w), lambda i, t: (i, jnp.maximum(t * (n_rows // 8) - 1, 0), c))
    carry = lambda w: pl.BlockSpec((None, 1, w), lambda i, t: (i, 0, 0))
    state = pl.BlockSpec((n_state, N_HEADS, HEAD_DIM, HEAD_DIM), lambda i, t: (i, 0, 0, 0))
    wag_w = D_LORA_WA + D_G_LORA
    sp = shift_prev[:, None, :]
    all_valid = t_valid == (chunk if per_chunk_state else seq)
    return pl.pallas_call(
        functools.partial(_rwkv_kernel, t_valid=None if all_valid else t_valid, nsb=nsb,
                          chunk=chunk, per_chunk_state=per_chunk_state),
        out_shape=(jax.ShapeDtypeStruct((b, seq, C_HEADS), F32),
                   jax.ShapeDtypeStruct(m0.shape, F32)),
        grid=(b, nt),
        in_specs=[cur(C_HEADS, 3), cur(C_HEADS, 4), cur(C_HEADS, 5), cur(wag_w, 12),
                  prev(C_HEADS, 3), prev(C_HEADS, 4), prev(C_HEADS, 5), prev(wag_w, 12),
                  carry(C_HEADS), carry(C_HEADS), carry(C_HEADS), carry(wag_w),
                  state,
                  _const_spec((_V_ROWS, C_HEADS)), _const_spec((1, wag_w)),
                  _const_spec((3, PAIR, C_HEADS)), _const_spec((SEG_W, SEG_W))],
        out_specs=(pl.BlockSpec((None, n_rows, C_HEADS), lambda i, t: (i, t, 0)), state),
        scratch_shapes=[pltpu.VMEM((N_PAIRS, PAIR, PAIR), F32),
                        pltpu.VMEM((n_rows, C_HEADS), F32)],
        compiler_params=_params("arbitrary", "arbitrary"),
        name="rwkv",
    )(z3, z3, z3, z3, prev3, prev3, prev3, prev3,
      sp[:, :, 0:C_HEADS], sp[:, :, C_HEADS:2 * C_HEADS], sp[:, :, 2 * C_HEADS:3 * C_HEADS],
      sp[:, :, 3 * C_HEADS:], m0, wts["vec"], wts["mu_wag"], wts["lora"], wts["seg"])


def _pack_layer(w_in, mu_shift, w0, w2, a0, a2, g2, k_k, k_a, r_k, lnx_w, lnx_b, w_out, ln1_w,
                ln1_b, w_gate, w_up, w_down, ln2_w, ln2_b, w_ple_gate, w_ple_proj, ln3_w, ln3_b):
    mu_r, mu_k, mu_v = (mu_shift[i * C_HEADS:(i + 1) * C_HEADS] for i in range(3))
    vec = jnp.stack([mu_r, mu_k, mu_v, w0, a0, k_k, k_a, r_k.reshape(-1), lnx_w, lnx_b])
    vec = jnp.concatenate([vec, jnp.zeros((_V_ROWS - vec.shape[0], C_HEADS), F32)], axis=0)
    half = D_LORA_WA // 2
    zeros = jnp.zeros((half, C_HEADS), F32)
    lora = jnp.stack([jnp.concatenate([w2, zeros], axis=0),
                      jnp.concatenate([zeros, a2], axis=0),
                      g2]).astype(BF)
    head_of = jnp.arange(SEG_W) // HEAD_DIM
    seg = (head_of[:, None] == head_of[None, :]).astype(BF)
    ln = jnp.stack([ln1_w, ln1_b, ln2_w, ln2_b, ln3_w, ln3_b,
                    jnp.zeros_like(ln1_w), jnp.zeros_like(ln1_w)])

    return {
        "w_in": w_in.astype(BF), "vec": vec, "mu_wag": mu_shift[None, 3 * C_HEADS:],
        "lora": lora, "seg": seg, "w_out": w_out.astype(BF),
        "w_gate": w_gate.astype(BF), "w_up": w_up.astype(BF),
        "w_down": w_down.astype(BF),
        "w_ple_gate": w_ple_gate.astype(BF), "w_ple_proj": w_ple_proj.astype(BF), "ln": ln,
    }


def _alibi_slopes():
    h = jnp.arange(1, N_HEADS + 1, dtype=F32)
    return jnp.exp2(-8.0 * h / N_HEADS)


def _layer(x, p_l, wts, alpha, shift_prev, wkv_prev, cache_k=None, cache_v=None):
    b, seq, _ = x.shape
    x2d = x.reshape(b * seq, D_MODEL)
    z = _in_proj(x2d, wts["w_in"])
    z3 = z.reshape(b, seq, D_IN)
    slopes = _alibi_slopes()
    if cache_k is None:
        keep = min(WINDOW_MAX, seq)
        att, k_win, v_win = _attn_prompt(z3, slopes, keep)
        from_cm = lambda c: jnp.transpose(c.reshape(b, N_HEADS, HEAD_DIM, keep), (0, 3, 1, 2))
        k_win = from_cm(k_win)
        v_win = from_cm(v_win)
    else:
        cache_len = cache_k.shape[1]
        to_cm = lambda c: jnp.transpose(c, (0, 2, 3, 1)).reshape(b, C_HEADS, cache_len)
        from_cm = lambda c: jnp.transpose(c.reshape(b, N_HEADS, HEAD_DIM, cache_len), (0, 3, 1, 2))
        att, k_win, v_win = _attn_sample(z3, to_cm(cache_k), to_cm(cache_v), slopes)
        k_win = from_cm(k_win)
        v_win = from_cm(v_win)
    if seq % SUPER == 0:
        nsb = RWKV_BLOCKS if seq % (RWKV_BLOCKS * SUPER) == 0 else 1
        rw, m_last = _rwkv(z3, z3, shift_prev, wkv_prev, wts, seq, nsb, CHUNK,
                           False)
    else:
        chunk = 8
        per_block = SUPER // chunk
        assert seq <= chunk and b % per_block == 0
        first_prev = jnp.concatenate([jnp.zeros((b, 1, 3 * C_HEADS), F32), shift_prev[:, None, :]],
                                     axis=-1)
        prev = jnp.concatenate([first_prev, z3[:, :seq - 1]], axis=1)
        blocks = lambda a: jnp.pad(a, ((0, 0), (0, chunk - seq), (0, 0))).reshape(
            b // per_block, SUPER, D_IN)
        rw, m_last = _rwkv(blocks(z3), blocks(prev), jnp.zeros((b // per_block, D_B_IN), F32),
                           wkv_prev, wts, seq, 1, chunk, True)
        rw = rw.reshape(b, chunk, C_HEADS)[:, :seq]
    y = _post(att.reshape(b * seq, C_HEADS), rw.reshape(b * seq, C_HEADS), x2d,
              p_l.reshape(b * seq, PLE_DIM), wts, alpha)
    shift_new = z3[:, seq - 1, 3 * C_HEADS:]
    return y.reshape(b, seq, D_MODEL), k_win, v_win, shift_new, m_last


def kernel(x_prompt, x_sample, p_prompt, p_sample, cache_k_win, cache_v_win, state_wkv, state_shift, w_in, mu_shift, w0, w2, a0, a2, g2, k_k, k_a, r_k, lnx_w, lnx_b, w_out, ln1_w, ln1_b, w_gate, w_up, w_down, ln2_w, ln2_b, w_ple_gate, w_ple_proj, ln3_w, ln3_b):
    depth = w_in.shape[0]
    alpha = float((2 * depth) ** 0.25)
    xp, xs = x_prompt, x_sample
    bp = x_prompt.shape[0]
    shift0 = jnp.zeros((bp, D_B_IN), x_prompt.dtype)
    wkv0 = jnp.zeros((bp, N_HEADS, HEAD_DIM, HEAD_DIM), state_wkv.dtype)
    outs = [[] for _ in range(8)]
    for l in range(depth):
        wts = _pack_layer(w_in[l], mu_shift[l], w0[l], w2[l], a0[l], a2[l], g2[l], k_k[l], k_a[l],
                          r_k[l], lnx_w[l], lnx_b[l], w_out[l], ln1_w[l], ln1_b[l], w_gate[l],
                          w_up[l], w_down[l], ln2_w[l], ln2_b[l], w_ple_gate[l], w_ple_proj[l],
                          ln3_w[l], ln3_b[l])
        xp, kw, vw, sh, wk = _layer(xp, p_prompt[l], wts, alpha, shift0, wkv0)
        for lst, val in zip(outs[0:4], (kw, vw, wk, sh)):
            lst.append(val)
        xs, kw, vw, sh, wk = _layer(xs, p_sample[l], wts, alpha, state_shift[l], state_wkv[l],
                                    cache_k_win[l], cache_v_win[l])
        for lst, val in zip(outs[4:8], (kw, vw, wk, sh)):
            lst.append(val)
    return (xp, xs) + tuple(jnp.stack(o) for o in outs)
```

```python
import functools

import jax
import jax.numpy as jnp
from jax import lax
from jax.experimental import pallas as pl
from jax.experimental.pallas import tpu as pltpu

BF = jnp.bfloat16
F32 = jnp.float32

D_MODEL = 1024
HEAD_DIM = 64
N_HEADS = 8
C_HEADS = N_HEADS * HEAD_DIM
PAIR = 2 * HEAD_DIM
N_PAIRS = N_HEADS // 2
DILATIONS = ((128, 1), (512, 4), (2048, 16))
WINDOW_MAX = 2048
Q_BLOCK = 128
DIL_MAX = max(d for _, d in DILATIONS)
DIL_STEP = 4
D_LORA_WA = 128
D_G_LORA = 128
D_B_IN = 3 * C_HEADS + D_LORA_WA + D_G_LORA
D_IN = 3 * C_HEADS + D_B_IN
D_FF = 2816
PLE_DIM = 256
LN_EPS = 1e-5
GN_EPS = 64e-5
NEG_BIG = -1e30

CHUNK = 16
SUPER = 64
N_CHUNKS = SUPER // CHUNK
RWKV_BLOCKS = 8
RWKV_GROUP = 2
FF_CHUNK = 256
POST_PARTS = 2
SEG_W = 256
ATTN_UNROLL = 8
IN_CHUNK = 256
VMEM_LIMIT_BYTES = 56 * 1024 * 1024

(_V_MU_R, _V_MU_K, _V_MU_V, _V_W0, _V_A0, _V_KK, _V_KA, _V_RK, _V_LNW, _V_LNB) = range(10)
_V_ROWS = 16


def _params(*sem):
    return pltpu.CompilerParams(dimension_semantics=sem, vmem_limit_bytes=VMEM_LIMIT_BYTES)


def _const_spec(shape):
    nd = len(shape)
    return pl.BlockSpec(shape, lambda *_: (0,) * nd, pipeline_mode=pl.Buffered(1))


def _dot(a, b):
    return jnp.dot(a.astype(BF), b.astype(BF), preferred_element_type=F32)


def _dot_nt(a, b):
    return lax.dot_general(a.astype(BF), b.astype(BF), (((1,), (1,)), ((), ())),
                           preferred_element_type=F32)


def _split(x):
    hi = x.astype(BF)
    lo = (x - hi.astype(F32)).astype(BF)
    return hi, lo


def _dot_exact_lhs(mask_bf, x):
    hi, lo = _split(x)
    return (jnp.dot(mask_bf, hi, preferred_element_type=F32)
            + jnp.dot(mask_bf, lo, preferred_element_type=F32))


def _dot_exact_rhs(x, mask_bf):
    hi, lo = _split(x)
    return (jnp.dot(hi, mask_bf, preferred_element_type=F32)
            + jnp.dot(lo, mask_bf, preferred_element_type=F32))


def _layer_norm(x, w, b):
    mu = jnp.mean(x, axis=-1, keepdims=True)
    xc = x - mu
    var = jnp.mean(xc * xc, axis=-1, keepdims=True)
    return xc * lax.rsqrt(var + LN_EPS) * w + b


def _in_proj_kernel(x_ref, w_ref, o_ref):
    xb = x_ref[...].astype(BF)
    for c in range(D_IN // IN_CHUNK):
        o_ref[:, c * IN_CHUNK:(c + 1) * IN_CHUNK] = jnp.dot(xb, w_ref[:, c * IN_CHUNK:(c + 1) * IN_CHUNK],
                                                            preferred_element_type=F32)


def _in_proj(x2d, w_bf):
    m = x2d.shape[0]
    tm = min(1024, m)
    assert m % tm == 0
    return pl.pallas_call(
        _in_proj_kernel,
        out_shape=jax.ShapeDtypeStruct((m, D_IN), F32),
        grid=(m // tm,),
        in_specs=[pl.BlockSpec((tm, D_MODEL), lambda i: (i, 0)),
                  _const_spec((D_MODEL, D_IN))],
        out_specs=pl.BlockSpec((tm, D_IN), lambda i: (i, 0)),
        compiler_params=_params("arbitrary"),
        name="in_proj",
    )(x2d, w_bf)


def _post_kernel(att_ref, rw_ref, x_ref, p_ref, wo_ref, wg_ref, wu_ref, wd_ref, wpg_ref,
                 wpp_ref, ln_ref, o_ref, *, alpha):
    part = x_ref.shape[0] // POST_PARTS
    parts = [slice(i * part, (i + 1) * part) for i in range(POST_PARTS)]
    mix = [_dot(att_ref[r, :], wo_ref[0:C_HEADS, :])
           + _dot(rw_ref[r, :], wo_ref[C_HEADS:2 * C_HEADS, :]) for r in parts]
    h = jnp.concatenate([_layer_norm(alpha * x_ref[r, :] + m, ln_ref[0:1, :], ln_ref[1:2, :])
                         for r, m in zip(parts, mix)], axis=0)
    hb = h.astype(BF)
    ffn = jnp.zeros_like(h)
    for c in range(D_FF // FF_CHUNK):
        cols = slice(c * FF_CHUNK, (c + 1) * FF_CHUNK)
        g = jnp.dot(hb, wg_ref[:, cols], preferred_element_type=F32)
        u = jnp.dot(hb, wu_ref[:, cols], preferred_element_type=F32)
        act = g * jax.nn.sigmoid(g) * u
        ffn = ffn + jnp.dot(act.astype(BF), wd_ref[cols, :],
                            preferred_element_type=F32)
    emb = _dot(p_ref[...], wpp_ref[...])
    h2 = [_layer_norm(alpha * h[r, :] + ffn[r, :], ln_ref[2:3, :], ln_ref[3:4, :]) for r in parts]
    ple = [jax.nn.sigmoid(_dot(hh, wpg_ref[...])) * emb[r, :] for r, hh in zip(parts, h2)]
    for r, hh, e in zip(parts, h2, ple):
        o_ref[r, :] = _layer_norm(alpha * hh + e, ln_ref[4:5, :], ln_ref[5:6, :])


def _post(att, rw, x2d, p2d, wts, alpha):
    m = x2d.shape[0]
    tm = min(512, m)
    assert m % tm == 0
    row = lambda w: pl.BlockSpec((tm, w), lambda i: (i, 0))
    return pl.pallas_call(
        functools.partial(_post_kernel, alpha=alpha),
        out_shape=jax.ShapeDtypeStruct((m, D_MODEL), F32),
        grid=(m // tm,),
        in_specs=[row(C_HEADS), row(C_HEADS), row(D_MODEL), row(PLE_DIM),
                  _const_spec((2 * C_HEADS, D_MODEL)),
                  _const_spec((D_MODEL, D_FF)), _const_spec((D_MODEL, D_FF)),
                  _const_spec((D_FF, D_MODEL)),
                  _const_spec((D_MODEL, D_MODEL)), _const_spec((PLE_DIM, D_MODEL)),
                  _const_spec((8, D_MODEL))],
        out_specs=row(D_MODEL),
        compiler_params=_params("arbitrary"),
        name="post",
    )(att, rw, x2d, p2d, wts["w_out"], wts["w_gate"], wts["w_up"], wts["w_down"],
      wts["w_ple_gate"], wts["w_ple_proj"], wts["ln"])


def _attn_prompt_kernel(sl_ref, q_ref, k_ref, v_ref, o_ref, kw_ref, vw_ref, q_scr, k_scr, v_scr,
                        a_scr, m_scr, l_scr, mb_scr, *, seq, keep):
    pair = pl.program_id(1)
    slopes = (sl_ref[2 * pair], sl_ref[2 * pair + 1])
    lane = lax.broadcasted_iota(jnp.int32, (1, PAIR), 1)
    head0 = lane < HEAD_DIM
    kw_ref[...] = k_ref[seq - keep:seq, :].T
    vw_ref[...] = v_ref[seq - keep:seq, :].T
    qi = lax.broadcasted_iota(jnp.int32, (Q_BLOCK, 1), 0)
    ki = lax.broadcasted_iota(jnp.int32, (1, 2 * Q_BLOCK), 1)

    per = seq // DIL_MAX
    quarter = seq // DIL_STEP

    def to_residue_major(src_ref, dst_ref, convert):
        for rho in range(DIL_STEP):
            a_scr[rho * quarter:(rho + 1) * quarter, :] = src_ref[
                pl.ds(rho, quarter, stride=DIL_STEP), :]
        for rho in range(DIL_STEP):
            for j in range(DIL_MAX // DIL_STEP):
                r = DIL_STEP * j + rho
                dst_ref[r * per:(r + 1) * per, :] = convert(
                    a_scr[pl.ds(rho * quarter + j, per, stride=DIL_STEP), :])

    to_residue_major(q_ref, q_scr, lambda x: x * (HEAD_DIM ** -0.5))
    to_residue_major(k_ref, k_scr, lambda x: x.astype(BF))
    to_residue_major(v_ref, v_scr, lambda x: x.astype(BF))

    def gather(ref, runs):
        return jnp.concatenate([ref[rr, :] for rr in runs], axis=0)

    for window, dil in DILATIONS:
        n_runs = DIL_MAX // dil
        w = Q_BLOCK // n_runs
        nb = seq // dil // Q_BLOCK
        n_units = nb * dil
        assert nb >= 2 and n_units * Q_BLOCK == seq and n_units % ATTN_UNROLL == 0 and w % 8 == 0
        tq = (qi & (w - 1)) * DIL_MAX + (qi >> (w.bit_length() - 1)) * dil
        tk = (ki & (2 * w - 1)) * DIL_MAX + (ki >> (w.bit_length())) * dil
        if dil == 1:
            tk = ki
        for case in range(2):
            dist = case * Q_BLOCK * dil + tq - tk
            valid = (dist >= 0) & (dist <= window)
            distf = dist.astype(F32)
            mb_scr[case] = jnp.concatenate(
                [jnp.where(valid, -slopes[e] * distf, NEG_BIG) for e in range(2)], axis=0)

        def group(g, carry, dil=dil, n_runs=n_runs, w=w, first=(window, dil) == DILATIONS[0]):
            loaded = []
            for j in range(ATTN_UNROLL):
                u = g * ATTN_UNROLL + j
                n = u >> (dil.bit_length() - 1)
                rho = u & (dil - 1)
                nprev = jnp.maximum(n - 1, 0)
                base = [(dil * jj + rho) * per for jj in range(n_runs)]
                qruns = [pl.ds(pl.multiple_of(b + w * n, 8), w) for b in base]
                q = gather(q_scr, qruns)
                qq = jnp.concatenate([jnp.where(head0, q, 0.0), jnp.where(head0, 0.0, q)], axis=0)
                if dil == 1:
                    keys = pl.ds(pl.multiple_of(nprev * Q_BLOCK, Q_BLOCK), 2 * Q_BLOCK)
                    k = k_ref[keys, :].astype(BF)
                    v = v_ref[keys, :].astype(BF)
                else:
                    kruns = [pl.ds(pl.multiple_of(b + w * nprev, 16), 2 * w) for b in base]
                    k = gather(k_scr, kruns)
                    v = gather(v_scr, kruns)
                old = None if first else (
                    jnp.concatenate([gather(m_scr.at[0], qruns), gather(m_scr.at[1], qruns)], axis=0),
                    jnp.concatenate([gather(l_scr.at[0], qruns), gather(l_scr.at[1], qruns)], axis=0),
                    gather(a_scr, qruns))
                loaded.append((qruns, jnp.minimum(n, 1), qq.astype(BF), k,
                               jnp.concatenate([v, jnp.ones_like(v)], axis=1),
                               old))
            scores = [_dot_nt(qq, k) + mb_scr[case] for (_, case, qq, k, _, _) in loaded]
            stats = []
            for (_, _, _, _, _, old), s in zip(loaded, scores):
                m_new = jnp.max(s, axis=1, keepdims=True)
                m_new = (jnp.broadcast_to(m_new, (2 * Q_BLOCK, PAIR)) if first
                         else jnp.maximum(old[0], m_new))
                p = jnp.exp(s - jnp.concatenate([m_new, m_new], axis=1)).astype(BF)
                stats.append((m_new, p, None if first else jnp.exp(old[0] - m_new)))
            pvs = [jnp.dot(p, vo, preferred_element_type=F32)
                   for (_, _, _, _, vo, _), (_, p, _) in zip(loaded, stats)]
            for (qruns, _, _, _, _, old), (m_new, _, al), pv in zip(loaded, stats, pvs):
                l_new = pv[:, PAIR:2 * PAIR]
                a_new = (pv[0:Q_BLOCK, 0:PAIR], pv[Q_BLOCK:, 0:PAIR])
                if not first:
                    l_new = al * old[1] + l_new
                    a_new = (al[0:Q_BLOCK, :] * old[2] + a_new[0], al[Q_BLOCK:, :] * old[2] + a_new[1])
                a_new = jnp.where(head0, a_new[0], a_new[1])
                for jj, rr in enumerate(qruns):
                    a_scr[rr, :] = a_new[jj * w:(jj + 1) * w, :]
                    for e in range(2):
                        m_scr[e, rr, :] = m_new[e * Q_BLOCK + jj * w:e * Q_BLOCK + (jj + 1) * w, :]
                        l_scr[e, rr, :] = l_new[e * Q_BLOCK + jj * w:e * Q_BLOCK + (jj + 1) * w, :]
            return carry

        lax.fori_loop(0, n_units // ATTN_UNROLL, group, 0)

    for rho in range(DIL_STEP):
        for j in range(DIL_MAX // DIL_STEP):
            src = slice((DIL_STEP * j + rho) * per, (DIL_STEP * j + rho + 1) * per)
            q_scr[pl.ds(rho * quarter + j, per, stride=DIL_STEP), :] = a_scr[src, :] / jnp.where(
                head0, l_scr[0, src, :], l_scr[1, src, :])
    for rho in range(DIL_STEP):
        o_ref[pl.ds(rho, quarter, stride=DIL_STEP), :] = q_scr[rho * quarter:(rho + 1) * quarter, :]


def _attn_prompt(z3, slopes, keep):
    b, seq, _ = z3.shape
    win = pl.BlockSpec((None, PAIR, keep), lambda i, p: (i, p, 0))
    col = lambda off: pl.BlockSpec((None, seq, PAIR), lambda i, p: (i, 0, off + p))
    return pl.pallas_call(
        functools.partial(_attn_prompt_kernel, seq=seq, keep=keep),
        out_shape=(jax.ShapeDtypeStruct((b, seq, C_HEADS), F32),
                   jax.ShapeDtypeStruct((b, C_HEADS, keep), F32),
                   jax.ShapeDtypeStruct((b, C_HEADS, keep), F32)),
        grid=(b, N_PAIRS),
        in_specs=[pl.BlockSpec(memory_space=pltpu.SMEM),
                  col(0), col(N_PAIRS), col(2 * N_PAIRS)],
        out_specs=(pl.BlockSpec((None, seq, PAIR), lambda i, p: (i, 0, p)), win, win),
        scratch_shapes=[pltpu.VMEM((seq, PAIR), F32), pltpu.VMEM((seq, PAIR), BF),
                        pltpu.VMEM((seq, PAIR), BF), pltpu.VMEM((seq, PAIR), F32)]
        + [pltpu.VMEM((2, seq, PAIR), F32)] * 2
        + [pltpu.VMEM((2, 2 * Q_BLOCK, 2 * Q_BLOCK), F32)],
        compiler_params=_params("arbitrary", "arbitrary"),
        name="attn_prompt",
    )(slopes, z3, z3, z3)


def _attn_sample_kernel(sl_ref, q_ref, kn_ref, vn_ref, ck_ref, cv_ref, o_ref, ko_ref, vo_ref,
                        *, s_new, cache_len):
    lane_t = lax.broadcasted_iota(jnp.int32, (1, PAIR), 1)

    def shift_in(c_ref, new_ref, out_ref):
        rolled = pltpu.roll(c_ref[...], cache_len - s_new, 1)
        new_t = jnp.concatenate([new_ref[...], jnp.zeros((PAIR - s_new, C_HEADS), F32)], axis=0).T
        tail = jnp.where(lane_t >= PAIR - s_new, pltpu.roll(new_t, PAIR - s_new, 1),
                         rolled[:, cache_len - PAIR:cache_len])
        out_ref[:, 0:cache_len - PAIR] = rolled[:, 0:cache_len - PAIR]
        out_ref[:, cache_len - PAIR:cache_len] = tail

    shift_in(ck_ref, kn_ref, ko_ref)
    shift_in(cv_ref, vn_ref, vo_ref)

    rows = s_new * N_HEADS
    ri = lax.broadcasted_iota(jnp.int32, (rows, 1), 0)
    lane = lax.broadcasted_iota(jnp.int32, (1, C_HEADS), 1)
    own = (ri & (N_HEADS - 1)) == (lane >> 6)
    q = q_ref[...]
    qe = jnp.broadcast_to(q[:, None, :], (s_new, N_HEADS, C_HEADS)).reshape(rows, C_HEADS)
    qe = jnp.where(own, qe, 0.0)
    slope = sl_ref[:, 0:1]
    spos = ri >> 3

    def weights(dist):
        mult = jnp.zeros(dist.shape, F32)
        for window, dil in DILATIONS:
            hit = (dist >= 0) & (dist <= window) & ((dist & (dil - 1)) == 0)
            mult = mult + hit.astype(F32)
        return mult

    dist_c = cache_len + spos - lax.broadcasted_iota(jnp.int32, (1, cache_len), 1)
    dist_n = spos - lax.broadcasted_iota(jnp.int32, (1, s_new), 1)
    mult_c = weights(dist_c)
    mult_n = weights(dist_n)
    sc = _dot(qe, ck_ref[...]) * (HEAD_DIM ** -0.5) - slope * dist_c.astype(F32)
    sn = _dot_nt(qe, kn_ref[...]) * (HEAD_DIM ** -0.5) - slope * dist_n.astype(F32)
    sc = jnp.where(mult_c > 0, sc, NEG_BIG)
    sn = jnp.where(mult_n > 0, sn, NEG_BIG)
    mx = jnp.maximum(jnp.max(sc, axis=1, keepdims=True), jnp.max(sn, axis=1, keepdims=True))
    pc = mult_c * jnp.exp(sc - mx)
    pn = mult_n * jnp.exp(sn - mx)
    den = jnp.sum(pc, axis=1, keepdims=True) + jnp.sum(pn, axis=1, keepdims=True)
    num = _dot_nt(pc, cv_ref[...]) + _dot(pn, vn_ref[...])
    num = jnp.where(own, num, 0.0).reshape(s_new, N_HEADS, C_HEADS).sum(axis=1)
    den = jnp.where(own, den, 0.0).reshape(s_new, N_HEADS, C_HEADS).sum(axis=1)
    o_ref[...] = num / den


def _attn_sample(z3, cache_k, cache_v, slopes):
    b, s_new, _ = z3.shape
    cache_len = cache_k.shape[2]
    assert cache_len % PAIR == 0 and cache_len > PAIR and s_new < PAIR
    new = lambda c: pl.BlockSpec((None, s_new, C_HEADS), lambda i: (i, 0, c))
    cache = pl.BlockSpec((None, C_HEADS, cache_len), lambda i: (i, 0, 0))
    slope_rows = jnp.broadcast_to(jnp.tile(slopes, s_new)[:, None], (s_new * N_HEADS, PAIR))
    return pl.pallas_call(
        functools.partial(_attn_sample_kernel, s_new=s_new, cache_len=cache_len),
        out_shape=(jax.ShapeDtypeStruct((b, s_new, C_HEADS), F32),
                   jax.ShapeDtypeStruct(cache_k.shape, F32),
                   jax.ShapeDtypeStruct(cache_v.shape, F32)),
        grid=(b,),
        in_specs=[_const_spec((s_new * N_HEADS, PAIR)), new(0), new(1), new(2), cache, cache],
        out_specs=(new(0), cache, cache),
        compiler_params=_params("arbitrary"),
        name="attn_sample",
    )(slope_rows, z3, z3, z3, cache_k, cache_v)


def _rwkv_kernel(r_ref, k_ref, v_ref, wag_ref, pr_ref, pk_ref, pv_ref, pwag_ref,
                 sr_ref, sk_ref, sv_ref, swag_ref, m0_ref, vec_ref, muwag_ref, lora_ref, seg_ref,
                 o_ref, mout_ref, m_scr, y_scr, *, t_valid, nsb, chunk, per_chunk_state):
    tb = pl.program_id(1)
    first = tb == 0
    n_rows = nsb * SUPER
    shift = chunk.bit_length() - 1
    n_chunks = SUPER // chunk

    lane = lax.broadcasted_iota(jnp.int32, (1, PAIR), 1)
    head0 = lane < HEAD_DIM

    def stack(x):
        return jnp.concatenate([jnp.where(head0, x, 0.0), jnp.where(head0, 0.0, x)], axis=0)

    def transposed(x):
        n = x.shape[1]
        eye_n = (lax.broadcasted_iota(jnp.int32, (n, n), 0)
                 == lax.broadcasted_iota(jnp.int32, (n, n), 1)).astype(BF)
        hi, lo = _split(x)
        return _dot_nt(eye_n, hi) + _dot_nt(eye_n, lo)

    def states_in(ref):
        return transposed(ref[...].reshape(N_HEADS * HEAD_DIM, HEAD_DIM))

    def pair_state(t, p):
        return stack(t[:, p * PAIR:(p + 1) * PAIR])

    def pair_out(m_pair):
        return transposed(m_pair[0:HEAD_DIM, :] + m_pair[HEAD_DIM:PAIR, :])

    if not per_chunk_state:
        @pl.when(first)
        def _():
            t_in = states_in(m0_ref.at[0])
            for p in range(N_PAIRS):
                m_scr[p] = pair_state(t_in, p)

    rows = lax.broadcasted_iota(jnp.int32, (n_rows, 1), 0)

    def token_shift(cur_ref, prev_ref, carry_ref, mu):
        cur = cur_ref[...]
        if per_chunk_state:
            prev = prev_ref[...]
        else:
            last = jnp.where(first, carry_ref[...], prev_ref[7:8, :])
            prev = jnp.where(rows == 0, last, pltpu.roll(cur, 1, 0))
        return cur + (prev - cur) * mu

    vec = vec_ref[...]
    row = lambda i: vec[i:i + 1, :]
    zr = token_shift(r_ref, pr_ref, sr_ref, row(_V_MU_R))
    zk = token_shift(k_ref, pk_ref, sk_ref, row(_V_MU_K))
    zv = token_shift(v_ref, pv_ref, sv_ref, row(_V_MU_V))
    zwag = token_shift(wag_ref, pwag_ref, swag_ref, muwag_ref[...])
    wa = zwag[:, 0:D_LORA_WA]
    gi = zwag[:, D_LORA_WA:D_LORA_WA + D_G_LORA]

    wlin = row(_V_W0) + _dot(jnp.tanh(wa), lora_ref[0])
    softplus = jnp.maximum(-wlin, 0.0) + jnp.log(1.0 + jnp.exp(-jnp.abs(wlin)))
    w_log = -softplus - 0.5
    ld = -jnp.exp(w_log)
    lr = jax.nn.sigmoid(row(_V_A0) + _dot(wa, lora_ref[1]))
    gate = _dot(jax.nn.sigmoid(gi), lora_ref[2])
    seg = seg_ref[...]

    def head_sum(x):
        return jnp.concatenate([_dot_exact_rhs(x[:, i:i + SEG_W], seg)
                                for i in range(0, C_HEADS, SEG_W)], axis=1)

    kk = zk * row(_V_KK)
    kk = kk / jnp.maximum(jnp.sqrt(head_sum(kk * kk)), 1e-12)
    kmod = zk * (1.0 + (lr - 1.0) * row(_V_KA))
    vv = zv
    if t_valid is not None:
        live = ((rows & (chunk - 1)) if per_chunk_state else (rows + tb * n_rows)) < t_valid
        ld = jnp.where(live, ld, 0.0)
        kk = jnp.where(live, kk, 0.0)
        kmod = jnp.where(live, kmod, 0.0)
        vv = jnp.where(live, vv, 0.0)

    span = min(PAIR, n_rows)
    ti = lax.broadcasted_iota(jnp.int32, (span, span), 0)
    tj = lax.broadcasted_iota(jnp.int32, (span, span), 1)
    same_chunk = (ti >> shift) == (tj >> shift)
    sums = jnp.concatenate([same_chunk & (ti >= tj), same_chunk], axis=0).astype(BF)
    cum_tot = [_dot_exact_lhs(sums, ld[i:i + span, :]) for i in range(0, n_rows, span)]
    cum = jnp.concatenate([x[0:span, :] for x in cum_tot], axis=0)
    tot = jnp.concatenate([x[span:2 * span, :] for x in cum_tot], axis=0)
    dec_in = jnp.exp(cum)
    dec_ex = jnp.exp(cum - ld)
    dec_inv = jnp.exp(-cum)
    dec_end = jnp.exp(tot - cum)
    dec_tot = jnp.exp(tot)
    beta = kk * lr
    abar = -(kk * dec_ex)
    rbar = zr * dec_in
    bt = beta * dec_inv
    kt = kmod * dec_inv
    bh = beta * dec_end
    kh = kmod * dec_end

    ri = lax.broadcasted_iota(jnp.int32, (PAIR, PAIR), 0)
    ci = lax.broadcasted_iota(jnp.int32, (PAIR, PAIR), 1)
    same_blk = (ri >> shift) == (ci >> shift)
    strict = same_blk & (ri > ci)
    incl = same_blk & (ri >= ci)
    same_head = (ri >> 6) == (ci >> 6)
    eye = ri == ci
    zeros_sp = jnp.zeros((SUPER, PAIR), F32)
    zeros_pp = jnp.zeros((PAIR, PAIR), F32)

    def unstack(x):
        return x[0:SUPER, :] + x[SUPER:PAIR, :]

    def both_heads(x, keep):
        return jnp.where(keep, jnp.concatenate([x, x], axis=0), 0.0)

    eye_f = jnp.where(eye, 1.0, 0.0)
    in_chunk = [((lane & (SUPER - 1)) >> shift) == c for c in range(n_chunks)]

    def phase1(blocks, out):
        units = [(s, p) for s in blocks for p in range(N_PAIRS)]
        tile = lambda x, u: x[u[0] * SUPER:(u[0] + 1) * SUPER, u[1] * PAIR:(u[1] + 1) * PAIR]
        ab = [tile(abar, u) for u in units]
        rb = [tile(rbar, u) for u in units]
        v_p = [tile(vv, u) for u in units]
        v_s = [stack(x) for x in v_p]
        a_all = [_dot_nt(jnp.concatenate([a, r], axis=0),
                         jnp.concatenate([stack(tile(bt, u)), stack(tile(kt, u))], axis=0))
                 for a, r, u in zip(ab, rb, units)]
        yield
        n_ab = [both_heads(a[0:SUPER, 0:PAIR], strict) for a in a_all]
        a_ak = [both_heads(a[0:SUPER, PAIR:2 * PAIR], strict) for a in a_all]
        a_rbk = [jnp.concatenate([both_heads(a[SUPER:PAIR, 0:PAIR], incl),
                                  both_heads(a[SUPER:PAIR, PAIR:2 * PAIR], incl)], axis=1)
                 for a in a_all]
        tinv = [eye_f + n for n in n_ab]
        power = n_ab
        for _ in range(shift - 1):
            power = [_dot(x, x) for x in power]
            yield
            tinv = [t + _dot(x, t) for x, t in zip(power, tinv)]
            yield
        u_s = [_dot(a, v) for a, v in zip(a_ak, v_s)]
        yield
        ta = [_dot(t, jnp.concatenate([stack(a), u], axis=1))
              for t, a, u in zip(tinv, ab, u_s)]
        yield
        ry = [_dot(a, jnp.concatenate([t, jnp.concatenate([zeros_pp, v], axis=1)], axis=0))
              for a, t, v in zip(a_rbk, ta, v_s)]
        yield
        r1 = [unstack(stack(r) + y[:, 0:PAIR]) for r, y in zip(rb, ry)]
        y0 = [unstack(y[:, PAIR:2 * PAIR]) for y in ry]
        bk_t = [jnp.concatenate([tile(bh, u), tile(kh, u)], axis=0).astype(BF).T
                for u in units]
        rhs3 = [jnp.concatenate([jnp.concatenate([unstack(t[:, 0:PAIR]),
                                                  unstack(t[:, PAIR:2 * PAIR])], axis=1),
                                 jnp.concatenate([zeros_sp, v], axis=1)], axis=0)
                for t, v in zip(ta, v_p)]
        gh = [_dot(jnp.concatenate([jnp.where(in_chunk[c], b, jnp.zeros_like(b))
                                    for c in range(n_chunks)], axis=0), r)
              for b, r in zip(bk_t, rhs3)]
        for i, u in enumerate(units):
            out[u] = (r1[i], y0[i], gh[i])
        yield

    m = None if per_chunk_state else [m_scr[p] for p in range(N_PAIRS)]

    def phase2(blocks, fac):
        for s in blocks:
            for c in range(n_chunks):
                tok = slice(c * chunk, (c + 1) * chunk)
                r0 = s * SUPER + c * chunk
                seq_i = s * n_chunks + c
                t_in = states_in(m0_ref.at[seq_i]) if per_chunk_state else None
                for p in range(N_PAIRS):
                    r1, y0, gh = fac[(s, p)]
                    sl = slice(p * PAIR, (p + 1) * PAIR)
                    gh_c = gh[c * PAIR:(c + 1) * PAIR, :]
                    g_c = (jnp.where(eye, dec_tot[r0:r0 + 1, sl], 0.0)
                           + jnp.where(same_head, gh_c[:, 0:PAIR], 0.0))
                    h_c = jnp.where(same_head, gh_c[:, PAIR:2 * PAIR], 0.0)
                    m_in = pair_state(t_in, p) if per_chunk_state else m[p]
                    y_scr[r0:r0 + chunk, sl] = _dot(r1[tok, :], m_in) + y0[tok, :]
                    m_out = _dot(g_c, m_in) + h_c
                    if per_chunk_state:
                        mout_ref[seq_i, 2 * p:2 * p + 2] = pair_out(m_out).reshape(
                            2, HEAD_DIM, HEAD_DIM)
                    else:
                        m[p] = m_out
                yield

    groups = [list(range(g, min(g + RWKV_GROUP, nsb))) for g in range(0, nsb, RWKV_GROUP)]
    fac = {}
    for _ in phase1(groups[0], fac):
        pass
    for done, nxt in zip(groups, groups[1:]):
        steps = phase2(done, fac)
        for _ in phase1(nxt, fac):
            next(steps, None)
        for _ in steps:
            pass
    for _ in phase2(groups[-1], fac):
        pass
    if not per_chunk_state:
        for p in range(N_PAIRS):
            m_scr[p] = m[p]

    y = y_scr[...]
    mean = head_sum(y) * (1.0 / HEAD_DIM)
    yc = y - mean
    var = head_sum(yc * yc) * (1.0 / HEAD_DIM)
    yn = yc * lax.rsqrt(var + GN_EPS) * row(_V_LNW) + row(_V_LNB)
    bonus = head_sum(zr * kmod * row(_V_RK)) * zv
    o_ref[...] = (yn + bonus) * gate

    if not per_chunk_state:
        @pl.when(tb == pl.num_programs(1) - 1)
        def _():
            for p in range(N_PAIRS):
                mout_ref[0, 2 * p:2 * p + 2] = pair_out(m_scr[p]).reshape(2, HEAD_DIM, HEAD_DIM)


def _rwkv(z3, prev3, shift_prev, m0, wts, t_valid, nsb, chunk, per_chunk_state):
    b, seq, _ = z3.shape
    n_rows = nsb * SUPER
    assert seq % n_rows == 0 and SUPER % chunk == 0 and chunk % 8 == 0
    nt = seq // n_rows
    n_state = n_rows // chunk if per_chunk_state else 1
    assert m0.shape[0] == b * n_state and (nt == 1 or not per_chunk_state)
    cur = lambda w, c: pl.BlockSpec((None, n_rows, w), lambda i, t: (i, t, c))
    if per_chunk_state:
        prev = cur
    else:
        prev = lambda w, c: pl.BlockSpec(
            (None, 8, w), lambda i, t: (i, jnp.maximum(t * (n_rows // 8) - 1, 0), c))
    carry = lambda w: pl.BlockSpec((None, 1, w), lambda i, t: (i, 0, 0))
    state = pl.BlockSpec((n_state, N_HEADS, HEAD_DIM, HEAD_DIM), lambda i, t: (i, 0, 0, 0))
    wag_w = D_LORA_WA + D_G_LORA
    sp = shift_prev[:, None, :]
    all_valid = t_valid == (chunk if per_chunk_state else seq)
    return pl.pallas_call(
        functools.partial(_rwkv_kernel, t_valid=None if all_valid else t_valid, nsb=nsb,
                          chunk=chunk, per_chunk_state=per_chunk_state),
        out_shape=(jax.ShapeDtypeStruct((b, seq, C_HEADS), F32),
                   jax.ShapeDtypeStruct(m0.shape, F32)),
        grid=(b, nt),
        in_specs=[cur(C_HEADS, 3), cur(C_HEADS, 4), cur(C_HEADS, 5), cur(wag_w, 12),
                  prev(C_HEADS, 3), prev(C_HEADS, 4), prev(C_HEADS, 5), prev(wag_w, 12),
                  carry(C_HEADS), carry(C_HEADS), carry(C_HEADS), carry(wag_w),
                  state,
                  _const_spec((_V_ROWS, C_HEADS)), _const_spec((1, wag_w)),
                  _const_spec((3, PAIR, C_HEADS)), _const_spec((SEG_W, SEG_W))],
        out_specs=(pl.BlockSpec((None, n_rows, C_HEADS), lambda i, t: (i, t, 0)), state),
        scratch_shapes=[pltpu.VMEM((N_PAIRS, PAIR, PAIR), F32),
                        pltpu.VMEM((n_rows, C_HEADS), F32)],
        compiler_params=_params("arbitrary", "arbitrary"),
        name="rwkv",
    )(z3, z3, z3, z3, prev3, prev3, prev3, prev3,
      sp[:, :, 0:C_HEADS], sp[:, :, C_HEADS:2 * C_HEADS], sp[:, :, 2 * C_HEADS:3 * C_HEADS],
      sp[:, :, 3 * C_HEADS:], m0, wts["vec"], wts["mu_wag"], wts["lora"], wts["seg"])


def _pack_layer(w_in, mu_shift, w0, w2, a0, a2, g2, k_k, k_a, r_k, lnx_w, lnx_b, w_out, ln1_w,
                ln1_b, w_gate, w_up, w_down, ln2_w, ln2_b, w_ple_gate, w_ple_proj, ln3_w, ln3_b):
    mu_r, mu_k, mu_v = (mu_shift[i * C_HEADS:(i + 1) * C_HEADS] for i in range(3))
    vec = jnp.stack([mu_r, mu_k, mu_v, w0, a0, k_k, k_a, r_k.reshape(-1), lnx_w, lnx_b])
    vec = jnp.concatenate([vec, jnp.zeros((_V_ROWS - vec.shape[0], C_HEADS), F32)], axis=0)
    half = D_LORA_WA // 2
    zeros = jnp.zeros((half, C_HEADS), F32)
    lora = jnp.stack([jnp.concatenate([w2, zeros], axis=0),
                      jnp.concatenate([zeros, a2], axis=0),
                      g2]).astype(BF)
    head_of = jnp.arange(SEG_W) // HEAD_DIM
    seg = (head_of[:, None] == head_of[None, :]).astype(BF)
    ln = jnp.stack([ln1_w, ln1_b, ln2_w, ln2_b, ln3_w, ln3_b,
                    jnp.zeros_like(ln1_w), jnp.zeros_like(ln1_w)])

    return {
        "w_in": w_in.astype(BF), "vec": vec, "mu_wag": mu_shift[None, 3 * C_HEADS:],
        "lora": lora, "seg": seg, "w_out": w_out.astype(BF),
        "w_gate": w_gate.astype(BF), "w_up": w_up.astype(BF),
        "w_down": w_down.astype(BF),
        "w_ple_gate": w_ple_gate.astype(BF), "w_ple_proj": w_ple_proj.astype(BF), "ln": ln,
    }


def _alibi_slopes():
    h = jnp.arange(1, N_HEADS + 1, dtype=F32)
    return jnp.exp2(-8.0 * h / N_HEADS)


def _layer(x, p_l, wts, alpha, shift_prev, wkv_prev, cache_k=None, cache_v=None):
    b, seq, _ = x.shape
    x2d = x.reshape(b * seq, D_MODEL)
    z = _in_proj(x2d, wts["w_in"])
    z3 = z.reshape(b, seq, D_IN)
    slopes = _alibi_slopes()
    if cache_k is None:
        keep = min(WINDOW_MAX, seq)
        att, k_win, v_win = _attn_prompt(z3, slopes, keep)
        from_cm = lambda c: jnp.transpose(c.reshape(b, N_HEADS, HEAD_DIM, keep), (0, 3, 1, 2))
        k_win = from_cm(k_win)
        v_win = from_cm(v_win)
    else:
        cache_len = cache_k.shape[1]
        to_cm = lambda c: jnp.transpose(c, (0, 2, 3, 1)).reshape(b, C_HEADS, cache_len)
        from_cm = lambda c: jnp.transpose(c.reshape(b, N_HEADS, HEAD_DIM, cache_len), (0, 3, 1, 2))
        att, k_win, v_win = _attn_sample(z3, to_cm(cache_k), to_cm(cache_v), slopes)
        k_win = from_cm(k_win)
        v_win = from_cm(v_win)
    if seq % SUPER == 0:
        nsb = RWKV_BLOCKS if seq % (RWKV_BLOCKS * SUPER) == 0 else 1
        rw, m_last = _rwkv(z3, z3, shift_prev, wkv_prev, wts, seq, nsb, CHUNK,
                           False)
    else:
        chunk = 8
        per_block = SUPER // chunk
        assert seq <= chunk and b % per_block == 0
        first_prev = jnp.concatenate([jnp.zeros((b, 1, 3 * C_HEADS), F32), shift_prev[:, None, :]],
                                     axis=-1)
        prev = jnp.concatenate([first_prev, z3[:, :seq - 1]], axis=1)
        blocks = lambda a: jnp.pad(a, ((0, 0), (0, chunk - seq), (0, 0))).reshape(
            b // per_block, SUPER, D_IN)
        rw, m_last = _rwkv(blocks(z3), blocks(prev), jnp.zeros((b // per_block, D_B_IN), F32),
                           wkv_prev, wts, seq, 1, chunk, True)
        rw = rw.reshape(b, chunk, C_HEADS)[:, :seq]
    y = _post(att.reshape(b * seq, C_HEADS), rw.reshape(b * seq, C_HEADS), x2d,
              p_l.reshape(b * seq, PLE_DIM), wts, alpha)
    shift_new = z3[:, seq - 1, 3 * C_HEADS:]
    return y.reshape(b, seq, D_MODEL), k_win, v_win, shift_new, m_last


def kernel(x_prompt, x_sample, p_prompt, p_sample, cache_k_win, cache_v_win, state_wkv, state_shift, w_in, mu_shift, w0, w2, a0, a2, g2, k_k, k_a, r_k, lnx_w, lnx_b, w_out, ln1_w, ln1_b, w_gate, w_up, w_down, ln2_w, ln2_b, w_ple_gate, w_ple_proj, ln3_w, ln3_b):
    depth = w_in.shape[0]
    alpha = float((2 * depth) ** 0.25)
    xp, xs = x_prompt, x_sample
    bp = x_prompt.shape[0]
    shift0 = jnp.zeros((bp, D_B_IN), x_prompt.dtype)
    wkv0 = jnp.zeros((bp, N_HEADS, HEAD_DIM, HEAD_DIM), state_wkv.dtype)
    outs = [[] for _ in range(8)]
    for l in range(depth):
        wts = _pack_layer(w_in[l], mu_shift[l], w0[l], w2[l], a0[l], a2[l], g2[l], k_k[l], k_a[l],
                          r_k[l], lnx_w[l], lnx_b[l], w_out[l], ln1_w[l], ln1_b[l], w_gate[l],
                          w_up[l], w_down[l], ln2_w[l], ln2_b[l], w_ple_gate[l], w_ple_proj[l],
                          ln3_w[l], ln3_b[l])
        xp, kw, vw, sh, wk = _layer(xp, p_prompt[l], wts, alpha, shift0, wkv0)
        for lst, val in zip(outs[0:4], (kw, vw, wk, sh)):
            lst.append(val)
        xs, kw, vw, sh, wk = _layer(xs, p_sample[l], wts, alpha, state_shift[l], state_wkv[l],
                                    cache_k_win[l], cache_v_win[l])
        for lst, val in zip(outs[4:8], (kw, vw, wk, sh)):
            lst.append(val)
    return (xp, xs) + tuple(jnp.stack(o) for o in outs)
```

```python
import functools

import jax
import jax.numpy as jnp
from jax import lax
from jax.experimental import pallas as pl
from jax.experimental.pallas import tpu as pltpu

BF = jnp.bfloat16
F32 = jnp.float32

D_MODEL = 1024
HEAD_DIM = 64
N_HEADS = 8
C_HEADS = N_HEADS * HEAD_DIM
PAIR = 2 * HEAD_DIM
N_PAIRS = N_HEADS // 2
DILATIONS = ((128, 1), (512, 4), (2048, 16))
WINDOW_MAX = 2048
Q_BLOCK = 128
DIL_MAX = max(d for _, d in DILATIONS)
DIL_STEP = 4
D_LORA_WA = 128
D_G_LORA = 128
D_B_IN = 3 * C_HEADS + D_LORA_WA + D_G_LORA
D_IN = 3 * C_HEADS + D_B_IN
D_FF = 2816
PLE_DIM = 256
LN_EPS = 1e-5
GN_EPS = 64e-5
NEG_BIG = -1e30

CHUNK = 16
SUPER = 64
N_CHUNKS = SUPER // CHUNK
RWKV_BLOCKS = 8
RWKV_GROUP = 2
FF_CHUNK = 256
POST_PARTS = 2
SEG_W = 256
ATTN_UNROLL = 8
IN_CHUNK = 256
VMEM_LIMIT_BYTES = 56 * 1024 * 1024

(_V_MU_R, _V_MU_K, _V_MU_V, _V_W0, _V_A0, _V_KK, _V_KA, _V_RK, _V_LNW, _V_LNB) = range(10)
_V_ROWS = 16


def _params(*sem):
    return pltpu.CompilerParams(dimension_semantics=sem, vmem_limit_bytes=VMEM_LIMIT_BYTES)


def _const_spec(shape):
    nd = len(shape)
    return pl.BlockSpec(shape, lambda *_: (0,) * nd, pipeline_mode=pl.Buffered(1))


def _dot(a, b):
    return jnp.dot(a.astype(BF), b.astype(BF), preferred_element_type=F32)


def _dot_nt(a, b):
    return lax.dot_general(a.astype(BF), b.astype(BF), (((1,), (1,)), ((), ())),
                           preferred_element_type=F32)


def _split(x):
    hi = x.astype(BF)
    lo = (x - hi.astype(F32)).astype(BF)
    return hi, lo


def _dot_exact_lhs(mask_bf, x):
    hi, lo = _split(x)
    return (jnp.dot(mask_bf, hi, preferred_element_type=F32)
            + jnp.dot(mask_bf, lo, preferred_element_type=F32))


def _dot_exact_rhs(x, mask_bf):
    hi, lo = _split(x)
    return (jnp.dot(hi, mask_bf, preferred_element_type=F32)
            + jnp.dot(lo, mask_bf, preferred_element_type=F32))


def _layer_norm(x, w, b):
    mu = jnp.mean(x, axis=-1, keepdims=True)
    xc = x - mu
    var = jnp.mean(xc * xc, axis=-1, keepdims=True)
    return xc * lax.rsqrt(var + LN_EPS) * w + b


def _in_proj_kernel(x_ref, w_ref, o_ref):
    xb = x_ref[...].astype(BF)
    for c in range(D_IN // IN_CHUNK):
        o_ref[:, c * IN_CHUNK:(c + 1) * IN_CHUNK] = jnp.dot(xb, w_ref[:, c * IN_CHUNK:(c + 1) * IN_CHUNK],
                                                            preferred_element_type=F32)


def _in_proj(x2d, w_bf):
    m = x2d.shape[0]
    tm = min(1024, m)
    assert m % tm == 0
    return pl.pallas_call(
        _in_proj_kernel,
        out_shape=jax.ShapeDtypeStruct((m, D_IN), F32),
        grid=(m // tm,),
        in_specs=[pl.BlockSpec((tm, D_MODEL), lambda i: (i, 0)),
                  _const_spec((D_MODEL, D_IN))],
        out_specs=pl.BlockSpec((tm, D_IN), lambda i: (i, 0)),
        compiler_params=_params("arbitrary"),
        name="in_proj",
    )(x2d, w_bf)


def _post_kernel(att_ref, rw_ref, x_ref, p_ref, wo_ref, wg_ref, wu_ref, wd_ref, wpg_ref,
                 wpp_ref, ln_ref, o_ref, *, alpha):
    part = x_ref.shape[0] // POST_PARTS
    parts = [slice(i * part, (i + 1) * part) for i in range(POST_PARTS)]
    mix = [_dot(att_ref[r, :], wo_ref[0:C_HEADS, :])
           + _dot(rw_ref[r, :], wo_ref[C_HEADS:2 * C_HEADS, :]) for r in parts]
    h = jnp.concatenate([_layer_norm(alpha * x_ref[r, :] + m, ln_ref[0:1, :], ln_ref[1:2, :])
                         for r, m in zip(parts, mix)], axis=0)
    hb = h.astype(BF)
    ffn = jnp.zeros_like(h)
    for c in range(D_FF // FF_CHUNK):
        cols = slice(c * FF_CHUNK, (c + 1) * FF_CHUNK)
        g = jnp.dot(hb, wg_ref[:, cols], preferred_element_type=F32)
        u = jnp.dot(hb, wu_ref[:, cols], preferred_element_type=F32)
        act = g * jax.nn.sigmoid(g) * u
        ffn = ffn + jnp.dot(act.astype(BF), wd_ref[cols, :],
                            preferred_element_type=F32)
    emb = _dot(p_ref[...], wpp_ref[...])
    h2 = [_layer_norm(alpha * h[r, :] + ffn[r, :], ln_ref[2:3, :], ln_ref[3:4, :]) for r in parts]
    ple = [jax.nn.sigmoid(_dot(hh, wpg_ref[...])) * emb[r, :] for r, hh in zip(parts, h2)]
    for r, hh, e in zip(parts, h2, ple):
        o_ref[r, :] = _layer_norm(alpha * hh + e, ln_ref[4:5, :], ln_ref[5:6, :])


def _post(att, rw, x2d, p2d, wts, alpha):
    m = x2d.shape[0]
    tm = min(512, m)
    assert m % tm == 0
    row = lambda w: pl.BlockSpec((tm, w), lambda i: (i, 0))
    return pl.pallas_call(
        functools.partial(_post_kernel, alpha=alpha),
        out_shape=jax.ShapeDtypeStruct((m, D_MODEL), F32),
        grid=(m // tm,),
        in_specs=[row(C_HEADS), row(C_HEADS), row(D_MODEL), row(PLE_DIM),
                  _const_spec((2 * C_HEADS, D_MODEL)),
                  _const_spec((D_MODEL, D_FF)), _const_spec((D_MODEL, D_FF)),
                  _const_spec((D_FF, D_MODEL)),
                  _const_spec((D_MODEL, D_MODEL)), _const_spec((PLE_DIM, D_MODEL)),
                  _const_spec((8, D_MODEL))],
        out_specs=row(D_MODEL),
        compiler_params=_params("arbitrary"),
        name="post",
    )(att, rw, x2d, p2d, wts["w_out"], wts["w_gate"], wts["w_up"], wts["w_down"],
      wts["w_ple_gate"], wts["w_ple_proj"], wts["ln"])


def _attn_prompt_kernel(sl_ref, q_ref, k_ref, v_ref, o_ref, kw_ref, vw_ref, q_scr, k_scr, v_scr,
                        a_scr, m_scr, l_scr, mb_scr, *, seq, keep):
    pair = pl.program_id(1)
    slopes = (sl_ref[2 * pair], sl_ref[2 * pair + 1])
    lane = lax.broadcasted_iota(jnp.int32, (1, PAIR), 1)
    head0 = lane < HEAD_DIM
    kw_ref[...] = k_ref[seq - keep:seq, :].T
    vw_ref[...] = v_ref[seq - keep:seq, :].T
    qi = lax.broadcasted_iota(jnp.int32, (Q_BLOCK, 1), 0)
    ki = lax.broadcasted_iota(jnp.int32, (1, 2 * Q_BLOCK), 1)

    per = seq // DIL_MAX
    quarter = seq // DIL_STEP

    def to_residue_major(src_ref, dst_ref, convert):
        for rho in range(DIL_STEP):
            a_scr[rho * quarter:(rho + 1) * quarter, :] = src_ref[
                pl.ds(rho, quarter, stride=DIL_STEP), :]
        for rho in range(DIL_STEP):
            for j in range(DIL_MAX // DIL_STEP):
                r = DIL_STEP * j + rho
                dst_ref[r * per:(r + 1) * per, :] = convert(
                    a_scr[pl.ds(rho * quarter + j, per, stride=DIL_STEP), :])

    to_residue_major(q_ref, q_scr, lambda x: x * (HEAD_DIM ** -0.5))
    to_residue_major(k_ref, k_scr, lambda x: x.astype(BF))
    to_residue_major(v_ref, v_scr, lambda x: x.astype(BF))

    def gather(ref, runs):
        return jnp.concatenate([ref[rr, :] for rr in runs], axis=0)

    for window, dil in DILATIONS:
        n_runs = DIL_MAX // dil
        w = Q_BLOCK // n_runs
        nb = seq // dil // Q_BLOCK
        n_units = nb * dil
        assert nb >= 2 and n_units * Q_BLOCK == seq and n_units % ATTN_UNROLL == 0 and w % 8 == 0
        tq = (qi & (w - 1)) * DIL_MAX + (qi >> (w.bit_length() - 1)) * dil
        tk = (ki & (2 * w - 1)) * DIL_MAX + (ki >> (w.bit_length())) * dil
        if dil == 1:
            tk = ki
        for case in range(2):
            dist = case * Q_BLOCK * dil + tq - tk
            valid = (dist >= 0) & (dist <= window)
            distf = dist.astype(F32)
            mb_scr[case] = jnp.concatenate(
                [jnp.where(valid, -slopes[e] * distf, NEG_BIG) for e in range(2)], axis=0)

        def group(g, carry, dil=dil, n_runs=n_runs, w=w, first=(window, dil) == DILATIONS[0]):
            loaded = []
            for j in range(ATTN_UNROLL):
                u = g * ATTN_UNROLL + j
                n = u >> (dil.bit_length() - 1)
                rho = u & (dil - 1)
                nprev = jnp.maximum(n - 1, 0)
                base = [(dil * jj + rho) * per for jj in range(n_runs)]
                qruns = [pl.ds(pl.multiple_of(b + w * n, 8), w) for b in base]
                q = gather(q_scr, qruns)
                qq = jnp.concatenate([jnp.where(head0, q, 0.0), jnp.where(head0, 0.0, q)], axis=0)
                if dil == 1:
                    keys = pl.ds(pl.multiple_of(nprev * Q_BLOCK, Q_BLOCK), 2 * Q_BLOCK)
                    k = k_ref[keys, :].astype(BF)
                    v = v_ref[keys, :].astype(BF)
                else:
                    kruns = [pl.ds(pl.multiple_of(b + w * nprev, 16), 2 * w) for b in base]
                    k = gather(k_scr, kruns)
                    v = gather(v_scr, kruns)
                old = None if first else (
                    jnp.concatenate([gather(m_scr.at[0], qruns), gather(m_scr.at[1], qruns)], axis=0),
                    jnp.concatenate([gather(l_scr.at[0], qruns), gather(l_scr.at[1], qruns)], axis=0),
                    gather(a_scr, qruns))
                loaded.append((qruns, jnp.minimum(n, 1), qq.astype(BF), k,
                               jnp.concatenate([v, jnp.ones_like(v)], axis=1),
                               old))
            scores = [_dot_nt(qq, k) + mb_scr[case] for (_, case, qq, k, _, _) in loaded]
            stats = []
            for (_, _, _, _, _, old), s in zip(loaded, scores):
                m_new = jnp.max(s, axis=1, keepdims=True)
                m_new = (jnp.broadcast_to(m_new, (2 * Q_BLOCK, PAIR)) if first
                         else jnp.maximum(old[0], m_new))
                p = jnp.exp(s - jnp.concatenate([m_new, m_new], axis=1)).astype(BF)
                stats.append((m_new, p, None if first else jnp.exp(old[0] - m_new)))
            pvs = [jnp.dot(p, vo, preferred_element_type=F32)
                   for (_, _, _, _, vo, _), (_, p, _) in zip(loaded, stats)]
            for (qruns, _, _, _, _, old), (m_new, _, al), pv in zip(loaded, stats, pvs):
                l_new = pv[:, PAIR:2 * PAIR]
                a_new = (pv[0:Q_BLOCK, 0:PAIR], pv[Q_BLOCK:, 0:PAIR])
                if not first:
                    l_new = al * old[1] + l_new
                    a_new = (al[0:Q_BLOCK, :] * old[2] + a_new[0], al[Q_BLOCK:, :] * old[2] + a_new[1])
                a_new = jnp.where(head0, a_new[0], a_new[1])
                for jj, rr in enumerate(qruns):
                    a_scr[rr, :] = a_new[jj * w:(jj + 1) * w, :]
                    for e in range(2):
                        m_scr[e, rr, :] = m_new[e * Q_BLOCK + jj * w:e * Q_BLOCK + (jj + 1) * w, :]
                        l_scr[e, rr, :] = l_new[e * Q_BLOCK + jj * w:e * Q_BLOCK + (jj + 1) * w, :]
            return carry

        lax.fori_loop(0, n_units // ATTN_UNROLL, group, 0)

    for rho in range(DIL_STEP):
        for j in range(DIL_MAX // DIL_STEP):
            src = slice((DIL_STEP * j + rho) * per, (DIL_STEP * j + rho + 1) * per)
            q_scr[pl.ds(rho * quarter + j, per, stride=DIL_STEP), :] = a_scr[src, :] / jnp.where(
                head0, l_scr[0, src, :], l_scr[1, src, :])
    for rho in range(DIL_STEP):
        o_ref[pl.ds(rho, quarter, stride=DIL_STEP), :] = q_scr[rho * quarter:(rho + 1) * quarter, :]


def _attn_prompt(z3, slopes, keep):
    b, seq, _ = z3.shape
    win = pl.BlockSpec((None, PAIR, keep), lambda i, p: (i, p, 0))
    col = lambda off: pl.BlockSpec((None, seq, PAIR), lambda i, p: (i, 0, off + p))
    return pl.pallas_call(
        functools.partial(_attn_prompt_kernel, seq=seq, keep=keep),
        out_shape=(jax.ShapeDtypeStruct((b, seq, C_HEADS), F32),
                   jax.ShapeDtypeStruct((b, C_HEADS, keep), F32),
                   jax.ShapeDtypeStruct((b, C_HEADS, keep), F32)),
        grid=(b, N_PAIRS),
        in_specs=[pl.BlockSpec(memory_space=pltpu.SMEM),
                  col(0), col(N_PAIRS), col(2 * N_PAIRS)],
        out_specs=(pl.BlockSpec((None, seq, PAIR), lambda i, p: (i, 0, p)), win, win),
        scratch_shapes=[pltpu.VMEM((seq, PAIR), F32), pltpu.VMEM((seq, PAIR), BF),
                        pltpu.VMEM((seq, PAIR), BF), pltpu.VMEM((seq, PAIR), F32)]
        + [pltpu.VMEM((2, seq, PAIR), F32)] * 2
        + [pltpu.VMEM((2, 2 * Q_BLOCK, 2 * Q_BLOCK), F32)],
        compiler_params=_params("arbitrary", "arbitrary"),
        name="attn_prompt",
    )(slopes, z3, z3, z3)


def _attn_sample_kernel(sl_ref, q_ref, kn_ref, vn_ref, ck_ref, cv_ref, o_ref, ko_ref, vo_ref,
                        *, s_new, cache_len):
    lane_t = lax.broadcasted_iota(jnp.int32, (1, PAIR), 1)

    def shift_in(c_ref, new_ref, out_ref):
        rolled = pltpu.roll(c_ref[...], cache_len - s_new, 1)
        new_t = jnp.concatenate([new_ref[...], jnp.zeros((PAIR - s_new, C_HEADS), F32)], axis=0).T
        tail = jnp.where(lane_t >= PAIR - s_new, pltpu.roll(new_t, PAIR - s_new, 1),
                         rolled[:, cache_len - PAIR:cache_len])
        out_ref[:, 0:cache_len - PAIR] = rolled[:, 0:cache_len - PAIR]
        out_ref[:, cache_len - PAIR:cache_len] = tail

    shift_in(ck_ref, kn_ref, ko_ref)
    shift_in(cv_ref, vn_ref, vo_ref)

    rows = s_new * N_HEADS
    ri = lax.broadcasted_iota(jnp.int32, (rows, 1), 0)
    lane = lax.broadcasted_iota(jnp.int32, (1, C_HEADS), 1)
    own = (ri & (N_HEADS - 1)) == (lane >> 6)
    q = q_ref[...]
    qe = jnp.broadcast_to(q[:, None, :], (s_new, N_HEADS, C_HEADS)).reshape(rows, C_HEADS)
    qe = jnp.where(own, qe, 0.0)
    slope = sl_ref[:, 0:1]
    spos = ri >> 3

    def weights(dist):
        mult = jnp.zeros(dist.shape, F32)
        for window, dil in DILATIONS:
            hit = (dist >= 0) & (dist <= window) & ((dist & (dil - 1)) == 0)
            mult = mult + hit.astype(F32)
        return mult

    dist_c = cache_len + spos - lax.broadcasted_iota(jnp.int32, (1, cache_len), 1)
    dist_n = spos - lax.broadcasted_iota(jnp.int32, (1, s_new), 1)
    mult_c = weights(dist_c)
    mult_n = weights(dist_n)
    sc = _dot(qe, ck_ref[...]) * (HEAD_DIM ** -0.5) - slope * dist_c.astype(F32)
    sn = _dot_nt(qe, kn_ref[...]) * (HEAD_DIM ** -0.5) - slope * dist_n.astype(F32)
    sc = jnp.where(mult_c > 0, sc, NEG_BIG)
    sn = jnp.where(mult_n > 0, sn, NEG_BIG)
    mx = jnp.maximum(jnp.max(sc, axis=1, keepdims=True), jnp.max(sn, axis=1, keepdims=True))
    pc = mult_c * jnp.exp(sc - mx)
    pn = mult_n * jnp.exp(sn - mx)
    den = jnp.sum(pc, axis=1, keepdims=True) + jnp.sum(pn, axis=1, keepdims=True)
    num = _dot_nt(pc, cv_ref[...]) + _dot(pn, vn_ref[...])
    num = jnp.where(own, num, 0.0).reshape(s_new, N_HEADS, C_HEADS).sum(axis=1)
    den = jnp.where(own, den, 0.0).reshape(s_new, N_HEADS, C_HEADS).sum(axis=1)
    o_ref[...] = num / den


def _attn_sample(z3, cache_k, cache_v, slopes):
    b, s_new, _ = z3.shape
    cache_len = cache_k.shape[2]
    assert cache_len % PAIR == 0 and cache_len > PAIR and s_new < PAIR
    new = lambda c: pl.BlockSpec((None, s_new, C_HEADS), lambda i: (i, 0, c))
    cache = pl.BlockSpec((None, C_HEADS, cache_len), lambda i: (i, 0, 0))
    slope_rows = jnp.broadcast_to(jnp.tile(slopes, s_new)[:, None], (s_new * N_HEADS, PAIR))
    return pl.pallas_call(
        functools.partial(_attn_sample_kernel, s_new=s_new, cache_len=cache_len),
        out_shape=(jax.ShapeDtypeStruct((b, s_new, C_HEADS), F32),
                   jax.ShapeDtypeStruct(cache_k.shape, F32),
                   jax.ShapeDtypeStruct(cache_v.shape, F32)),
        grid=(b,),
        in_specs=[_const_spec((s_new * N_HEADS, PAIR)), new(0), new(1), new(2), cache, cache],
        out_specs=(new(0), cache, cache),
        compiler_params=_params("arbitrary"),
        name="attn_sample",
    )(slope_rows, z3, z3, z3, cache_k, cache_v)


def _rwkv_kernel(r_ref, k_ref, v_ref, wag_ref, pr_ref, pk_ref, pv_ref, pwag_ref,
                 sr_ref, sk_ref, sv_ref, swag_ref, m0_ref, vec_ref, muwag_ref, lora_ref, seg_ref,
                 o_ref, mout_ref, m_scr, y_scr, *, t_valid, nsb, chunk, per_chunk_state):
    tb = pl.program_id(1)
    first = tb == 0
    n_rows = nsb * SUPER
    shift = chunk.bit_length() - 1
    n_chunks = SUPER // chunk

    lane = lax.broadcasted_iota(jnp.int32, (1, PAIR), 1)
    head0 = lane < HEAD_DIM

    def stack(x):
        return jnp.concatenate([jnp.where(head0, x, 0.0), jnp.where(head0, 0.0, x)], axis=0)

    def transposed(x):
        n = x.shape[1]
        eye_n = (lax.broadcasted_iota(jnp.int32, (n, n), 0)
                 == lax.broadcasted_iota(jnp.int32, (n, n), 1)).astype(BF)
        hi, lo = _split(x)
        return _dot_nt(eye_n, hi) + _dot_nt(eye_n, lo)

    def states_in(ref):
        return transposed(ref[...].reshape(N_HEADS * HEAD_DIM, HEAD_DIM))

    def pair_state(t, p):
        return stack(t[:, p * PAIR:(p + 1) * PAIR])

    def pair_out(m_pair):
        return transposed(m_pair[0:HEAD_DIM, :] + m_pair[HEAD_DIM:PAIR, :])

    if not per_chunk_state:
        @pl.when(first)
        def _():
            t_in = states_in(m0_ref.at[0])
            for p in range(N_PAIRS):
                m_scr[p] = pair_state(t_in, p)

    rows = lax.broadcasted_iota(jnp.int32, (n_rows, 1), 0)

    def token_shift(cur_ref, prev_ref, carry_ref, mu):
        cur = cur_ref[...]
        if per_chunk_state:
            prev = prev_ref[...]
        else:
            last = jnp.where(first, carry_ref[...], prev_ref[7:8, :])
            prev = jnp.where(rows == 0, last, pltpu.roll(cur, 1, 0))
        return cur + (prev - cur) * mu

    vec = vec_ref[...]
    row = lambda i: vec[i:i + 1, :]
    zr = token_shift(r_ref, pr_ref, sr_ref, row(_V_MU_R))
    zk = token_shift(k_ref, pk_ref, sk_ref, row(_V_MU_K))
    zv = token_shift(v_ref, pv_ref, sv_ref, row(_V_MU_V))
    zwag = token_shift(wag_ref, pwag_ref, swag_ref, muwag_ref[...])
    wa = zwag[:, 0:D_LORA_WA]
    gi = zwag[:, D_LORA_WA:D_LORA_WA + D_G_LORA]

    wlin = row(_V_W0) + _dot(jnp.tanh(wa), lora_ref[0])
    softplus = jnp.maximum(-wlin, 0.0) + jnp.log(1.0 + jnp.exp(-jnp.abs(wlin)))
    w_log = -softplus - 0.5
    ld = -jnp.exp(w_log)
    lr = jax.nn.sigmoid(row(_V_A0) + _dot(wa, lora_ref[1]))
    gate = _dot(jax.nn.sigmoid(gi), lora_ref[2])
    seg = seg_ref[...]

    def head_sum(x):
        return jnp.concatenate([_dot_exact_rhs(x[:, i:i + SEG_W], seg)
                                for i in range(0, C_HEADS, SEG_W)], axis=1)

    kk = zk * row(_V_KK)
    kk = kk / jnp.maximum(jnp.sqrt(head_sum(kk * kk)), 1e-12)
    kmod = zk * (1.0 + (lr - 1.0) * row(_V_KA))
    vv = zv
    if t_valid is not None:
        live = ((rows & (chunk - 1)) if per_chunk_state else (rows + tb * n_rows)) < t_valid
        ld = jnp.where(live, ld, 0.0)
        kk = jnp.where(live, kk, 0.0)
        kmod = jnp.where(live, kmod, 0.0)
        vv = jnp.where(live, vv, 0.0)
    bonus = head_sum(zr * kmod * row(_V_RK)) * zv

    span = min(PAIR, n_rows)
    ti = lax.broadcasted_iota(jnp.int32, (span, span), 0)
    tj = lax.broadcasted_iota(jnp.int32, (span, span), 1)
    same_chunk = (ti >> shift) == (tj >> shift)
    sums = jnp.concatenate([same_chunk & (ti >= tj), same_chunk], axis=0).astype(BF)
    cum_tot = [_dot_exact_lhs(sums, ld[i:i + span, :]) for i in range(0, n_rows, span)]
    cum = jnp.concatenate([x[0:span, :] for x in cum_tot], axis=0)
    tot = jnp.concatenate([x[span:2 * span, :] for x in cum_tot], axis=0)
    dec_in = jnp.exp(cum)
    dec_ex = jnp.exp(cum - ld)
    dec_inv = jnp.exp(-cum)
    dec_end = jnp.exp(tot - cum)
    dec_tot = jnp.exp(tot)
    beta = kk * lr
    abar = -(kk * dec_ex)
    rbar = zr * dec_in
    bt = beta * dec_inv
    kt = kmod * dec_inv
    bh = beta * dec_end
    kh = kmod * dec_end

    ri = lax.broadcasted_iota(jnp.int32, (PAIR, PAIR), 0)
    ci = lax.broadcasted_iota(jnp.int32, (PAIR, PAIR), 1)
    same_blk = (ri >> shift) == (ci >> shift)
    strict = same_blk & (ri > ci)
    incl = same_blk & (ri >= ci)
    same_head = (ri >> 6) == (ci >> 6)
    eye = ri == ci
    zeros_sp = jnp.zeros((SUPER, PAIR), F32)
    zeros_pp = jnp.zeros((PAIR, PAIR), F32)

    def unstack(x):
        return x[0:SUPER, :] + x[SUPER:PAIR, :]

    def both_heads(x, keep):
        return jnp.where(keep, jnp.concatenate([x, x], axis=0), 0.0)

    eye_f = jnp.where(eye, 1.0, 0.0)
    in_chunk = [((lane & (SUPER - 1)) >> shift) == c for c in range(n_chunks)]

    def phase1(blocks, out):
        units = [(s, p) for s in blocks for p in range(N_PAIRS)]
        tile = lambda x, u: x[u[0] * SUPER:(u[0] + 1) * SUPER, u[1] * PAIR:(u[1] + 1) * PAIR]
        ab = [tile(abar, u) for u in units]
        rb = [tile(rbar, u) for u in units]
        v_p = [tile(vv, u) for u in units]
        v_s = [stack(x) for x in v_p]
        a_all = [_dot_nt(jnp.concatenate([a, r], axis=0),
                         jnp.concatenate([stack(tile(bt, u)), stack(tile(kt, u))], axis=0))
                 for a, r, u in zip(ab, rb, units)]
        yield
        n_ab = [both_heads(a[0:SUPER, 0:PAIR], strict) for a in a_all]
        a_ak = [both_heads(a[0:SUPER, PAIR:2 * PAIR], strict) for a in a_all]
        a_rbk = [jnp.concatenate([both_heads(a[SUPER:PAIR, 0:PAIR], incl),
                                  both_heads(a[SUPER:PAIR, PAIR:2 * PAIR], incl)], axis=1)
                 for a in a_all]
        tinv = [eye_f + n for n in n_ab]
        power = n_ab
        for _ in range(shift - 1):
            power = [_dot(x, x) for x in power]
            yield
            tinv = [t + _dot(x, t) for x, t in zip(power, tinv)]
            yield
        u_s = [_dot(a, v) for a, v in zip(a_ak, v_s)]
        yield
        ta = [_dot(t, jnp.concatenate([stack(a), u], axis=1))
              for t, a, u in zip(tinv, ab, u_s)]
        yield
        ry = [_dot(a, jnp.concatenate([t, jnp.concatenate([zeros_pp, v], axis=1)], axis=0))
              for a, t, v in zip(a_rbk, ta, v_s)]
        yield
        r1 = [unstack(stack(r) + y[:, 0:PAIR]) for r, y in zip(rb, ry)]
        y0 = [unstack(y[:, PAIR:2 * PAIR]) for y in ry]
        bk_t = [jnp.concatenate([tile(bh, u), tile(kh, u)], axis=0).astype(BF).T
                for u in units]
        rhs3 = [jnp.concatenate([jnp.concatenate([unstack(t[:, 0:PAIR]),
                                                  unstack(t[:, PAIR:2 * PAIR])], axis=1),
                                 jnp.concatenate([zeros_sp, v], axis=1)], axis=0)
                for t, v in zip(ta, v_p)]
        gh = [_dot(jnp.concatenate([jnp.where(in_chunk[c], b, jnp.zeros_like(b))
                                    for c in range(n_chunks)], axis=0), r)
              for b, r in zip(bk_t, rhs3)]
        for i, u in enumerate(units):
            out[u] = (r1[i], y0[i], gh[i])
        yield

    m = None if per_chunk_state else [m_scr[p] for p in range(N_PAIRS)]

    def phase2(blocks, fac):
        for s in blocks:
            for c in range(n_chunks):
                tok = slice(c * chunk, (c + 1) * chunk)
                r0 = s * SUPER + c * chunk
                seq_i = s * n_chunks + c
                t_in = states_in(m0_ref.at[seq_i]) if per_chunk_state else None
                for p in range(N_PAIRS):
                    r1, y0, gh = fac[(s, p)]
                    sl = slice(p * PAIR, (p + 1) * PAIR)
                    gh_c = gh[c * PAIR:(c + 1) * PAIR, :]
                    g_c = (jnp.where(eye, dec_tot[r0:r0 + 1, sl], 0.0)
                           + jnp.where(same_head, gh_c[:, 0:PAIR], 0.0))
                    h_c = jnp.where(same_head, gh_c[:, PAIR:2 * PAIR], 0.0)
                    m_in = pair_state(t_in, p) if per_chunk_state else m[p]
                    y_scr[r0:r0 + chunk, sl] = _dot(r1[tok, :], m_in) + y0[tok, :]
                    m_out = _dot(g_c, m_in) + h_c
                    if per_chunk_state:
                        mout_ref[seq_i, 2 * p:2 * p + 2] = pair_out(m_out).reshape(
                            2, HEAD_DIM, HEAD_DIM)
                    else:
                        m[p] = m_out
                yield

    groups = [list(range(g, min(g + RWKV_GROUP, nsb))) for g in range(0, nsb, RWKV_GROUP)]
    fac = {}
    for _ in phase1(groups[0], fac):
        pass
    for done, nxt in zip(groups, groups[1:]):
        steps = phase2(done, fac)
        for _ in phase1(nxt, fac):
            next(steps, None)
        for _ in steps:
            pass
    for _ in phase2(groups[-1], fac):
        pass
    if not per_chunk_state:
        for p in range(N_PAIRS):
            m_scr[p] = m[p]

    y = y_scr[...]
    mean = head_sum(y) * (1.0 / HEAD_DIM)
    yc = y - mean
    var = head_sum(yc * yc) * (1.0 / HEAD_DIM)
    yn = yc * lax.rsqrt(var + GN_EPS) * row(_V_LNW) + row(_V_LNB)
    o_ref[...] = (yn + bonus) * gate

    if not per_chunk_state:
        @pl.when(tb == pl.num_programs(1) - 1)
        def _():
            for p in range(N_PAIRS):
                mout_ref[0, 2 * p:2 * p + 2] = pair_out(m_scr[p]).reshape(2, HEAD_DIM, HEAD_DIM)


def _rwkv(z3, prev3, shift_prev, m0, wts, t_valid, nsb, chunk, per_chunk_state):
    b, seq, _ = z3.shape
    n_rows = nsb * SUPER
    assert seq % n_rows == 0 and SUPER % chunk == 0 and chunk % 8 == 0
    nt = seq // n_rows
    n_state = n_rows // chunk if per_chunk_state else 1
    assert m0.shape[0] == b * n_state and (nt == 1 or not per_chunk_state)
    cur = lambda w, c: pl.BlockSpec((None, n_rows, w), lambda i, t: (i, t, c))
    if per_chunk_state:
        prev = cur
    else:
        prev = lambda w, c: pl.BlockSpec(
            (None, 8, w), lambda i, t: (i, jnp.maximum(t * (n_rows // 8) - 1, 0), c))
    carry = lambda w: pl.BlockSpec((None, 1, w), lambda i, t: (i, 0, 0))
    state = pl.BlockSpec((n_state, N_HEADS, HEAD_DIM, HEAD_DIM), lambda i, t: (i, 0, 0, 0))
    wag_w = D_LORA_WA + D_G_LORA
    sp = shift_prev[:, None, :]
    all_valid = t_valid == (chunk if per_chunk_state else seq)
    return pl.pallas_call(
        functools.partial(_rwkv_kernel, t_valid=None if all_valid else t_valid, nsb=nsb,
                          chunk=chunk, per_chunk_state=per_chunk_state),
        out_shape=(jax.ShapeDtypeStruct((b, seq, C_HEADS), F32),
                   jax.ShapeDtypeStruct(m0.shape, F32)),
        grid=(b, nt),
        in_specs=[cur(C_HEADS, 3), cur(C_HEADS, 4), cur(C_HEADS, 5), cur(wag_w, 12),
                  prev(C_HEADS, 3), prev(C_HEADS, 4), prev(C_HEADS, 5), prev(wag_w, 12),
                  carry(C_HEADS), carry(C_HEADS), carry(C_HEADS), carry(wag_w),
                  state,
                  _const_spec((_V_ROWS, C_HEADS)), _const_spec((1, wag_w)),
                  _const_spec((3, PAIR, C_HEADS)), _const_spec((SEG_W, SEG_W))],
        out_specs=(pl.BlockSpec((None, n_rows, C_HEADS), lambda i, t: (i, t, 0)), state),
        scratch_shapes=[pltpu.VMEM((N_PAIRS, PAIR, PAIR), F32),
                        pltpu.VMEM((n_rows, C_HEADS), F32)],
        compiler_params=_params("arbitrary", "arbitrary"),
        name="rwkv",
    )(z3, z3, z3, z3, prev3, prev3, prev3, prev3,
      sp[:, :, 0:C_HEADS], sp[:, :, C_HEADS:2 * C_HEADS], sp[:, :, 2 * C_HEADS:3 * C_HEADS],
      sp[:, :, 3 * C_HEADS:], m0, wts["vec"], wts["mu_wag"], wts["lora"], wts["seg"])


def _pack_layer(w_in, mu_shift, w0, w2, a0, a2, g2, k_k, k_a, r_k, lnx_w, lnx_b, w_out, ln1_w,
                ln1_b, w_gate, w_up, w_down, ln2_w, ln2_b, w_ple_gate, w_ple_proj, ln3_w, ln3_b):
    mu_r, mu_k, mu_v = (mu_shift[i * C_HEADS:(i + 1) * C_HEADS] for i in range(3))
    vec = jnp.stack([mu_r, mu_k, mu_v, w0, a0, k_k, k_a, r_k.reshape(-1), lnx_w, lnx_b])
    vec = jnp.concatenate([vec, jnp.zeros((_V_ROWS - vec.shape[0], C_HEADS), F32)], axis=0)
    half = D_LORA_WA // 2
    zeros = jnp.zeros((half, C_HEADS), F32)
    lora = jnp.stack([jnp.concatenate([w2, zeros], axis=0),
                      jnp.concatenate([zeros, a2], axis=0),
                      g2]).astype(BF)
    head_of = jnp.arange(SEG_W) // HEAD_DIM
    seg = (head_of[:, None] == head_of[None, :]).astype(BF)
    ln = jnp.stack([ln1_w, ln1_b, ln2_w, ln2_b, ln3_w, ln3_b,
                    jnp.zeros_like(ln1_w), jnp.zeros_like(ln1_w)])

    return {
        "w_in": w_in.astype(BF), "vec": vec, "mu_wag": mu_shift[None, 3 * C_HEADS:],
        "lora": lora, "seg": seg, "w_out": w_out.astype(BF),
        "w_gate": w_gate.astype(BF), "w_up": w_up.astype(BF),
        "w_down": w_down.astype(BF),
        "w_ple_gate": w_ple_gate.astype(BF), "w_ple_proj": w_ple_proj.astype(BF), "ln": ln,
    }


def _alibi_slopes():
    h = jnp.arange(1, N_HEADS + 1, dtype=F32)
    return jnp.exp2(-8.0 * h / N_HEADS)


def _layer(x, p_l, wts, alpha, shift_prev, wkv_prev, cache_k=None, cache_v=None):
    b, seq, _ = x.shape
    x2d = x.reshape(b * seq, D_MODEL)
    z = _in_proj(x2d, wts["w_in"])
    z3 = z.reshape(b, seq, D_IN)
    slopes = _alibi_slopes()
    if cache_k is None:
        keep = min(WINDOW_MAX, seq)
        att, k_win, v_win = _attn_prompt(z3, slopes, keep)
        from_cm = lambda c: jnp.transpose(c.reshape(b, N_HEADS, HEAD_DIM, keep), (0, 3, 1, 2))
        k_win = from_cm(k_win)
        v_win = from_cm(v_win)
    else:
        cache_len = cache_k.shape[1]
        to_cm = lambda c: jnp.transpose(c, (0, 2, 3, 1)).reshape(b, C_HEADS, cache_len)
        from_cm = lambda c: jnp.transpose(c.reshape(b, N_HEADS, HEAD_DIM, cache_len), (0, 3, 1, 2))
        att, k_win, v_win = _attn_sample(z3, to_cm(cache_k), to_cm(cache_v), slopes)
        k_win = from_cm(k_win)
        v_win = from_cm(v_win)
    if seq % SUPER == 0:
        nsb = RWKV_BLOCKS if seq % (RWKV_BLOCKS * SUPER) == 0 else 1
        rw, m_last = _rwkv(z3, z3, shift_prev, wkv_prev, wts, seq, nsb, CHUNK,
                           False)
    else:
        chunk = 8
        per_block = SUPER // chunk
        assert seq <= chunk and b % per_block == 0
        first_prev = jnp.concatenate([jnp.zeros((b, 1, 3 * C_HEADS), F32), shift_prev[:, None, :]],
                                     axis=-1)
        prev = jnp.concatenate([first_prev, z3[:, :seq - 1]], axis=1)
        blocks = lambda a: jnp.pad(a, ((0, 0), (0, chunk - seq), (0, 0))).reshape(
            b // per_block, SUPER, D_IN)
        rw, m_last = _rwkv(blocks(z3), blocks(prev), jnp.zeros((b // per_block, D_B_IN), F32),
                           wkv_prev, wts, seq, 1, chunk, True)
        rw = rw.reshape(b, chunk, C_HEADS)[:, :seq]
    y = _post(att.reshape(b * seq, C_HEADS), rw.reshape(b * seq, C_HEADS), x2d,
              p_l.reshape(b * seq, PLE_DIM), wts, alpha)
    shift_new = z3[:, seq - 1, 3 * C_HEADS:]
    return y.reshape(b, seq, D_MODEL), k_win, v_win, shift_new, m_last


def kernel(x_prompt, x_sample, p_prompt, p_sample, cache_k_win, cache_v_win, state_wkv, state_shift, w_in, mu_shift, w0, w2, a0, a2, g2, k_k, k_a, r_k, lnx_w, lnx_b, w_out, ln1_w, ln1_b, w_gate, w_up, w_down, ln2_w, ln2_b, w_ple_gate, w_ple_proj, ln3_w, ln3_b):
    depth = w_in.shape[0]
    alpha = float((2 * depth) ** 0.25)
    xp, xs = x_prompt, x_sample
    bp = x_prompt.shape[0]
    shift0 = jnp.zeros((bp, D_B_IN), x_prompt.dtype)
    wkv0 = jnp.zeros((bp, N_HEADS, HEAD_DIM, HEAD_DIM), state_wkv.dtype)
    outs = [[] for _ in range(8)]
    for l in range(depth):
        wts = _pack_layer(w_in[l], mu_shift[l], w0[l], w2[l], a0[l], a2[l], g2[l], k_k[l], k_a[l],
                          r_k[l], lnx_w[l], lnx_b[l], w_out[l], ln1_w[l], ln1_b[l], w_gate[l],
                          w_up[l], w_down[l], ln2_w[l], ln2_b[l], w_ple_gate[l], w_ple_proj[l],
                          ln3_w[l], ln3_b[l])
        xp, kw, vw, sh, wk = _layer(xp, p_prompt[l], wts, alpha, shift0, wkv0)
        for lst, val in zip(outs[0:4], (kw, vw, wk, sh)):
            lst.append(val)
        xs, kw, vw, sh, wk = _layer(xs, p_sample[l], wts, alpha, state_shift[l], state_wkv[l],
                                    cache_k_win[l], cache_v_win[l])
        for lst, val in zip(outs[4:8], (kw, vw, wk, sh)):
            lst.append(val)
    return (xp, xs) + tuple(jnp.stack(o) for o in outs)
```
